```python
import math
import jax, jax.numpy as jnp
from jax import lax
import numpy as np

D_MODEL = 1024
BATCH = 32
SEQ = 256
DEPTH = 4
DEC_BATCH = 4
DEC_SEQ = 1024
PAST_LEN = 512

GRID_W = 64
S5_GROUP_CH = 16
S5_STATE = 64
S5_GROUPS = D_MODEL // S5_GROUP_CH
N_DIR = 2
GM_CHUNK = 128
GM_WIDTH = D_MODEL
GM_HEADS = 8
GM_HEAD_CH = GM_WIDTH // GM_HEADS
FFN_DIM = 2816
N_EXPERTS = 8
TOP_K = 2
EXPERT_DIM = 3584
N_S5_LAYERS = (DEPTH + 1) // 2
N_GM_LAYERS = DEPTH // 2
ALPHA = (2.0 * DEPTH) ** 0.25
BETA = (8.0 * DEPTH) ** -0.25
LN_EPS = 1e-5

kernel_name = "hybrid_s5_gmlp_diffusion_step"


def layer_norm(x, g, b):
    xf = x.astype(jnp.float32)
    mu = jnp.mean(xf, axis=-1, keepdims=True)
    var = jnp.mean(jnp.square(xf - mu), axis=-1, keepdims=True)
    y = (xf - mu) * lax.rsqrt(var + LN_EPS) * g.astype(jnp.float32) + b.astype(jnp.float32)
    return y.astype(x.dtype)


def grid_pos_embed(n_tokens, dtype):
    rows = n_tokens // GRID_W
    r, col = jnp.meshgrid(jnp.arange(rows, dtype=jnp.float32), jnp.arange(GRID_W, dtype=jnp.float32), indexing="ij")
    quarter = D_MODEL // 4
    freq = 1.0 / (10000.0 ** (jnp.arange(quarter, dtype=jnp.float32) / quarter))

    def emb(p):
        ang = p.reshape(-1)[:, None] * freq[None, :]
        return jnp.concatenate([jnp.sin(ang), jnp.cos(ang)], axis=-1)

    return jnp.concatenate([emb(r), emb(col)], axis=-1).astype(dtype)


def s5_discretize(a_re, a_im, log_step, b_re, b_im):
    lam_re = jnp.minimum(a_re.astype(jnp.float32), -1e-4)
    lam_im = a_im.astype(jnp.float32)
    dt = jnp.exp(log_step.astype(jnp.float32))[:, None]
    mag = jnp.exp(lam_re * dt)
    abar_re = mag * jnp.cos(lam_im * dt)
    abar_im = mag * jnp.sin(lam_im * dt)
    nr = abar_re - 1.0
    ni = abar_im
    den = lam_re * lam_re + lam_im * lam_im
    q_re = (nr * lam_re + ni * lam_im) / den
    q_im = (ni * lam_re - nr * lam_im) / den
    b_re = b_re.astype(jnp.float32)
    b_im = b_im.astype(jnp.float32)
    bb_re = q_re[..., None] * b_re - q_im[..., None] * b_im
    bb_im = q_re[..., None] * b_im + q_im[..., None] * b_re
    return abar_re, abar_im, bb_re, bb_im


def _complex_affine_combine(e1, e2):
    ar1, ai1, br1, bi1 = e1
    ar2, ai2, br2, bi2 = e2
    return (ar2 * ar1 - ai2 * ai1,
            ar2 * ai1 + ai2 * ar1,
            ar2 * br1 - ai2 * bi1 + br2,
            ar2 * bi1 + ai2 * br1 + bi2)


def s5_scan(u, abar_re, abar_im, bb_re, bb_im, c_re, c_im, h0_re, h0_im):
    bu_re = jnp.einsum("gpc,blgc->blgp", bb_re, u)
    bu_im = jnp.einsum("gpc,blgc->blgp", bb_im, u)
    bu_re = bu_re.at[:, 0].add(abar_re * h0_re - abar_im * h0_im)
    bu_im = bu_im.at[:, 0].add(abar_re * h0_im + abar_im * h0_re)
    a_re_b = jnp.broadcast_to(abar_re, bu_re.shape)
    a_im_b = jnp.broadcast_to(abar_im, bu_re.shape)
    _, _, h_re, h_im = lax.associative_scan(_complex_affine_combine, (a_re_b, a_im_b, bu_re, bu_im), axis=1)
    c_re = c_re.astype(jnp.float32)
    c_im = c_im.astype(jnp.float32)
    y = jnp.einsum("gcp,blgp->blgc", c_re, h_re) - jnp.einsum("gcp,blgp->blgc", c_im, h_im)
    return y, h_re[:, -1], h_im[:, -1]


def s5_mixer(h, a_re, a_im, log_step, b_re, b_im, c_re, c_im, d, w_glu, h0_re, h0_im):
    bsz, seq_len, _ = h.shape
    u = h.astype(jnp.float32).reshape(bsz, seq_len, S5_GROUPS, S5_GROUP_CH)
    y_sum = jnp.zeros_like(u)
    fin_re, fin_im = [], []
    for k in range(N_DIR):
        abar_re, abar_im, bb_re, bb_im = s5_discretize(a_re[k], a_im[k], log_step[k], b_re[k], b_im[k])
        uk = u if k == 0 else jnp.flip(u, axis=1)
        yk, fr, fi = s5_scan(uk, abar_re, abar_im, bb_re, bb_im, c_re[k], c_im[k], h0_re[:, k], h0_im[:, k])
        y_sum = y_sum + (yk if k == 0 else jnp.flip(yk, axis=1))
        fin_re.append(fr)
        fin_im.append(fi)
    y = y_sum.reshape(bsz, seq_len, D_MODEL) + d.astype(jnp.float32) * h.astype(jnp.float32)
    y = jax.nn.gelu(y).astype(h.dtype)
    val, gate = jnp.split(y @ w_glu, 2, axis=-1)
    out = val * jax.nn.sigmoid(gate)
    return out, jnp.stack(fin_re, axis=1), jnp.stack(fin_im, axis=1)


def gmlp_mixer(h, w_in, b_in, ln_g, ln_b, w_s, b_s, w_out):
    bsz, seq_len, _ = h.shape
    z = jax.nn.gelu(h @ w_in + b_in)
    u, v = jnp.split(z, 2, axis=-1)
    v = layer_norm(v, ln_g, ln_b)
    vc = v.reshape(bsz, seq_len // GM_CHUNK, GM_CHUNK, GM_HEADS, GM_HEAD_CH)
    s = jnp.einsum("hpq,bnqhc->bnphc", w_s, vc) + b_s.T[None, None, :, :, None]
    return (u * s.reshape(bsz, seq_len, GM_WIDTH)) @ w_out


def swiglu(h, w_gate, w_up, w_down):
    return (jax.nn.silu(h @ w_gate) * (h @ w_up)) @ w_down


def moe_swiglu(h, w_router, b_router, w_gate, w_up, w_down):
    bsz, seq_len, dm = h.shape
    t = h.reshape(-1, dm)
    logits = (t @ w_router).astype(jnp.float32) + b_router.astype(jnp.float32)
    top_vals, top_idx = lax.top_k(logits, TOP_K)
    probs = jax.nn.softmax(top_vals, axis=-1)
    gates = jnp.sum(jax.nn.one_hot(top_idx, N_EXPERTS, dtype=jnp.float32) * probs[..., None], axis=1).astype(t.dtype)
    out = jnp.zeros_like(t)
    for e in range(N_EXPERTS):
        out = out + gates[:, e:e + 1] * swiglu(t, w_gate[e], w_up[e], w_down[e])
    return out.reshape(bsz, seq_len, dm)


def setup_inputs(seed: int = 0) -> dict:
    key = jax.random.key(seed)
    ks = iter(jax.random.split(key, 40))
    f32 = jnp.float32

    def nrm(shape, scale):
        return jax.random.normal(next(ks), shape, f32) * scale

    D = D_MODEL
    inv = D ** -0.5
    x_prompt = nrm((BATCH, SEQ, D), 1.0)
    x_sample = nrm((DEC_BATCH, DEC_SEQ, D), 1.0)
    c = nrm((DEC_BATCH, D), 1.0)
    state_ssm_re = nrm((DEC_BATCH, N_S5_LAYERS, N_DIR, S5_GROUPS, S5_STATE), 0.1)
    state_ssm_im = nrm((DEC_BATCH, N_S5_LAYERS, N_DIR, S5_GROUPS, S5_STATE), 0.1)
    c_ctx = nrm((D,), 1.0)
    w_ada = nrm((DEPTH, D, 6 * D), 0.1 * inv)
    b_ada = nrm((DEPTH, 6 * D), 0.01)
    ln_g = 1.0 + nrm((DEPTH, 2, D), 0.01)
    ln_b = nrm((DEPTH, 2, D), 0.01)
    s5_shape = (N_S5_LAYERS, N_DIR, S5_GROUPS, S5_STATE)
    s5_a_re = -0.5 + nrm(s5_shape, 0.01)
    s5_a_im = jnp.pi * jnp.broadcast_to(jnp.arange(S5_STATE, dtype=f32), s5_shape) + nrm(s5_shape, 0.01)
    s5_log_step = jax.random.uniform(next(ks), (N_S5_LAYERS, N_DIR, S5_GROUPS), f32, math.log(1e-3), math.log(1e-1))
    s5_b_re = nrm((N_S5_LAYERS, N_DIR, S5_GROUPS, S5_STATE, S5_GROUP_CH), (2 * S5_GROUP_CH) ** -0.5)
    s5_b_im = nrm((N_S5_LAYERS, N_DIR, S5_GROUPS, S5_STATE, S5_GROUP_CH), (2 * S5_GROUP_CH) ** -0.5)
    s5_c_re = nrm((N_S5_LAYERS, N_DIR, S5_GROUPS, S5_GROUP_CH, S5_STATE), S5_STATE ** -0.5)
    s5_c_im = nrm((N_S5_LAYERS, N_DIR, S5_GROUPS, S5_GROUP_CH, S5_STATE), S5_STATE ** -0.5)
    s5_d = nrm((N_S5_LAYERS, D), 1.0)
    s5_w_glu = jnp.concatenate([nrm((N_S5_LAYERS, D, D), inv * BETA), nrm((N_S5_LAYERS, D, D), inv)], axis=-1)
    gm_w_in = nrm((N_GM_LAYERS, D, 2 * GM_WIDTH), inv)
    gm_b_in = nrm((N_GM_LAYERS, 2 * GM_WIDTH), 0.02)
    gm_ln_g = 1.0 + nrm((N_GM_LAYERS, GM_WIDTH), 0.01)
    gm_ln_b = nrm((N_GM_LAYERS, GM_WIDTH), 0.01)
    gm_w_s = nrm((N_GM_LAYERS, GM_HEADS, GM_CHUNK, GM_CHUNK), GM_CHUNK ** -0.5)
    gm_b_s = nrm((N_GM_LAYERS, GM_HEADS, GM_CHUNK), 0.02)
    gm_w_out = nrm((N_GM_LAYERS, GM_WIDTH, D), GM_WIDTH ** -0.5 * BETA)
    ffn_w_gate = nrm((N_S5_LAYERS, D, FFN_DIM), inv)
    ffn_w_up = nrm((N_S5_LAYERS, D, FFN_DIM), inv)
    ffn_w_down = nrm((N_S5_LAYERS, FFN_DIM, D), FFN_DIM ** -0.5 * BETA)
    moe_w_router = nrm((N_GM_LAYERS, D, N_EXPERTS), inv)
    moe_b_router = nrm((N_GM_LAYERS, N_EXPERTS), 0.01)
    moe_w_gate = nrm((N_GM_LAYERS, N_EXPERTS, D, EXPERT_DIM), inv)
    moe_w_up = nrm((N_GM_LAYERS, N_EXPERTS, D, EXPERT_DIM), inv)
    moe_w_down = nrm((N_GM_LAYERS, N_EXPERTS, EXPERT_DIM, D), EXPERT_DIM ** -0.5 * BETA)
    return {"x_prompt": x_prompt, "x_sample": x_sample, "c": c,
            "state_ssm_re": state_ssm_re, "state_ssm_im": state_ssm_im, "c_ctx": c_ctx,
            "w_ada": w_ada, "b_ada": b_ada, "ln_g": ln_g, "ln_b": ln_b,
            "s5_a_re": s5_a_re, "s5_a_im": s5_a_im, "s5_log_step": s5_log_step,
            "s5_b_re": s5_b_re, "s5_b_im": s5_b_im, "s5_c_re": s5_c_re, "s5_c_im": s5_c_im,
            "s5_d": s5_d, "s5_w_glu": s5_w_glu,
            "gm_w_in": gm_w_in, "gm_b_in": gm_b_in, "gm_ln_g": gm_ln_g, "gm_ln_b": gm_ln_b,
            "gm_w_s": gm_w_s, "gm_b_s": gm_b_s, "gm_w_out": gm_w_out,
            "ffn_w_gate": ffn_w_gate, "ffn_w_up": ffn_w_up, "ffn_w_down": ffn_w_down,
            "moe_w_router": moe_w_router, "moe_b_router": moe_b_router,
            "moe_w_gate": moe_w_gate, "moe_w_up": moe_w_up, "moe_w_down": moe_w_down}


def reference(x_prompt, x_sample, c, state_ssm_re, state_ssm_im, c_ctx,
              w_ada, b_ada, ln_g, ln_b,
              s5_a_re, s5_a_im, s5_log_step, s5_b_re, s5_b_im, s5_c_re, s5_c_im, s5_d, s5_w_glu,
              gm_w_in, gm_b_in, gm_ln_g, gm_ln_b, gm_w_s, gm_b_s, gm_w_out,
              ffn_w_gate, ffn_w_up, ffn_w_down,
              moe_w_router, moe_b_router, moe_w_gate, moe_w_up, moe_w_down):

    def run_trunk(x, cond, h0_re_all, h0_im_all):
        silu_c = jax.nn.silu(cond)
        fin_re, fin_im = [], []
        for i in range(DEPTH):
            j = i // 2
            mod = (silu_c @ w_ada[i] + b_ada[i])[:, None, :]
            sh1, sc1, g1, sh2, sc2, g2 = jnp.split(mod, 6, axis=-1)
            h = x * (1.0 + sc1) + sh1
            if i % 2 == 0:
                f, fr, fi = s5_mixer(h, s5_a_re[j], s5_a_im[j], s5_log_step[j], s5_b_re[j], s5_b_im[j],
                                     s5_c_re[j], s5_c_im[j], s5_d[j], s5_w_glu[j],
                                     h0_re_all[:, j], h0_im_all[:, j])
                fin_re.append(fr)
                fin_im.append(fi)
            else:
                f = gmlp_mixer(h, gm_w_in[j], gm_b_in[j], gm_ln_g[j], gm_ln_b[j], gm_w_s[j], gm_b_s[j], gm_w_out[j])
            x = layer_norm(ALPHA * x + (1.0 + g1) * f, ln_g[i, 0], ln_b[i, 0])
            h = x * (1.0 + sc2) + sh2
            if i % 2 == 0:
                g = swiglu(h, ffn_w_gate[j], ffn_w_up[j], ffn_w_down[j])
            else:
                g = moe_swiglu(h, moe_w_router[j], moe_b_router[j], moe_w_gate[j], moe_w_up[j], moe_w_down[j])
            x = layer_norm(ALPHA * x + (1.0 + g2) * g, ln_g[i, 1], ln_b[i, 1])
        return x, jnp.stack(fin_re, axis=1), jnp.stack(fin_im, axis=1)

    zero_state = jnp.zeros((x_prompt.shape[0], N_S5_LAYERS, N_DIR, S5_GROUPS, S5_STATE), jnp.float32)
    y_prompt, new_state_ssm_re, new_state_ssm_im = run_trunk(x_prompt, c_ctx[None, :], zero_state, zero_state)

    x_lat = x_sample + grid_pos_embed(x_sample.shape[1], x_sample.dtype)[None]
    y_sample, _, _ = run_trunk(x_lat, c, state_ssm_re.astype(jnp.float32), state_ssm_im.astype(jnp.float32))

    return (y_prompt, y_sample, new_state_ssm_re.astype(x_prompt.dtype), new_state_ssm_im.astype(x_prompt.dtype))
```

```python
import functools

import jax
import jax.numpy as jnp
from jax import lax
from jax.experimental import pallas as pl
from jax.experimental.pallas import tpu as pltpu

F32 = jnp.float32
BF16 = jnp.bfloat16

D = 1024
BATCH, SEQ = 32, 256
DEC_BATCH, DEC_SEQ = 4, 1024
DEPTH = 4
GRID_W = 64
N_P = BATCH * SEQ
N_S = DEC_BATCH * DEC_SEQ
N_ROWS = N_P + N_S
N_COND = 8

S5_CG, S5_P, S5_G = 16, 64, 64
S5_T = 16
S5_Q = S5_G // 2
P_CHUNKS = SEQ // S5_T
S_CHUNKS = DEC_SEQ // S5_T
S5_ROWS_P = P_CHUNKS * BATCH
S5_ROWS_S = S_CHUNKS * DEC_BATCH
S5_ROWS = S5_ROWS_P + S5_ROWS_S

GM_CHUNK, GM_HEADS = 128, 8
FFN_DIM = 2816
N_EXPERTS, TOP_K, EXPERT_DIM = 8, 2, 3584
ALPHA = (2.0 * DEPTH) ** 0.25
LN_EPS = 1e-5

TM = 512
MOE_TM = 512
MOE_TF = 896
MOE_TILES = (N_ROWS * TOP_K) // MOE_TM + N_EXPERTS
MOE_ROWS = MOE_TILES * MOE_TM
VMEM_LIMIT = 56 * 1024 * 1024


def _cparams(sem):
    return pltpu.CompilerParams(dimension_semantics=sem, vmem_limit_bytes=VMEM_LIMIT)


def _cond_of_tile(t, tm):
    row0 = t * tm
    return jnp.where(row0 < N_P, 0, 1 + (row0 - N_P) // DEC_SEQ)


def _mod_spec(tm):
    return pl.BlockSpec((None, 6, D), lambda t: (_cond_of_tile(t, tm), 0, 0))


def _row_spec(tm, width=D):
    return pl.BlockSpec((tm, width), lambda t: (t, 0))


def _const_spec(shape):
    return pl.BlockSpec(shape, lambda t: (0,) * len(shape))


def _layer_norm(r, g, b):
    mu = jnp.mean(r, axis=-1, keepdims=True)
    c = r - mu
    var = jnp.mean(c * c, axis=-1, keepdims=True)
    return c * lax.rsqrt(var + LN_EPS) * g + b


def _modulate(x, mod_ref, shift_row, scale_row):
    return x * (1.0 + mod_ref[scale_row:scale_row + 1, :]) + mod_ref[shift_row:shift_row + 1, :]


def _ada_kernel(c_ref, w_ref, b_ref, o_ref):
    c = c_ref[...]
    o_ref[...] = jnp.dot(jax.nn.silu(c), w_ref[...], preferred_element_type=F32) + b_ref[...]


def _ada(cond8, w_ada, b_ada):
    tn = 1536
    return pl.pallas_call(
        _ada_kernel,
        grid=(DEPTH, 6 * D // tn),
        in_specs=[pl.BlockSpec((N_COND, D), lambda i, n: (0, 0)),
                  pl.BlockSpec((None, D, tn), lambda i, n: (i, 0, n)),
                  pl.BlockSpec((None, 1, tn), lambda i, n: (i, 0, n))],
        out_specs=pl.BlockSpec((None, N_COND, tn), lambda i, n: (i, 0, n)),
        out_shape=jax.ShapeDtypeStruct((DEPTH, N_COND, 6 * D), F32),
        compiler_params=_cparams(("arbitrary", "arbitrary")),
        name="ada",
    )(cond8, w_ada, b_ada.reshape(DEPTH, 1, 6 * D))


def _prep_kernel(x_ref, pos_ref, mod_ref, xo_ref, h_ref):
    is_latent = pl.program_id(0) * TM >= N_P
    x = x_ref[...]
    x = jnp.where(is_latent, x + pos_ref[...], x)
    xo_ref[...] = x
    h_ref[...] = _modulate(x, mod_ref, 0, 1).astype(BF16)


def _prep(x_cat, pos, mod):
    def pos_map(t):
        row0 = t * TM
        return (jnp.where(row0 >= N_P, ((row0 - N_P) % DEC_SEQ) // TM, 0), 0)

    return pl.pallas_call(
        _prep_kernel,
        grid=(N_ROWS // TM,),
        in_specs=[_row_spec(TM), pl.BlockSpec((TM, D), pos_map), _mod_spec(TM)],
        out_specs=[_row_spec(TM), _row_spec(TM)],
        out_shape=[jax.ShapeDtypeStruct((N_ROWS, D), F32), jax.ShapeDtypeStruct((N_ROWS, D), BF16)],
        compiler_params=_cparams(("parallel",)),
        name="prep",
    )(x_cat, pos, mod)


def _s5_kernel(u_ref, mi_ref, min_ref, mout_ref, a_ref, h0_ref, y_ref, fin_ref, s_scr, hin_scr):
    u = u_ref[...]
    half = 2 * S5_P
    y = None
    for k in range(2):
        s_scr[...] = jnp.dot(u, min_ref[k], preferred_element_type=F32)
        ar_p = jnp.broadcast_to(a_ref[k, 0:1, :], (BATCH, half))
        ai_p = jnp.broadcast_to(a_ref[k, 1:2, :], (BATCH, half))
        ar_s = ar_p[:DEC_BATCH]
        ai_s = ai_p[:DEC_BATCH]

        def run(row0, nb, nchunks, hre, him, ar, ai):
            order = range(nchunks - 1, -1, -1) if k == 1 else range(nchunks)
            for n in order:
                lo = row0 + n * nb
                hin_scr[lo:lo + nb, 0:half] = hre
                hin_scr[lo:lo + nb, half:2 * half] = him
                sre = s_scr[lo:lo + nb, 0:half]
                sim = s_scr[lo:lo + nb, half:2 * half]
                hre, him = ar * hre - ai * him + sre, ar * him + ai * hre + sim
            return hre, him

        zero = jnp.zeros((BATCH, half), F32)
        fre, fim = run(0, BATCH, P_CHUNKS, zero, zero, ar_p, ai_p)
        fin_ref[k, :, 0:half] = fre
        fin_ref[k, :, half:2 * half] = fim
        run(S5_ROWS_P, DEC_BATCH, S_CHUNKS, h0_ref[k, :, 0:half], h0_ref[k, :, half:2 * half], ar_s, ai_s)

        intra = jnp.concatenate(
            [jnp.dot(u[:, 0:256], mi_ref[k, 0], preferred_element_type=F32),
             jnp.dot(u[:, 256:512], mi_ref[k, 1], preferred_element_type=F32)], axis=-1)
        yk = intra + jnp.dot(hin_scr[...].astype(BF16), mout_ref[k], preferred_element_type=F32)
        y = yk if y is None else y + yk
    y_ref[...] = y


def _s5(u_t, mats, h0):
    mi, m_in, m_out, a16 = mats
    return pl.pallas_call(
        _s5_kernel,
        grid=(S5_Q,),
        in_specs=[pl.BlockSpec((None, S5_ROWS, 512), lambda q: (q, 0, 0)),
                  pl.BlockSpec((None, 2, 2, 256, 256), lambda q: (q, 0, 0, 0, 0)),
                  pl.BlockSpec((None, 2, 512, 256), lambda q: (q, 0, 0, 0)),
                  pl.BlockSpec((None, 2, 256, 512), lambda q: (q, 0, 0, 0)),
                  pl.BlockSpec((None, 2, 2, 128), lambda q: (q, 0, 0, 0)),
                  pl.BlockSpec((None, 2, DEC_BATCH, 256), lambda q: (q, 0, 0, 0))],
        out_specs=[pl.BlockSpec((None, S5_ROWS, 512), lambda q: (q, 0, 0)),
                   pl.BlockSpec((None, 2, BATCH, 256), lambda q: (q, 0, 0, 0))],
        out_shape=[jax.ShapeDtypeStruct((S5_Q, S5_ROWS, 512), F32),
                   jax.ShapeDtypeStruct((S5_Q, 2, BATCH, 256), F32)],
        scratch_shapes=[pltpu.VMEM((S5_ROWS, 256), F32), pltpu.VMEM((S5_ROWS, 256), F32)],
        compiler_params=_cparams(("parallel",)),
        name="s5",
    )(u_t, mi, m_in, m_out, a16, h0)


def _s5_matrices(a_re, a_im, log_step, b_re, b_im, c_re, c_im):
    hp = lax.Precision.HIGHEST
    lam_re = jnp.minimum(a_re.astype(F32), -1e-4)
    lam_im = a_im.astype(F32)
    dt = jnp.exp(log_step.astype(F32))[..., None]
    j = jnp.arange(S5_T + 1, dtype=F32)[:, None, None, None]
    mag = jnp.exp(lam_re * dt * j)
    pw_re = mag * jnp.cos(lam_im * dt * j)
    pw_im = mag * jnp.sin(lam_im * dt * j)
    nr, ni = pw_re[1] - 1.0, pw_im[1]
    den = lam_re * lam_re + lam_im * lam_im
    q_re = (nr * lam_re + ni * lam_im) / den
    q_im = (ni * lam_re - nr * lam_im) / den
    b_re, b_im = b_re.astype(F32), b_im.astype(F32)
    bb_re = q_re[..., None] * b_re - q_im[..., None] * b_im
    bb_im = q_re[..., None] * b_im + q_im[..., None] * b_re
    c_re, c_im = c_re.astype(F32), c_im.astype(F32)
    ab_re = pw_re[:S5_T, ..., None] * bb_re - pw_im[:S5_T, ..., None] * bb_im
    ab_im = pw_re[:S5_T, ..., None] * bb_im + pw_im[:S5_T, ..., None] * bb_re
    kern = (jnp.einsum("kgcp,tkgpd->tkgcd", c_re, ab_re, precision=hp)
            - jnp.einsum("kgcp,tkgpd->tkgcd", c_im, ab_im, precision=hp))
    r = jnp.arange(S5_T)[:, None]
    s = jnp.arange(S5_T)[None, :]
    lag = jnp.clip(s - r, 0, S5_T - 1)
    toe = jnp.where((s >= r)[:, :, None, None, None, None], kern[lag], 0.0)
    m_intra = toe.transpose(2, 3, 0, 5, 1, 4)
    inj_re = ab_re[::-1].transpose(1, 2, 0, 4, 3)
    inj_im = ab_im[::-1].transpose(1, 2, 0, 4, 3)
    ca_re = c_re[None] * pw_re[1:, :, :, None, :] - c_im[None] * pw_im[1:, :, :, None, :]
    ca_im = c_re[None] * pw_im[1:, :, :, None, :] + c_im[None] * pw_re[1:, :, :, None, :]
    out_re = ca_re.transpose(1, 2, 4, 0, 3)
    out_im = -ca_im.transpose(1, 2, 4, 0, 3)

    def both_dirs(m, time_axes):
        return jnp.stack([m[0], jnp.flip(m[1], axis=[a - 1 for a in time_axes])], axis=0)

    m_intra = both_dirs(m_intra, (2, 4))
    inj_re, inj_im = both_dirs(inj_re, (2,)), both_dirs(inj_im, (2,))
    out_re, out_im = both_dirs(out_re, (3,)), both_dirs(out_im, (3,))

    w = S5_T * S5_CG
    mi = m_intra.reshape(2, S5_Q, 2, w, w).transpose(1, 0, 2, 3, 4).astype(BF16)
    eye = jnp.eye(2, dtype=F32)
    inj = jnp.stack([inj_re, inj_im], axis=0).reshape(2, 2, S5_Q, 2, w, S5_P)
    m_in = jnp.einsum("rkqgwp,gh->qkgwrhp", inj, eye).reshape(S5_Q, 2, 2 * w, 4 * S5_P).astype(BF16)
    out = jnp.stack([out_re, out_im], axis=0).reshape(2, 2, S5_Q, 2, S5_P, w)
    m_out = jnp.einsum("rkqgpw,gh->qkrgphw", out, eye).reshape(S5_Q, 2, 4 * S5_P, 2 * w).astype(BF16)
    a16 = jnp.stack([pw_re[S5_T], pw_im[S5_T]], axis=0).reshape(2, 2, S5_Q, 2 * S5_P).transpose(2, 1, 0, 3)
    return mi, m_in, m_out, a16


def _s5_to_chunks(h):
    def part(hp, nb, nchunks):
        hp = hp.reshape(nb, nchunks, S5_T, S5_Q, 2, S5_CG)
        return hp.transpose(3, 1, 0, 4, 2, 5).reshape(S5_Q, nchunks * nb, 2 * S5_T * S5_CG)
    return jnp.concatenate([part(h[:N_P], BATCH, P_CHUNKS), part(h[N_P:], DEC_BATCH, S_CHUNKS)], axis=1)


def _s5_from_chunks(y):
    def part(yp, nb, nchunks):
        yp = yp.reshape(S5_Q, nchunks, nb, 2, S5_T, S5_CG)
        return yp.transpose(2, 1, 4, 0, 3, 5).reshape(nb * nchunks * S5_T, D)
    return jnp.concatenate([part(y[:, :S5_ROWS_P], BATCH, P_CHUNKS), part(y[:, S5_ROWS_P:], DEC_BATCH, S_CHUNKS)], axis=0)


def _s5_state_in(st_re, st_im):
    def lay(s):
        return s.astype(F32).reshape(DEC_BATCH, 2, S5_Q, 2 * S5_P).transpose(2, 1, 0, 3)
    return jnp.concatenate([lay(st_re), lay(st_im)], axis=-1)


def _s5_state_out(fin):
    def lay(s):
        return s.transpose(2, 1, 0, 3).reshape(BATCH, 2, S5_G, S5_P)
    return lay(fin[..., :2 * S5_P]), lay(fin[..., 2 * S5_P:])


def _glu_kernel(ys_ref, x_ref, mod_ref, d_ref, w_ref, g_ref, b_ref, x1_ref, h2_ref):
    x = x_ref[...]
    h = _modulate(x, mod_ref, 0, 1)
    y = jax.nn.gelu(ys_ref[...] + d_ref[...] * h).astype(BF16)
    z = jnp.dot(y, w_ref[...], preferred_element_type=F32)
    f = z[:, :D] * jax.nn.sigmoid(z[:, D:])
    x1 = _layer_norm(ALPHA * x + (1.0 + mod_ref[2:3, :]) * f, g_ref[...], b_ref[...])
    x1_ref[...] = x1
    h2_ref[...] = _modulate(x1, mod_ref, 3, 4).astype(BF16)


def _glu(ysum, x, mod, d, w_glu, ln_g, ln_b):
    return pl.pallas_call(
        _glu_kernel,
        grid=(N_ROWS // TM,),
        in_specs=[_row_spec(TM), _row_spec(TM), _mod_spec(TM), _const_spec((1, D)),
                  _const_spec((D, 2 * D)), _const_spec((1, D)), _const_spec((1, D))],
        out_specs=[_row_spec(TM), _row_spec(TM)],
        out_shape=[jax.ShapeDtypeStruct((N_ROWS, D), F32), jax.ShapeDtypeStruct((N_ROWS, D), BF16)],
        compiler_params=_cparams(("parallel",)),
        name="glu",
    )(ysum, x, mod, d.reshape(1, D), w_glu, ln_g.reshape(1, D), ln_b.reshape(1, D))


def _finish(x1, g, mod_ref, nmod_ref, g_ref, b_ref, x2_ref, hn_ref):
    x2 = _layer_norm(ALPHA * x1 + (1.0 + mod_ref[5:6, :]) * g, g_ref[...], b_ref[...])
    x2_ref[...] = x2
    hn_ref[...] = _modulate(x2, nmod_ref, 0, 1).astype(BF16)


FFN_TF = FFN_DIM // 2


def _ffn_kernel(h_ref, x1_ref, mod_ref, nmod_ref, wg_ref, wu_ref, wd_ref, g_ref, b_ref, x2_ref, hn_ref):
    h = h_ref[...]
    acc = None
    for c in range(FFN_DIM // FFN_TF):
        sl = slice(c * FFN_TF, (c + 1) * FFN_TF)
        a = jnp.dot(h, wg_ref[:, sl], preferred_element_type=F32)
        b = jnp.dot(h, wu_ref[:, sl], preferred_element_type=F32)
        p = jnp.dot((jax.nn.silu(a) * b).astype(BF16), wd_ref[sl, :], preferred_element_type=F32)
        acc = p if acc is None else acc + p
    _finish(x1_ref[...], acc, mod_ref, nmod_ref, g_ref, b_ref, x2_ref, hn_ref)


def _ffn(h2, x1, mod, nmod, wg, wu, wd, ln_g, ln_b):
    single = pl.Buffered(1)
    return pl.pallas_call(
        _ffn_kernel,
        grid=(N_ROWS // TM,),
        in_specs=[_row_spec(TM), _row_spec(TM), _mod_spec(TM), _mod_spec(TM),
                  pl.BlockSpec((D, FFN_DIM), lambda t: (0, 0), pipeline_mode=single),
                  pl.BlockSpec((D, FFN_DIM), lambda t: (0, 0), pipeline_mode=single),
                  pl.BlockSpec((FFN_DIM, D), lambda t: (0, 0), pipeline_mode=single),
                  _const_spec((1, D)), _const_spec((1, D))],
        out_specs=[_row_spec(TM), _row_spec(TM)],
        out_shape=[jax.ShapeDtypeStruct((N_ROWS, D), F32), jax.ShapeDtypeStruct((N_ROWS, D), BF16)],
        compiler_params=_cparams(("parallel",)),
        name="ffn",
    )(h2, x1, mod, nmod, wg, wu, wd, ln_g.reshape(1, D), ln_b.reshape(1, D))


def _gmlp_kernel(h_ref, x_ref, mod_ref, win_ref, bin_ref, vg_ref, vb_ref, ws_ref, bs_ref, wout_ref,
                 g_ref, b_ref, wr_ref, br_ref, x1_ref, h2_ref, idx_ref, p_ref, s_scr):
    z = jax.nn.gelu(jnp.dot(h_ref[...], win_ref[...], preferred_element_type=F32) + bin_ref[...])
    u = z[:, :D]
    v = _layer_norm(z[:, D:], vg_ref[...], vb_ref[...]).astype(BF16)
    n_chunks = TM // GM_CHUNK
    for hd in range(GM_HEADS):
        cols = slice(hd * GM_CHUNK, (hd + 1) * GM_CHUNK)
        rhs = jnp.concatenate([v[n * GM_CHUNK:(n + 1) * GM_CHUNK, cols] for n in range(n_chunks)], axis=-1)
        s = jnp.dot(ws_ref[hd], rhs, preferred_element_type=F32)
        for n in range(n_chunks):
            s_scr[n * GM_CHUNK:(n + 1) * GM_CHUNK, cols] = s[:, n * GM_CHUNK:(n + 1) * GM_CHUNK] + bs_ref[hd]
    f = jnp.dot((u * s_scr[...]).astype(BF16), wout_ref[...], preferred_element_type=F32)
    x = x_ref[...]
    x1 = _layer_norm(ALPHA * x + (1.0 + mod_ref[2:3, :]) * f, g_ref[...], b_ref[...])
    x1_ref[...] = x1
    h2 = _modulate(x1, mod_ref, 3, 4)
    h2_ref[...] = h2.astype(BF16)
    logits = jnp.dot(h2, wr_ref[...], preferred_element_type=F32, precision=lax.Precision.HIGHEST) + br_ref[...]
    lane = lax.broadcasted_iota(jnp.int32, logits.shape, 1)
    neg = jnp.float32(-jnp.inf)
    logits = jnp.where(lane < N_EXPERTS, logits, neg)
    v1 = jnp.max(logits, axis=-1, keepdims=True)
    i1 = jnp.min(jnp.where(logits == v1, lane, 128), axis=-1, keepdims=True)
    rest = jnp.where(lane == i1, neg, logits)
    v2 = jnp.max(rest, axis=-1, keepdims=True)
    i2 = jnp.min(jnp.where(rest == v2, lane, 128), axis=-1, keepdims=True)
    e2 = jnp.exp(v2 - v1)
    p1 = 1.0 / (1.0 + e2)
    p2 = e2 / (1.0 + e2)
    idx_ref[...] = jnp.where(lane == 0, i1, jnp.where(lane == 1, i2, 0))
    p_ref[...] = jnp.where(lane == 0, p1, jnp.where(lane == 1, p2, 0.0))


def _gmlp(h1, x, mod, w_in, b_in, vg, vb, w_s, b_s, w_out, ln_g, ln_b, w_r, b_r):
    return pl.pallas_call(
        _gmlp_kernel,
        grid=(N_ROWS // TM,),
        in_specs=[_row_spec(TM), _row_spec(TM), _mod_spec(TM),
                  _const_spec((D, 2 * D)), _const_spec((1, 2 * D)), _const_spec((1, D)), _const_spec((1, D)),
                  _const_spec((GM_HEADS, GM_CHUNK, GM_CHUNK)), _const_spec((GM_HEADS, GM_CHUNK, GM_CHUNK)),
                  _const_spec((D, D)), _const_spec((1, D)), _const_spec((1, D)),
                  _const_spec((D, 128)), _const_spec((1, 128))],
        out_specs=[_row_spec(TM), _row_spec(TM), _row_spec(TM, 128), _row_spec(TM, 128)],
        out_shape=[jax.ShapeDtypeStruct((N_ROWS, D), F32), jax.ShapeDtypeStruct((N_ROWS, D), BF16),
                   jax.ShapeDtypeStruct((N_ROWS, 128), jnp.int32), jax.ShapeDtypeStruct((N_ROWS, 128), F32)],
        scratch_shapes=[pltpu.VMEM((TM, D), F32)],
        compiler_params=_cparams(("parallel",)),
        name="gmlp",
    )(h1, x, mod, w_in, b_in.reshape(1, 2 * D), vg.reshape(1, D), vb.reshape(1, D), w_s, b_s, w_out,
      ln_g.reshape(1, D), ln_b.reshape(1, D), w_r, b_r)


def _moe_kernel(te_ref, used_ref, x_ref, wg_ref, wu_ref, wd_ref, y_ref, acc_ref):
    t, f = pl.program_id(0), pl.program_id(1)
    live = t < used_ref[0]

    @pl.when(jnp.logical_and(live, f == 0))
    def _():
        acc_ref[...] = jnp.zeros_like(acc_ref)

    @pl.when(live)
    def _():
        x = x_ref[...]
        a = jnp.dot(x, wg_ref[...], preferred_element_type=F32)
        b = jnp.dot(x, wu_ref[...], preferred_element_type=F32)
        acc_ref[...] += jnp.dot((jax.nn.silu(a) * b).astype(BF16), wd_ref[...], preferred_element_type=F32)

    @pl.when(f == pl.num_programs(1) - 1)
    def _():
        y_ref[...] = jnp.where(live, acc_ref[...], 0.0)


def _moe(tile_expert, n_used, xs, wg, wu, wd):
    nf = EXPERT_DIM // MOE_TF

    def last_f(t, f, used):
        return jnp.where(t < used[0], f, nf - 1)

    def row(t, used):
        return jnp.minimum(t, used[0] - 1)

    grid_spec = pltpu.PrefetchScalarGridSpec(
        num_scalar_prefetch=2,
        grid=(MOE_TILES, nf),
        in_specs=[pl.BlockSpec((MOE_TM, D), lambda t, f, te, used: (row(t, used), 0)),
                  pl.BlockSpec((None, D, MOE_TF), lambda t, f, te, used: (te[t], 0, last_f(t, f, used))),
                  pl.BlockSpec((None, D, MOE_TF), lambda t, f, te, used: (te[t], 0, last_f(t, f, used))),
                  pl.BlockSpec((None, MOE_TF, D), lambda t, f, te, used: (te[t], last_f(t, f, used), 0))],
        out_specs=pl.BlockSpec((MOE_TM, D), lambda t, f, te, used: (t, 0)),
        scratch_shapes=[pltpu.VMEM((MOE_TM, D), F32)],
    )
    return pl.pallas_call(
        _moe_kernel,
        grid_spec=grid_spec,
        out_shape=jax.ShapeDtypeStruct((MOE_ROWS, D), F32),
        compiler_params=_cparams(("arbitrary", "arbitrary")),
        name="moe",
    )(tile_expert, n_used, xs, wg, wu, wd)


def _route(idx):
    e = idx.reshape(-1)
    onehot = (e[:, None] == jnp.arange(N_EXPERTS, dtype=jnp.int32)[None, :]).astype(jnp.int32)
    counts = jnp.sum(onehot, axis=0)
    rank = jnp.sum((jnp.cumsum(onehot, axis=0) - 1) * onehot, axis=1)
    padded = ((counts + MOE_TM - 1) // MOE_TM) * MOE_TM
    ends = jnp.cumsum(padded)
    starts = ends - padded
    pos = starts[e] + rank
    token = jnp.arange(N_ROWS * TOP_K, dtype=jnp.int32) // TOP_K
    src = jnp.zeros((MOE_ROWS,), jnp.int32).at[pos].set(token)
    n_used = (ends[-1] // MOE_TM).astype(jnp.int32)
    tile_start = jnp.minimum(jnp.arange(MOE_TILES, dtype=jnp.int32), n_used - 1) * MOE_TM
    tile_expert = jnp.sum((tile_start[:, None] >= ends[None, :]).astype(jnp.int32), axis=1)
    tile_expert = jnp.minimum(tile_expert, N_EXPERTS - 1).astype(jnp.int32)
    return src, pos.reshape(N_ROWS, TOP_K), tile_expert, n_used.reshape(1)


def _combine_kernel(ya_ref, yb_ref, p_ref, x1_ref, mod_ref, nmod_ref, g_ref, b_ref, x2_ref, hn_ref):
    g = p_ref[:, 0:1] * ya_ref[...] + p_ref[:, 1:2] * yb_ref[...]
    _finish(x1_ref[...], g, mod_ref, nmod_ref, g_ref, b_ref, x2_ref, hn_ref)


def _combine(ya, yb, p, x1, mod, nmod, ln_g, ln_b):
    return pl.pallas_call(
        _combine_kernel,
        grid=(N_ROWS // TM,),
        in_specs=[_row_spec(TM), _row_spec(TM), _row_spec(TM, 128), _row_spec(TM), _mod_spec(TM), _mod_spec(TM),
                  _const_spec((1, D)), _const_spec((1, D))],
        out_specs=[_row_spec(TM), _row_spec(TM)],
        out_shape=[jax.ShapeDtypeStruct((N_ROWS, D), F32), jax.ShapeDtypeStruct((N_ROWS, D), BF16)],
        compiler_params=_cparams(("parallel",)),
        name="combine",
    )(ya, yb, p, x1, mod, nmod, ln_g.reshape(1, D), ln_b.reshape(1, D))


def _grid_pos_embed():
    rows = DEC_SEQ // GRID_W
    r, col = jnp.meshgrid(jnp.arange(rows, dtype=F32), jnp.arange(GRID_W, dtype=F32), indexing="ij")
    quarter = D // 4
    freq = 1.0 / (10000.0 ** (jnp.arange(quarter, dtype=F32) / quarter))

    def emb(p):
        ang = p.reshape(-1)[:, None] * freq[None, :]
        return jnp.concatenate([jnp.sin(ang), jnp.cos(ang)], axis=-1)

    return jnp.concatenate([emb(r), emb(col)], axis=-1)


def kernel(x_prompt, x_sample, c, state_ssm_re, state_ssm_im, c_ctx, w_ada, b_ada, ln_g, ln_b, s5_a_re, s5_a_im, s5_log_step, s5_b_re, s5_b_im, s5_c_re, s5_c_im, s5_d, s5_w_glu, gm_w_in, gm_b_in, gm_ln_g, gm_ln_b, gm_w_s, gm_b_s, gm_w_out, ffn_w_gate, ffn_w_up, ffn_w_down, moe_w_router, moe_b_router, moe_w_gate, moe_w_up, moe_w_down):
    cond8 = jnp.concatenate([c_ctx[None, :], c, jnp.zeros((N_COND - 1 - DEC_BATCH, D), F32)], axis=0)
    mods = _ada(cond8, w_ada, b_ada).reshape(DEPTH, N_COND, 6, D)

    x_cat = jnp.concatenate([x_prompt.reshape(N_P, D), x_sample.reshape(N_S, D)], axis=0)
    x, h = _prep(x_cat, _grid_pos_embed(), mods[0])

    fin_re, fin_im = [], []
    for i in range(DEPTH):
        j = i // 2
        mod = mods[i]
        nmod = mods[min(i + 1, DEPTH - 1)]
        if i % 2 == 0:
            mats = _s5_matrices(s5_a_re[j], s5_a_im[j], s5_log_step[j], s5_b_re[j], s5_b_im[j], s5_c_re[j], s5_c_im[j])
            y_t, fin = _s5(_s5_to_chunks(h), mats, _s5_state_in(state_ssm_re[:, j], state_ssm_im[:, j]))
            fr, fi = _s5_state_out(fin)
            fin_re.append(fr)
            fin_im.append(fi)
            x1, h2 = _glu(_s5_from_chunks(y_t), x, mod, s5_d[j], s5_w_glu[j].astype(BF16), ln_g[i, 0], ln_b[i, 0])
            x, h = _ffn(h2, x1, mod, nmod, ffn_w_gate[j].astype(BF16), ffn_w_up[j].astype(BF16),
                        ffn_w_down[j].astype(BF16), ln_g[i, 1], ln_b[i, 1])
        else:
            b_s = jnp.broadcast_to(gm_b_s[j][:, :, None], (GM_HEADS, GM_CHUNK, GM_CHUNK)).astype(F32)
            w_r = jnp.pad(moe_w_router[j].astype(F32), ((0, 0), (0, 128 - N_EXPERTS)))
            b_r = jnp.pad(moe_b_router[j].astype(F32), (0, 128 - N_EXPERTS)).reshape(1, 128)
            x1, h2, idx, p = _gmlp(h, x, mod, gm_w_in[j].astype(BF16), gm_b_in[j], gm_ln_g[j], gm_ln_b[j],
                                   gm_w_s[j].astype(BF16), b_s, gm_w_out[j].astype(BF16), ln_g[i, 0], ln_b[i, 0], w_r, b_r)
            src, pos, tile_expert, n_used = _route(idx[:, :TOP_K])
            ys = _moe(tile_expert, n_used, jnp.take(h2, src, axis=0), moe_w_gate[j].astype(BF16),
                      moe_w_up[j].astype(BF16), moe_w_down[j].astype(BF16))
            x, h = _combine(jnp.take(ys, pos[:, 0], axis=0), jnp.take(ys, pos[:, 1], axis=0), p, x1, mod, nmod,
                            ln_g[i, 1], ln_b[i, 1])

    y_prompt = x[:N_P].reshape(BATCH, SEQ, D)
    y_sample = x[N_P:].reshape(DEC_BATCH, DEC_SEQ, D)
    new_re = jnp.stack(fin_re, axis=1).astype(x_prompt.dtype)
    new_im = jnp.stack(fin_im, axis=1).astype(x_prompt.dtype)
    return (y_prompt, y_sample, new_re, new_im)
```

```python
import jax
import jax.numpy as jnp
from jax import lax
from jax.experimental import pallas as pl
from jax.experimental.pallas import tpu as pltpu

F32 = jnp.float32
BF16 = jnp.bfloat16

D = 1024
BATCH, SEQ = 32, 256
DEC_BATCH, DEC_SEQ = 4, 1024
DEPTH = 4
GRID_W = 64
N_P = BATCH * SEQ
N_S = DEC_BATCH * DEC_SEQ
N_ROWS = N_P + N_S
N_COND = 8

S5_CG, S5_P, S5_G = 16, 64, 64
S5_T = 16
S5_Q = S5_G // 2
P_CHUNKS = SEQ // S5_T
S_CHUNKS = DEC_SEQ // S5_T
S5_ROWS_P = P_CHUNKS * BATCH
S5_ROWS_S = S_CHUNKS * DEC_BATCH
S5_ROWS = S5_ROWS_P + S5_ROWS_S

GM_CHUNK, GM_HEADS = 128, 8
FFN_DIM = 2816
N_EXPERTS, TOP_K, EXPERT_DIM = 8, 2, 3584
ALPHA = (2.0 * DEPTH) ** 0.25
LN_EPS = 1e-5

TM = 512
MOE_TM = 512
MOE_TF = 896
MOE_TILES = (N_ROWS * TOP_K) // MOE_TM + N_EXPERTS
MOE_ROWS = MOE_TILES * MOE_TM
VMEM_LIMIT = 56 * 1024 * 1024


def _cparams(sem):
    return pltpu.CompilerParams(dimension_semantics=sem, vmem_limit_bytes=VMEM_LIMIT)


def _cond_of_tile(t, tm):
    row0 = t * tm
    return jnp.where(row0 < N_P, 0, 1 + (row0 - N_P) // DEC_SEQ)


def _mod_spec(tm):
    return pl.BlockSpec((None, 6, D), lambda t: (_cond_of_tile(t, tm), 0, 0))


def _row_spec(tm, width=D):
    return pl.BlockSpec((tm, width), lambda t: (t, 0))


def _const_spec(shape):
    return pl.BlockSpec(shape, lambda t: (0,) * len(shape))


def _layer_norm(r, g, b):
    mu = jnp.mean(r, axis=-1, keepdims=True)
    c = r - mu
    var = jnp.mean(c * c, axis=-1, keepdims=True)
    return c * lax.rsqrt(var + LN_EPS) * g + b


def _modulate(x, mod_ref, shift_row, scale_row):
    return x * (1.0 + mod_ref[scale_row:scale_row + 1, :]) + mod_ref[shift_row:shift_row + 1, :]


def _ada_kernel(c_ref, w_ref, b_ref, o_ref):
    c = c_ref[...]
    o_ref[...] = jnp.dot(jax.nn.silu(c), w_ref[...], preferred_element_type=F32) + b_ref[...]


def _ada(cond8, w_ada, b_ada):
    tn = 1536
    return pl.pallas_call(
        _ada_kernel,
        grid=(DEPTH, 6 * D // tn),
        in_specs=[pl.BlockSpec((N_COND, D), lambda i, n: (0, 0)),
                  pl.BlockSpec((None, D, tn), lambda i, n: (i, 0, n)),
                  pl.BlockSpec((None, 1, tn), lambda i, n: (i, 0, n))],
        out_specs=pl.BlockSpec((None, N_COND, tn), lambda i, n: (i, 0, n)),
        out_shape=jax.ShapeDtypeStruct((DEPTH, N_COND, 6 * D), F32),
        compiler_params=_cparams(("arbitrary", "arbitrary")),
        name="ada",
    )(cond8, w_ada, b_ada.reshape(DEPTH, 1, 6 * D))


def _prep_kernel(x_ref, pos_ref, xo_ref):
    is_latent = pl.program_id(0) * TM >= N_P
    x = x_ref[...]
    xo_ref[...] = jnp.where(is_latent, x + pos_ref[...], x)


def _prep(x_cat, pos):
    def pos_map(t):
        row0 = t * TM
        return (jnp.where(row0 >= N_P, ((row0 - N_P) % DEC_SEQ) // TM, 0), 0)

    return pl.pallas_call(
        _prep_kernel,
        grid=(N_ROWS // TM,),
        in_specs=[_row_spec(TM), pl.BlockSpec((TM, D), pos_map)],
        out_specs=_row_spec(TM),
        out_shape=jax.ShapeDtypeStruct((N_ROWS, D), F32),
        compiler_params=_cparams(("parallel",)),
        name="prep",
    )(x_cat, pos)


S5_BLK = 4
S5_W = 2 * S5_T * S5_CG
S5_GR = 2 * S5_CG
_NT = (((1,), (1,)), ((), ()))


def _granule_transpose4(a, lane):
    low = lane < 64
    a0 = jnp.where(low, a[0], pltpu.roll(a[2], 64, 1))
    a2 = jnp.where(low, pltpu.roll(a[0], 64, 1), a[2])
    a1 = jnp.where(low, a[1], pltpu.roll(a[3], 64, 1))
    a3 = jnp.where(low, pltpu.roll(a[1], 64, 1), a[3])
    even = (lane // S5_GR) % 2 == 0
    return [jnp.where(even, a0, pltpu.roll(a1, 32, 1)), jnp.where(even, pltpu.roll(a0, 96, 1), a1),
            jnp.where(even, a2, pltpu.roll(a3, 32, 1)), jnp.where(even, pltpu.roll(a2, 96, 1), a3)]


def _s5_chunk_matrices(p4, bb_ref, c_ref, pw_ref, min_scr, mi_scr, p_scr):
    half = 2 * S5_P
    gran = lax.broadcasted_iota(jnp.int32, (S5_W, 128), 1) // S5_GR
    rep = (lax.broadcasted_iota(jnp.int32, (S5_GR, S5_W), 1) % S5_GR
           == lax.broadcasted_iota(jnp.int32, (S5_GR, S5_W), 0)).astype(BF16)
    mout = []
    for k in range(2):
        bb_re, bb_im = bb_ref[p4, k, 0], bb_ref[p4, k, 1]
        c_re, c_im = c_ref[p4, k, 0], c_ref[p4, k, 1]
        out_blocks = []
        for r in range(S5_T):
            t_in = S5_T - 1 - r if k == 0 else r
            t_out = r + 1 if k == 0 else S5_T - r
            rows = slice(S5_GR * r, S5_GR * (r + 1))
            pr, pi = pw_ref[p4, k, 0, t_in:t_in + 1, :], pw_ref[p4, k, 1, t_in:t_in + 1, :]
            min_scr[k, rows, 0:half] = bb_re * pr - bb_im * pi
            min_scr[k, rows, half:2 * half] = bb_re * pi + bb_im * pr
            pr, pi = pw_ref[p4, k, 0, t_out:t_out + 1, :], pw_ref[p4, k, 1, t_out:t_out + 1, :]
            out_blocks.append(jnp.concatenate([c_re * pr - c_im * pi, -(c_re * pi + c_im * pr)], axis=-1))
        mout.append(jnp.concatenate(out_blocks, axis=0).astype(BF16))
        c2 = jnp.concatenate([c_re, -c_im], axis=-1)
        z = lax.dot_general(min_scr[k], c2, _NT, preferred_element_type=F32, precision=lax.Precision.HIGHEST)
        zrep = jnp.dot(z.astype(BF16), rep, preferred_element_type=F32)
        zeros = jnp.zeros((S5_W, S5_W), F32)
        p_scr[0:S5_W, :] = zrep if k == 0 else zeros
        p_scr[S5_W:2 * S5_W, :] = zeros if k == 0 else zrep
        for j in range(S5_W // 128):
            cols = slice(128 * j, 128 * (j + 1))
            acc = None
            for s4 in range(4):
                s = 4 * j + s4
                start = S5_GR * (S5_T - 1 - s) if k == 0 else S5_W - S5_GR * s
                win = p_scr[start:start + S5_W, cols]
                acc = win if acc is None else jnp.where(gran == s4, win, acc)
            mi_scr[k, :, cols] = acc.astype(BF16)
    return mout


def _s5_scan(k, s_scrs, h_scrs, row0, nb, nchunks, hre, him, ar, ai):
    for n in (range(nchunks - 1, -1, -1) if k == 1 else range(nchunks)):
        rows = pl.ds(row0 + n, nb, stride=nchunks)
        h_scrs[0][rows, :] = hre
        h_scrs[1][rows, :] = him
        sre = s_scrs[0][rows, :]
        sim = s_scrs[1][rows, :]
        hre, him = ar * hre - ai * him + sre, ar * him + ai * hre + sim
    return hre, him


def _s5_kernel(x_ref, mod_ref, bb_ref, c_ref, pw_ref, h0_ref, y_ref, fin_ref,
               u_scr, yp_scr, sre_scr, sim_scr, hre_scr, him_scr, min_scr, mi_scr, p_scr):
    half = 2 * S5_P
    lane = lax.broadcasted_iota(jnp.int32, (S5_ROWS, 128), 1)
    scale = jnp.concatenate(
        [jnp.broadcast_to(1.0 + mod_ref[0, 1:2, :], (S5_ROWS_P, 128))]
        + [jnp.broadcast_to(1.0 + mod_ref[1 + b, 1:2, :], (S_CHUNKS, 128)) for b in range(DEC_BATCH)], axis=0)
    shift = jnp.concatenate(
        [jnp.broadcast_to(mod_ref[0, 0:1, :], (S5_ROWS_P, 128))]
        + [jnp.broadcast_to(mod_ref[1 + b, 0:1, :], (S_CHUNKS, 128)) for b in range(DEC_BATCH)], axis=0)
    for j in range(S5_T // 4):
        a = [x_ref[pl.ds(4 * j + s4, S5_ROWS, stride=S5_T), :] * scale + shift for s4 in range(4)]
        b = _granule_transpose4(a, lane)
        for p4 in range(S5_BLK):
            u_scr[p4, :, 128 * j:128 * (j + 1)] = b[p4].astype(BF16)

    for p4 in range(S5_BLK):
        mout = _s5_chunk_matrices(p4, bb_ref, c_ref, pw_ref, min_scr, mi_scr, p_scr)
        u = u_scr[p4]
        u0, u1 = u[:, 0:256], u[:, 256:512]
        y = None
        for k in range(2):
            s = jnp.dot(u, min_scr[k].astype(BF16), preferred_element_type=F32)
            sre_scr[...] = s[:, 0:half]
            sim_scr[...] = s[:, half:2 * half]
            s_scrs, h_scrs = (sre_scr, sim_scr), (hre_scr, him_scr)
            ar_p = jnp.broadcast_to(pw_ref[p4, k, 0, S5_T:S5_T + 1, :], (BATCH, half))
            ai_p = jnp.broadcast_to(pw_ref[p4, k, 1, S5_T:S5_T + 1, :], (BATCH, half))
            zero = jnp.zeros((BATCH, half), F32)
            fre, fim = _s5_scan(k, s_scrs, h_scrs, 0, BATCH, P_CHUNKS, zero, zero, ar_p, ai_p)
            fin_ref[p4, k, :, 0:half] = fre
            fin_ref[p4, k, :, half:2 * half] = fim
            _s5_scan(k, s_scrs, h_scrs, S5_ROWS_P, DEC_BATCH, S_CHUNKS,
                     h0_ref[p4, k, :, 0:half], h0_ref[p4, k, :, half:2 * half], ar_p[:DEC_BATCH], ai_p[:DEC_BATCH])
            hin = jnp.concatenate([hre_scr[...], him_scr[...]], axis=-1).astype(BF16)
            readout = lax.dot_general(hin, mout[k], _NT, preferred_element_type=F32)
            m00 = jnp.dot(u0, mi_scr[k, 0:256, 0:256], preferred_element_type=F32)
            m11 = jnp.dot(u1, mi_scr[k, 256:512, 256:512], preferred_element_type=F32)
            if k == 0:
                m11 = m11 + jnp.dot(u0, mi_scr[k, 0:256, 256:512], preferred_element_type=F32)
            else:
                m00 = m00 + jnp.dot(u1, mi_scr[k, 256:512, 0:256], preferred_element_type=F32)
            yk = jnp.concatenate([m00, m11], axis=-1) + readout
            y = yk if y is None else y + yk
        yp_scr[p4] = y

    for j in range(S5_T // 4):
        b = [yp_scr[p4, :, 128 * j:128 * (j + 1)] for p4 in range(S5_BLK)]
        a = _granule_transpose4(b, lane)
        for s4 in range(4):
            y_ref[pl.ds(4 * j + s4, S5_ROWS, stride=S5_T), :] = a[s4]


def _s5(x, mods, params, h0):
    bb, cc, pw = params
    nblk = S5_Q // S5_BLK
    return pl.pallas_call(
        _s5_kernel,
        grid=(nblk,),
        in_specs=[pl.BlockSpec((N_ROWS, 128), lambda q: (0, q)),
                  pl.BlockSpec((N_COND, 6, 128), lambda q: (0, 0, q)),
                  pl.BlockSpec((S5_BLK, 2, 2, S5_GR, 128), lambda q: (q, 0, 0, 0, 0)),
                  pl.BlockSpec((S5_BLK, 2, 2, S5_GR, 128), lambda q: (q, 0, 0, 0, 0)),
                  pl.BlockSpec((S5_BLK, 2, 2, S5_T + 1, 128), lambda q: (q, 0, 0, 0, 0)),
                  pl.BlockSpec((S5_BLK, 2, DEC_BATCH, 256), lambda q: (q, 0, 0, 0))],
        out_specs=[pl.BlockSpec((N_ROWS, 128), lambda q: (0, q)),
                   pl.BlockSpec((S5_BLK, 2, BATCH, 256), lambda q: (q, 0, 0, 0))],
        out_shape=[jax.ShapeDtypeStruct((N_ROWS, D), F32),
                   jax.ShapeDtypeStruct((S5_Q, 2, BATCH, 256), F32)],
        scratch_shapes=[pltpu.VMEM((S5_BLK, S5_ROWS, S5_W), BF16), pltpu.VMEM((S5_BLK, S5_ROWS, S5_W), F32),
                        pltpu.VMEM((S5_ROWS, 128), F32), pltpu.VMEM((S5_ROWS, 128), F32),
                        pltpu.VMEM((S5_ROWS, 128), F32), pltpu.VMEM((S5_ROWS, 128), F32),
                        pltpu.VMEM((2, S5_W, 256), F32),
                        pltpu.VMEM((2, S5_W, S5_W), BF16), pltpu.VMEM((2 * S5_W, S5_W), F32)],
        compiler_params=_cparams(("parallel",)),
        name="s5",
    )(x, mods, bb, cc, pw, h0)


def _s5_params(a_re, a_im, log_step, b_re, b_im, c_re, c_im):
    lam_re = jnp.minimum(a_re.astype(F32), -1e-4)
    lam_im = a_im.astype(F32)
    dt = jnp.exp(log_step.astype(F32))[..., None]
    j = jnp.arange(S5_T + 1, dtype=F32)[:, None, None, None]
    mag = jnp.exp(lam_re * dt * j)
    pw_re = mag * jnp.cos(lam_im * dt * j)
    pw_im = mag * jnp.sin(lam_im * dt * j)
    nr, ni = pw_re[1] - 1.0, pw_im[1]
    den = lam_re * lam_re + lam_im * lam_im
    q_re = (nr * lam_re + ni * lam_im) / den
    q_im = (ni * lam_re - nr * lam_im) / den
    b_re, b_im = b_re.astype(F32), b_im.astype(F32)
    bb_re = q_re[..., None] * b_re - q_im[..., None] * b_im
    bb_im = q_re[..., None] * b_im + q_im[..., None] * b_re
    eye = jnp.eye(2, dtype=F32)

    def pair_rows(m, spec):
        m = m.reshape(2, S5_Q, 2, *m.shape[2:])
        return jnp.einsum(spec, m, eye).reshape(S5_Q, 2, S5_GR, 2 * S5_P)

    bb = jnp.stack([pair_rows(bb_re, "kqgpc,gh->qkgchp"), pair_rows(bb_im, "kqgpc,gh->qkgchp")], axis=2)
    cc = jnp.stack([pair_rows(c_re.astype(F32), "kqgcp,gh->qkgchp"),
                    pair_rows(c_im.astype(F32), "kqgcp,gh->qkgchp")], axis=2)

    def pw_lay(p):
        return p.reshape(S5_T + 1, 2, S5_Q, 2 * S5_P).transpose(2, 1, 0, 3)
    pw = jnp.stack([pw_lay(pw_re), pw_lay(pw_im)], axis=2)
    return bb, cc, pw


def _s5_state_in(st_re, st_im):
    def lay(s):
        return s.astype(F32).reshape(DEC_BATCH, 2, S5_Q, 2 * S5_P).transpose(2, 1, 0, 3)
    return jnp.concatenate([lay(st_re), lay(st_im)], axis=-1)


def _s5_state_out(fin):
    def lay(s):
        return s.transpose(2, 1, 0, 3).reshape(BATCH, 2, S5_G, S5_P)
    return lay(fin[..., :2 * S5_P]), lay(fin[..., 2 * S5_P:])


def _glu_kernel(ys_ref, x_ref, mod_ref, d_ref, w_ref, g_ref, b_ref, x1_ref, h2_ref):
    x = x_ref[...]
    h = _modulate(x, mod_ref, 0, 1)
    y = jax.nn.gelu(ys_ref[...] + d_ref[...] * h).astype(BF16)
    z = jnp.dot(y, w_ref[...], preferred_element_type=F32)
    f = z[:, :D] * jax.nn.sigmoid(z[:, D:])
    x1 = _layer_norm(ALPHA * x + (1.0 + mod_ref[2:3, :]) * f, g_ref[...], b_ref[...])
    x1_ref[...] = x1
    h2_ref[...] = _modulate(x1, mod_ref, 3, 4).astype(BF16)


def _glu(ysum, x, mod, d, w_glu, ln_g, ln_b):
    return pl.pallas_call(
        _glu_kernel,
        grid=(N_ROWS // TM,),
        in_specs=[_row_spec(TM), _row_spec(TM), _mod_spec(TM), _const_spec((1, D)),
                  _const_spec((D, 2 * D)), _const_spec((1, D)), _const_spec((1, D))],
        out_specs=[_row_spec(TM), _row_spec(TM)],
        out_shape=[jax.ShapeDtypeStruct((N_ROWS, D), F32), jax.ShapeDtypeStruct((N_ROWS, D), BF16)],
        compiler_params=_cparams(("parallel",)),
        name="glu",
    )(ysum, x, mod, d.reshape(1, D), w_glu, ln_g.reshape(1, D), ln_b.reshape(1, D))


def _finish(x1, g, mod_ref, g_ref, b_ref, x2_ref):
    x2 = _layer_norm(ALPHA * x1 + (1.0 + mod_ref[5:6, :]) * g, g_ref[...], b_ref[...])
    x2_ref[...] = x2
    return x2


FFN_TF = FFN_DIM // 2


def _ffn_kernel(h_ref, x1_ref, mod_ref, nmod_ref, wg_ref, wu_ref, wd_ref, g_ref, b_ref, x2_ref, hn_ref):
    h = h_ref[...]
    acc = None
    for c in range(FFN_DIM // FFN_TF):
        sl = slice(c * FFN_TF, (c + 1) * FFN_TF)
        a = jnp.dot(h, wg_ref[:, sl], preferred_element_type=F32)
        b = jnp.dot(h, wu_ref[:, sl], preferred_element_type=F32)
        p = jnp.dot((jax.nn.silu(a) * b).astype(BF16), wd_ref[sl, :], preferred_element_type=F32)
        acc = p if acc is None else acc + p
    x2 = _finish(x1_ref[...], acc, mod_ref, g_ref, b_ref, x2_ref)
    hn_ref[...] = _modulate(x2, nmod_ref, 0, 1).astype(BF16)


def _ffn(h2, x1, mod, nmod, wg, wu, wd, ln_g, ln_b):
    single = pl.Buffered(1)
    return pl.pallas_call(
        _ffn_kernel,
        grid=(N_ROWS // TM,),
        in_specs=[_row_spec(TM), _row_spec(TM), _mod_spec(TM), _mod_spec(TM),
                  pl.BlockSpec((D, FFN_DIM), lambda t: (0, 0), pipeline_mode=single),
                  pl.BlockSpec((D, FFN_DIM), lambda t: (0, 0), pipeline_mode=single),
                  pl.BlockSpec((FFN_DIM, D), lambda t: (0, 0), pipeline_mode=single),
                  _const_spec((1, D)), _const_spec((1, D))],
        out_specs=[_row_spec(TM), _row_spec(TM)],
        out_shape=[jax.ShapeDtypeStruct((N_ROWS, D), F32), jax.ShapeDtypeStruct((N_ROWS, D), BF16)],
        compiler_params=_cparams(("parallel",)),
        name="ffn",
    )(h2, x1, mod, nmod, wg, wu, wd, ln_g.reshape(1, D), ln_b.reshape(1, D))


def _gmlp_kernel(h_ref, x_ref, mod_ref, win_ref, bin_ref, vg_ref, vb_ref, ws_ref, bs_ref, wout_ref,
                 g_ref, b_ref, wr_ref, br_ref, x1_ref, h2_ref, idx_ref, p_ref, s_scr):
    z = jax.nn.gelu(jnp.dot(h_ref[...], win_ref[...], preferred_element_type=F32) + bin_ref[...])
    u = z[:, :D]
    v = _layer_norm(z[:, D:], vg_ref[...], vb_ref[...]).astype(BF16)
    n_chunks = TM // GM_CHUNK
    for hd in range(GM_HEADS):
        cols = slice(hd * GM_CHUNK, (hd + 1) * GM_CHUNK)
        rhs = jnp.concatenate([v[n * GM_CHUNK:(n + 1) * GM_CHUNK, cols] for n in range(n_chunks)], axis=-1)
        s = jnp.dot(ws_ref[hd], rhs, preferred_element_type=F32)
        for n in range(n_chunks):
            s_scr[n * GM_CHUNK:(n + 1) * GM_CHUNK, cols] = s[:, n * GM_CHUNK:(n + 1) * GM_CHUNK] + bs_ref[hd]
    f = jnp.dot((u * s_scr[...]).astype(BF16), wout_ref[...], preferred_element_type=F32)
    x = x_ref[...]
    x1 = _layer_norm(ALPHA * x + (1.0 + mod_ref[2:3, :]) * f, g_ref[...], b_ref[...])
    x1_ref[...] = x1
    h2 = _modulate(x1, mod_ref, 3, 4)
    h2_ref[...] = h2.astype(BF16)
    logits = jnp.dot(h2, wr_ref[...], preferred_element_type=F32, precision=lax.Precision.HIGHEST) + br_ref[...]
    lane = lax.broadcasted_iota(jnp.int32, logits.shape, 1)
    neg = jnp.float32(-jnp.inf)
    logits = jnp.where(lane < N_EXPERTS, logits, neg)
    v1 = jnp.max(logits, axis=-1, keepdims=True)
    i1 = jnp.min(jnp.where(logits == v1, lane, 128), axis=-1, keepdims=True)
    rest = jnp.where(lane == i1, neg, logits)
    v2 = jnp.max(rest, axis=-1, keepdims=True)
    i2 = jnp.min(jnp.where(rest == v2, lane, 128), axis=-1, keepdims=True)
    e2 = jnp.exp(v2 - v1)
    p1 = 1.0 / (1.0 + e2)
    p2 = e2 / (1.0 + e2)
    idx_ref[...] = jnp.where(lane == 0, i1, jnp.where(lane == 1, i2, 0))
    p_ref[...] = jnp.where(lane == 0, p1, jnp.where(lane == 1, p2, 0.0))


def _gmlp(h1, x, mod, w_in, b_in, vg, vb, w_s, b_s, w_out, ln_g, ln_b, w_r, b_r):
    return pl.pallas_call(
        _gmlp_kernel,
        grid=(N_ROWS // TM,),
        in_specs=[_row_spec(TM), _row_spec(TM), _mod_spec(TM),
                  _const_spec((D, 2 * D)), _const_spec((1, 2 * D)), _const_spec((1, D)), _const_spec((1, D)),
                  _const_spec((GM_HEADS, GM_CHUNK, GM_CHUNK)), _const_spec((GM_HEADS, GM_CHUNK, GM_CHUNK)),
                  _const_spec((D, D)), _const_spec((1, D)), _const_spec((1, D)),
                  _const_spec((D, 128)), _const_spec((1, 128))],
        out_specs=[_row_spec(TM), _row_spec(TM), _row_spec(TM, 128), _row_spec(TM, 128)],
        out_shape=[jax.ShapeDtypeStruct((N_ROWS, D), F32), jax.ShapeDtypeStruct((N_ROWS, D), BF16),
                   jax.ShapeDtypeStruct((N_ROWS, 128), jnp.int32), jax.ShapeDtypeStruct((N_ROWS, 128), F32)],
        scratch_shapes=[pltpu.VMEM((TM, D), F32)],
        compiler_params=_cparams(("parallel",)),
        name="gmlp",
    )(h1, x, mod, w_in, b_in.reshape(1, 2 * D), vg.reshape(1, D), vb.reshape(1, D), w_s, b_s, w_out,
      ln_g.reshape(1, D), ln_b.reshape(1, D), w_r, b_r)


def _moe_kernel(te_ref, used_ref, x_ref, wg_ref, wu_ref, wd_ref, y_ref, acc_ref):
    t, f = pl.program_id(0), pl.program_id(1)
    live = t < used_ref[0]

    @pl.when(jnp.logical_and(live, f == 0))
    def _():
        acc_ref[...] = jnp.zeros_like(acc_ref)

    @pl.when(live)
    def _():
        x = x_ref[...]
        a = jnp.dot(x, wg_ref[...], preferred_element_type=F32)
        b = jnp.dot(x, wu_ref[...], preferred_element_type=F32)
        acc_ref[...] += jnp.dot((jax.nn.silu(a) * b).astype(BF16), wd_ref[...], preferred_element_type=F32)

    @pl.when(f == pl.num_programs(1) - 1)
    def _():
        y_ref[...] = jnp.where(live, acc_ref[...], 0.0)


def _moe(tile_expert, n_used, xs, wg, wu, wd):
    nf = EXPERT_DIM // MOE_TF

    def last_f(t, f, used):
        return jnp.where(t < used[0], f, nf - 1)

    def row(t, used):
        return jnp.minimum(t, used[0] - 1)

    grid_spec = pltpu.PrefetchScalarGridSpec(
        num_scalar_prefetch=2,
        grid=(MOE_TILES, nf),
        in_specs=[pl.BlockSpec((MOE_TM, D), lambda t, f, te, used: (row(t, used), 0)),
                  pl.BlockSpec((None, D, MOE_TF), lambda t, f, te, used: (te[t], 0, last_f(t, f, used))),
                  pl.BlockSpec((None, D, MOE_TF), lambda t, f, te, used: (te[t], 0, last_f(t, f, used))),
                  pl.BlockSpec((None, MOE_TF, D), lambda t, f, te, used: (te[t], last_f(t, f, used), 0))],
        out_specs=pl.BlockSpec((MOE_TM, D), lambda t, f, te, used: (t, 0)),
        scratch_shapes=[pltpu.VMEM((MOE_TM, D), F32)],
    )
    return pl.pallas_call(
        _moe_kernel,
        grid_spec=grid_spec,
        out_shape=jax.ShapeDtypeStruct((MOE_ROWS, D), F32),
        compiler_params=_cparams(("arbitrary", "arbitrary")),
        name="moe",
    )(tile_expert, n_used, xs, wg, wu, wd)


def _route(idx):
    e = idx.reshape(-1)
    onehot = (e[:, None] == jnp.arange(N_EXPERTS, dtype=jnp.int32)[None, :]).astype(jnp.int32)
    counts = jnp.sum(onehot, axis=0)
    rank = jnp.sum((jnp.cumsum(onehot, axis=0) - 1) * onehot, axis=1)
    padded = ((counts + MOE_TM - 1) // MOE_TM) * MOE_TM
    ends = jnp.cumsum(padded)
    starts = ends - padded
    pos = starts[e] + rank
    token = jnp.arange(N_ROWS * TOP_K, dtype=jnp.int32) // TOP_K
    src = jnp.zeros((MOE_ROWS,), jnp.int32).at[pos].set(token)
    n_used = (ends[-1] // MOE_TM).astype(jnp.int32)
    tile_start = jnp.minimum(jnp.arange(MOE_TILES, dtype=jnp.int32), n_used - 1) * MOE_TM
    tile_expert = jnp.sum((tile_start[:, None] >= ends[None, :]).astype(jnp.int32), axis=1)
    tile_expert = jnp.minimum(tile_expert, N_EXPERTS - 1).astype(jnp.int32)
    return src, pos.reshape(N_ROWS, TOP_K), tile_expert, n_used.reshape(1)


def _combine_kernel(ya_ref, yb_ref, p_ref, x1_ref, mod_ref, g_ref, b_ref, x2_ref):
    g = p_ref[:, 0:1] * ya_ref[...] + p_ref[:, 1:2] * yb_ref[...]
    _finish(x1_ref[...], g, mod_ref, g_ref, b_ref, x2_ref)


def _combine(ya, yb, p, x1, mod, ln_g, ln_b):
    return pl.pallas_call(
        _combine_kernel,
        grid=(N_ROWS // TM,),
        in_specs=[_row_spec(TM), _row_spec(TM), _row_spec(TM, 128), _row_spec(TM), _mod_spec(TM),
                  _const_spec((1, D)), _const_spec((1, D))],
        out_specs=_row_spec(TM),
        out_shape=jax.ShapeDtypeStruct((N_ROWS, D), F32),
        compiler_params=_cparams(("parallel",)),
        name="combine",
    )(ya, yb, p, x1, mod, ln_g.reshape(1, D), ln_b.reshape(1, D))


def _grid_pos_embed():
    rows = DEC_SEQ // GRID_W
    r, col = jnp.meshgrid(jnp.arange(rows, dtype=F32), jnp.arange(GRID_W, dtype=F32), indexing="ij")
    quarter = D // 4
    freq = 1.0 / (10000.0 ** (jnp.arange(quarter, dtype=F32) / quarter))

    def emb(p):
        ang = p.reshape(-1)[:, None] * freq[None, :]
        return jnp.concatenate([jnp.sin(ang), jnp.cos(ang)], axis=-1)

    return jnp.concatenate([emb(r), emb(col)], axis=-1)


def kernel(x_prompt, x_sample, c, state_ssm_re, state_ssm_im, c_ctx, w_ada, b_ada, ln_g, ln_b, s5_a_re, s5_a_im, s5_log_step, s5_b_re, s5_b_im, s5_c_re, s5_c_im, s5_d, s5_w_glu, gm_w_in, gm_b_in, gm_ln_g, gm_ln_b, gm_w_s, gm_b_s, gm_w_out, ffn_w_gate, ffn_w_up, ffn_w_down, moe_w_router, moe_b_router, moe_w_gate, moe_w_up, moe_w_down):
    cond8 = jnp.concatenate([c_ctx[None, :], c, jnp.zeros((N_COND - 1 - DEC_BATCH, D), F32)], axis=0)
    mods = _ada(cond8, w_ada, b_ada).reshape(DEPTH, N_COND, 6, D)

    x_cat = jnp.concatenate([x_prompt.reshape(N_P, D), x_sample.reshape(N_S, D)], axis=0)
    x = _prep(x_cat, _grid_pos_embed())
    h = None

    fin_re, fin_im = [], []
    for i in range(DEPTH):
        j = i // 2
        mod = mods[i]
        nmod = mods[min(i + 1, DEPTH - 1)]
        if i % 2 == 0:
            params = _s5_params(s5_a_re[j], s5_a_im[j], s5_log_step[j], s5_b_re[j], s5_b_im[j], s5_c_re[j], s5_c_im[j])
            ysum, fin = _s5(x, mod, params, _s5_state_in(state_ssm_re[:, j], state_ssm_im[:, j]))
            fr, fi = _s5_state_out(fin)
            fin_re.append(fr)
            fin_im.append(fi)
            x1, h2 = _glu(ysum, x, mod, s5_d[j], s5_w_glu[j].astype(BF16), ln_g[i, 0], ln_b[i, 0])
            x, h = _ffn(h2, x1, mod, nmod, ffn_w_gate[j].astype(BF16), ffn_w_up[j].astype(BF16),
                        ffn_w_down[j].astype(BF16), ln_g[i, 1], ln_b[i, 1])
        else:
            b_s = jnp.broadcast_to(gm_b_s[j][:, :, None], (GM_HEADS, GM_CHUNK, GM_CHUNK)).astype(F32)
            w_r = jnp.pad(moe_w_router[j].astype(F32), ((0, 0), (0, 128 - N_EXPERTS)))
            b_r = jnp.pad(moe_b_router[j].astype(F32), (0, 128 - N_EXPERTS)).reshape(1, 128)
            x1, h2, idx, p = _gmlp(h, x, mod, gm_w_in[j].astype(BF16), gm_b_in[j], gm_ln_g[j], gm_ln_b[j],
                                   gm_w_s[j].astype(BF16), b_s, gm_w_out[j].astype(BF16), ln_g[i, 0], ln_b[i, 0], w_r, b_r)
            src, pos, tile_expert, n_used = _route(idx[:, :TOP_K])
            ys = _moe(tile_expert, n_used, jnp.take(h2, src, axis=0), moe_w_gate[j].astype(BF16),
                      moe_w_up[j].astype(BF16), moe_w_down[j].astype(BF16))
            x = _combine(jnp.take(ys, pos[:, 0], axis=0), jnp.take(ys, pos[:, 1], axis=0), p, x1, mod,
                         ln_g[i, 1], ln_b[i, 1])

    y_prompt = x[:N_P].reshape(BATCH, SEQ, D)
    y_sample = x[N_P:].reshape(DEC_BATCH, DEC_SEQ, D)
    new_re = jnp.stack(fin_re, axis=1).astype(x_prompt.dtype)
    new_im = jnp.stack(fin_im, axis=1).astype(x_prompt.dtype)
    return (y_prompt, y_sample, new_re, new_im)
```

```python
import jax
import jax.numpy as jnp
from jax import lax
from jax.experimental import pallas as pl
from jax.experimental.pallas import tpu as pltpu

F32 = jnp.float32
BF16 = jnp.bfloat16

D = 1024
BATCH, SEQ = 32, 256
DEC_BATCH, DEC_SEQ = 4, 1024
DEPTH = 4
GRID_W = 64
N_P = BATCH * SEQ
N_S = DEC_BATCH * DEC_SEQ
N_ROWS = N_P + N_S
N_COND = 8

S5_CG, S5_P, S5_G = 16, 64, 64
S5_T = 16
S5_Q = S5_G // 2
P_CHUNKS = SEQ // S5_T
S_CHUNKS = DEC_SEQ // S5_T
S5_ROWS_P = P_CHUNKS * BATCH
S5_ROWS_S = S_CHUNKS * DEC_BATCH
S5_ROWS = S5_ROWS_P + S5_ROWS_S

GM_CHUNK, GM_HEADS = 128, 8
FFN_DIM = 2816
N_EXPERTS, TOP_K, EXPERT_DIM = 8, 2, 3584
ALPHA = (2.0 * DEPTH) ** 0.25
LN_EPS = 1e-5

TM = 512
MOE_TM = 512
MOE_TF = 512
MOE_TILES = (N_ROWS * TOP_K) // MOE_TM + N_EXPERTS
MOE_ROWS = MOE_TILES * MOE_TM
VMEM_LIMIT = 56 * 1024 * 1024


def _cparams(sem):
    return pltpu.CompilerParams(dimension_semantics=sem, vmem_limit_bytes=VMEM_LIMIT)


def _cond_of_tile(t, tm):
    row0 = t * tm
    return jnp.where(row0 < N_P, 0, 1 + (row0 - N_P) // DEC_SEQ)


def _mod_spec(tm):
    return pl.BlockSpec((None, 6, D), lambda t: (_cond_of_tile(t, tm), 0, 0))


def _row_spec(tm, width=D):
    return pl.BlockSpec((tm, width), lambda t: (t, 0))


def _const_spec(shape):
    return pl.BlockSpec(shape, lambda t: (0,) * len(shape))


def _layer_norm(r, g, b):
    mu = jnp.mean(r, axis=-1, keepdims=True)
    c = r - mu
    var = jnp.mean(c * c, axis=-1, keepdims=True)
    return c * lax.rsqrt(var + LN_EPS) * g + b


def _modulate(x, mod_ref, shift_row, scale_row):
    return x * (1.0 + mod_ref[scale_row:scale_row + 1, :]) + mod_ref[shift_row:shift_row + 1, :]


def _ada_kernel(c_ref, w_ref, b_ref, o_ref):
    c = c_ref[...]
    o_ref[...] = jnp.dot(jax.nn.silu(c), w_ref[...], preferred_element_type=F32) + b_ref[...]


def _ada(cond8, w_ada, b_ada):
    tn = 1536
    return pl.pallas_call(
        _ada_kernel,
        grid=(DEPTH, 6 * D // tn),
        in_specs=[pl.BlockSpec((N_COND, D), lambda i, n: (0, 0)),
                  pl.BlockSpec((None, D, tn), lambda i, n: (i, 0, n)),
                  pl.BlockSpec((None, 1, tn), lambda i, n: (i, 0, n))],
        out_specs=pl.BlockSpec((None, N_COND, tn), lambda i, n: (i, 0, n)),
        out_shape=jax.ShapeDtypeStruct((DEPTH, N_COND, 6 * D), F32),
        compiler_params=_cparams(("arbitrary", "arbitrary")),
        name="ada",
    )(cond8, w_ada, b_ada.reshape(DEPTH, 1, 6 * D))


def _prep_kernel(xp_ref, xs_ref, pos_ref, xo_ref):
    is_latent = pl.program_id(0) * TM >= N_P
    xo_ref[...] = jnp.where(is_latent, xs_ref[...] + pos_ref[...], xp_ref[...])


def _prep(x_prompt, x_sample, pos):
    p_tiles = N_P // TM

    def latent_tile(t):
        return jnp.maximum(t - p_tiles, 0)

    return pl.pallas_call(
        _prep_kernel,
        grid=(N_ROWS // TM,),
        in_specs=[pl.BlockSpec((TM, D), lambda t: (jnp.minimum(t, p_tiles - 1), 0)),
                  pl.BlockSpec((TM, D), lambda t: (latent_tile(t), 0)),
                  pl.BlockSpec((TM, D), lambda t: (latent_tile(t) % (DEC_SEQ // TM), 0))],
        out_specs=_row_spec(TM),
        out_shape=jax.ShapeDtypeStruct((N_ROWS, D), F32),
        compiler_params=_cparams(("parallel",)),
        name="prep",
    )(x_prompt.reshape(N_P, D), x_sample.reshape(N_S, D), pos)


S5_BLK = 4
S5_W = 2 * S5_T * S5_CG
S5_GR = 2 * S5_CG
_NT = (((1,), (1,)), ((), ()))


def _granule_transpose4(a, lane):
    low = lane < 64
    a0 = jnp.where(low, a[0], pltpu.roll(a[2], 64, 1))
    a2 = jnp.where(low, pltpu.roll(a[0], 64, 1), a[2])
    a1 = jnp.where(low, a[1], pltpu.roll(a[3], 64, 1))
    a3 = jnp.where(low, pltpu.roll(a[1], 64, 1), a[3])
    even = (lane // S5_GR) % 2 == 0
    return [jnp.where(even, a0, pltpu.roll(a1, 32, 1)), jnp.where(even, pltpu.roll(a0, 96, 1), a1),
            jnp.where(even, a2, pltpu.roll(a3, 32, 1)), jnp.where(even, pltpu.roll(a2, 96, 1), a3)]


def _s5_chunk_matrices(p4, bb_ref, c_ref, pw_ref, min_scr, mi_scr, p_scr):
    half = 2 * S5_P
    gran = lax.broadcasted_iota(jnp.int32, (S5_W, 128), 1) // S5_GR
    rep = (lax.broadcasted_iota(jnp.int32, (S5_GR, S5_W), 1) % S5_GR
           == lax.broadcasted_iota(jnp.int32, (S5_GR, S5_W), 0)).astype(BF16)
    mout = []
    for k in range(2):
        bb_re, bb_im = bb_ref[p4, k, 0], bb_ref[p4, k, 1]
        c_re, c_im = c_ref[p4, k, 0], c_ref[p4, k, 1]
        out_blocks = []
        for r in range(S5_T):
            t_in = S5_T - 1 - r if k == 0 else r
            t_out = r + 1 if k == 0 else S5_T - r
            rows = slice(S5_GR * r, S5_GR * (r + 1))
            pr, pi = pw_ref[p4, k, 0, t_in:t_in + 1, :], pw_ref[p4, k, 1, t_in:t_in + 1, :]
            min_scr[k, rows, 0:half] = bb_re * pr - bb_im * pi
            min_scr[k, rows, half:2 * half] = bb_re * pi + bb_im * pr
            pr, pi = pw_ref[p4, k, 0, t_out:t_out + 1, :], pw_ref[p4, k, 1, t_out:t_out + 1, :]
            out_blocks.append(jnp.concatenate([c_re * pr - c_im * pi, -(c_re * pi + c_im * pr)], axis=-1))
        mout.append(jnp.concatenate(out_blocks, axis=0).astype(BF16))
        c2 = jnp.concatenate([c_re, -c_im], axis=-1)
        z = lax.dot_general(min_scr[k], c2, _NT, preferred_element_type=F32, precision=lax.Precision.HIGHEST)
        zrep = jnp.dot(z.astype(BF16), rep, preferred_element_type=F32)
        zeros = jnp.zeros((S5_W, S5_W), F32)
        p_scr[0:S5_W, :] = zrep if k == 0 else zeros
        p_scr[S5_W:2 * S5_W, :] = zeros if k == 0 else zrep
        for j in range(S5_W // 128):
            cols = slice(128 * j, 128 * (j + 1))
            acc = None
            for s4 in range(4):
                s = 4 * j + s4
                start = S5_GR * (S5_T - 1 - s) if k == 0 else S5_W - S5_GR * s
                win = p_scr[start:start + S5_W, cols]
                acc = win if acc is None else jnp.where(gran == s4, win, acc)
            mi_scr[k, :, cols] = acc.astype(BF16)
    return mout


def _s5_scan(k, s_scrs, h_scrs, row0, nb, nchunks, hre, him, ar, ai):
    for n in (range(nchunks - 1, -1, -1) if k == 1 else range(nchunks)):
        rows = pl.ds(row0 + n, nb, stride=nchunks)
        h_scrs[0][rows, :] = hre
        h_scrs[1][rows, :] = him
        sre = s_scrs[0][rows, :]
        sim = s_scrs[1][rows, :]
        hre, him = ar * hre - ai * him + sre, ar * him + ai * hre + sim
    return hre, him


def _s5_kernel(x_ref, mod_ref, bb_ref, c_ref, pw_ref, h0_ref, y_ref, fin_ref,
               u_scr, yp_scr, sre_scr, sim_scr, hre_scr, him_scr, min_scr, mi_scr, p_scr):
    half = 2 * S5_P
    lane = lax.broadcasted_iota(jnp.int32, (S5_ROWS, 128), 1)
    scale = jnp.concatenate(
        [jnp.broadcast_to(1.0 + mod_ref[0, 1:2, :], (S5_ROWS_P, 128))]
        + [jnp.broadcast_to(1.0 + mod_ref[1 + b, 1:2, :], (S_CHUNKS, 128)) for b in range(DEC_BATCH)], axis=0)
    shift = jnp.concatenate(
        [jnp.broadcast_to(mod_ref[0, 0:1, :], (S5_ROWS_P, 128))]
        + [jnp.broadcast_to(mod_ref[1 + b, 0:1, :], (S_CHUNKS, 128)) for b in range(DEC_BATCH)], axis=0)
    for j in range(S5_T // 4):
        a = [x_ref[pl.ds(4 * j + s4, S5_ROWS, stride=S5_T), :] * scale + shift for s4 in range(4)]
        b = _granule_transpose4(a, lane)
        for p4 in range(S5_BLK):
            u_scr[p4, :, 128 * j:128 * (j + 1)] = b[p4].astype(BF16)

    for p4 in range(S5_BLK):
        mout = _s5_chunk_matrices(p4, bb_ref, c_ref, pw_ref, min_scr, mi_scr, p_scr)
        u = u_scr[p4]
        u0, u1 = u[:, 0:256], u[:, 256:512]
        y = None
        for k in range(2):
            s = jnp.dot(u, min_scr[k].astype(BF16), preferred_element_type=F32)
            sre_scr[...] = s[:, 0:half]
            sim_scr[...] = s[:, half:2 * half]
            s_scrs, h_scrs = (sre_scr, sim_scr), (hre_scr, him_scr)
            ar_p = jnp.broadcast_to(pw_ref[p4, k, 0, S5_T:S5_T + 1, :], (BATCH, half))
            ai_p = jnp.broadcast_to(pw_ref[p4, k, 1, S5_T:S5_T + 1, :], (BATCH, half))
            zero = jnp.zeros((BATCH, half), F32)
            fre, fim = _s5_scan(k, s_scrs, h_scrs, 0, BATCH, P_CHUNKS, zero, zero, ar_p, ai_p)
            fin_ref[p4, k, :, 0:half] = fre
            fin_ref[p4, k, :, half:2 * half] = fim
            _s5_scan(k, s_scrs, h_scrs, S5_ROWS_P, DEC_BATCH, S_CHUNKS,
                     h0_ref[p4, k, :, 0:half], h0_ref[p4, k, :, half:2 * half], ar_p[:DEC_BATCH], ai_p[:DEC_BATCH])
            hin = jnp.concatenate([hre_scr[...], him_scr[...]], axis=-1).astype(BF16)
            readout = lax.dot_general(hin, mout[k], _NT, preferred_element_type=F32)
            m00 = jnp.dot(u0, mi_scr[k, 0:256, 0:256], preferred_element_type=F32)
            m11 = jnp.dot(u1, mi_scr[k, 256:512, 256:512], preferred_element_type=F32)
            if k == 0:
                m11 = m11 + jnp.dot(u0, mi_scr[k, 0:256, 256:512], preferred_element_type=F32)
            else:
                m00 = m00 + jnp.dot(u1, mi_scr[k, 256:512, 0:256], preferred_element_type=F32)
            yk = jnp.concatenate([m00, m11], axis=-1) + readout
            y = yk if y is None else y + yk
        yp_scr[p4] = y

    for j in range(S5_T // 4):
        b = [yp_scr[p4, :, 128 * j:128 * (j + 1)] for p4 in range(S5_BLK)]
        a = _granule_transpose4(b, lane)
        for s4 in range(4):
            y_ref[pl.ds(4 * j + s4, S5_ROWS, stride=S5_T), :] = a[s4]


def _s5(x, mods, params, h0):
    bb, cc, pw = params
    nblk = S5_Q // S5_BLK
    return pl.pallas_call(
        _s5_kernel,
        grid=(nblk,),
        in_specs=[pl.BlockSpec((N_ROWS, 128), lambda q: (0, q)),
                  pl.BlockSpec((N_COND, 6, 128), lambda q: (0, 0, q)),
                  pl.BlockSpec((S5_BLK, 2, 2, S5_GR, 128), lambda q: (q, 0, 0, 0, 0)),
                  pl.BlockSpec((S5_BLK, 2, 2, S5_GR, 128), lambda q: (q, 0, 0, 0, 0)),
                  pl.BlockSpec((S5_BLK, 2, 2, S5_T + 1, 128), lambda q: (q, 0, 0, 0, 0)),
                  pl.BlockSpec((S5_BLK, 2, DEC_BATCH, 256), lambda q: (q, 0, 0, 0))],
        out_specs=[pl.BlockSpec((N_ROWS, 128), lambda q: (0, q)),
                   pl.BlockSpec((S5_BLK, 2, BATCH, 256), lambda q: (q, 0, 0, 0))],
        out_shape=[jax.ShapeDtypeStruct((N_ROWS, D), F32),
                   jax.ShapeDtypeStruct((S5_Q, 2, BATCH, 256), F32)],
        scratch_shapes=[pltpu.VMEM((S5_BLK, S5_ROWS, S5_W), BF16), pltpu.VMEM((S5_BLK, S5_ROWS, S5_W), F32),
                        pltpu.VMEM((S5_ROWS, 128), F32), pltpu.VMEM((S5_ROWS, 128), F32),
                        pltpu.VMEM((S5_ROWS, 128), F32), pltpu.VMEM((S5_ROWS, 128), F32),
                        pltpu.VMEM((2, S5_W, 256), F32),
                        pltpu.VMEM((2, S5_W, S5_W), BF16), pltpu.VMEM((2 * S5_W, S5_W), F32)],
        compiler_params=_cparams(("parallel",)),
        name="s5",
    )(x, mods, bb, cc, pw, h0)


def _s5_params(a_re, a_im, log_step, b_re, b_im, c_re, c_im):
    lam_re = jnp.minimum(a_re.astype(F32), -1e-4)
    lam_im = a_im.astype(F32)
    dt = jnp.exp(log_step.astype(F32))[..., None]
    j = jnp.arange(S5_T + 1, dtype=F32)[:, None, None, None]
    mag = jnp.exp(lam_re * dt * j)
    pw_re = mag * jnp.cos(lam_im * dt * j)
    pw_im = mag * jnp.sin(lam_im * dt * j)
    nr, ni = pw_re[1] - 1.0, pw_im[1]
    den = lam_re * lam_re + lam_im * lam_im
    q_re = (nr * lam_re + ni * lam_im) / den
    q_im = (ni * lam_re - nr * lam_im) / den
    b_re, b_im = b_re.astype(F32), b_im.astype(F32)
    bb_re = q_re[..., None] * b_re - q_im[..., None] * b_im
    bb_im = q_re[..., None] * b_im + q_im[..., None] * b_re
    eye = jnp.eye(2, dtype=F32)

    def pair_rows(m, spec):
        m = m.reshape(2, S5_Q, 2, *m.shape[2:])
        return jnp.einsum(spec, m, eye).reshape(S5_Q, 2, S5_GR, 2 * S5_P)

    bb = jnp.stack([pair_rows(bb_re, "kqgpc,gh->qkgchp"), pair_rows(bb_im, "kqgpc,gh->qkgchp")], axis=2)
    cc = jnp.stack([pair_rows(c_re.astype(F32), "kqgcp,gh->qkgchp"),
                    pair_rows(c_im.astype(F32), "kqgcp,gh->qkgchp")], axis=2)

    def pw_lay(p):
        return p.reshape(S5_T + 1, 2, S5_Q, 2 * S5_P).transpose(2, 1, 0, 3)
    pw = jnp.stack([pw_lay(pw_re), pw_lay(pw_im)], axis=2)
    return bb, cc, pw


def _s5_state_in(st_re, st_im):
    def lay(s):
        return s.astype(F32).reshape(DEC_BATCH, 2, S5_Q, 2 * S5_P).transpose(2, 1, 0, 3)
    return jnp.concatenate([lay(st_re), lay(st_im)], axis=-1)


def _s5_state_out(fin):
    def lay(s):
        return s.transpose(2, 1, 0, 3).reshape(BATCH, 2, S5_G, S5_P)
    return lay(fin[..., :2 * S5_P]), lay(fin[..., 2 * S5_P:])


def _glu_kernel(ys_ref, x_ref, mod_ref, d_ref, w_ref, g_ref, b_ref, x1_ref, h2_ref):
    x = x_ref[...]
    h = _modulate(x, mod_ref, 0, 1)
    y = jax.nn.gelu(ys_ref[...] + d_ref[...] * h).astype(BF16)
    z = jnp.dot(y, w_ref[...], preferred_element_type=F32)
    f = z[:, :D] * jax.nn.sigmoid(z[:, D:])
    x1 = _layer_norm(ALPHA * x + (1.0 + mod_ref[2:3, :]) * f, g_ref[...], b_ref[...])
    x1_ref[...] = x1
    h2_ref[...] = _modulate(x1, mod_ref, 3, 4).astype(BF16)


def _glu(ysum, x, mod, d, w_glu, ln_g, ln_b):
    return pl.pallas_call(
        _glu_kernel,
        grid=(N_ROWS // TM,),
        in_specs=[_row_spec(TM), _row_spec(TM), _mod_spec(TM), _const_spec((1, D)),
                  _const_spec((D, 2 * D)), _const_spec((1, D)), _const_spec((1, D))],
        out_specs=[_row_spec(TM), _row_spec(TM)],
        out_shape=[jax.ShapeDtypeStruct((N_ROWS, D), F32), jax.ShapeDtypeStruct((N_ROWS, D), BF16)],
        compiler_params=_cparams(("parallel",)),
        name="glu",
    )(ysum, x, mod, d.reshape(1, D), w_glu, ln_g.reshape(1, D), ln_b.reshape(1, D))


def _finish(x1, g, mod_ref, g_ref, b_ref, x2_ref):
    x2 = _layer_norm(ALPHA * x1 + (1.0 + mod_ref[5:6, :]) * g, g_ref[...], b_ref[...])
    x2_ref[...] = x2
    return x2


FFN_TF = FFN_DIM // 2


def _ffn_kernel(h_ref, x1_ref, mod_ref, nmod_ref, wg_ref, wu_ref, wd_ref, g_ref, b_ref, x2_ref, hn_ref):
    h = h_ref[...]
    acc = None
    for c in range(FFN_DIM // FFN_TF):
        sl = slice(c * FFN_TF, (c + 1) * FFN_TF)
        a = jnp.dot(h, wg_ref[:, sl], preferred_element_type=F32)
        b = jnp.dot(h, wu_ref[:, sl], preferred_element_type=F32)
        p = jnp.dot((jax.nn.silu(a) * b).astype(BF16), wd_ref[sl, :], preferred_element_type=F32)
        acc = p if acc is None else acc + p
    x2 = _finish(x1_ref[...], acc, mod_ref, g_ref, b_ref, x2_ref)
    hn_ref[...] = _modulate(x2, nmod_ref, 0, 1).astype(BF16)


def _ffn(h2, x1, mod, nmod, wg, wu, wd, ln_g, ln_b):
    single = pl.Buffered(1)
    return pl.pallas_call(
        _ffn_kernel,
        grid=(N_ROWS // TM,),
        in_specs=[_row_spec(TM), _row_spec(TM), _mod_spec(TM), _mod_spec(TM),
                  pl.BlockSpec((D, FFN_DIM), lambda t: (0, 0), pipeline_mode=single),
                  pl.BlockSpec((D, FFN_DIM), lambda t: (0, 0), pipeline_mode=single),
                  pl.BlockSpec((FFN_DIM, D), lambda t: (0, 0), pipeline_mode=single),
                  _const_spec((1, D)), _const_spec((1, D))],
        out_specs=[_row_spec(TM), _row_spec(TM)],
        out_shape=[jax.ShapeDtypeStruct((N_ROWS, D), F32), jax.ShapeDtypeStruct((N_ROWS, D), BF16)],
        compiler_params=_cparams(("parallel",)),
        name="ffn",
    )(h2, x1, mod, nmod, wg, wu, wd, ln_g.reshape(1, D), ln_b.reshape(1, D))


def _gmlp_kernel(h_ref, x_ref, mod_ref, win_ref, bin_ref, vg_ref, vb_ref, ws_ref, bs_ref, wout_ref,
                 g_ref, b_ref, wr_ref, br_ref, x1_ref, h2_ref, idx_ref, p_ref, s_scr):
    z = jax.nn.gelu(jnp.dot(h_ref[...], win_ref[...], preferred_element_type=F32) + bin_ref[...])
    u = z[:, :D]
    v = _layer_norm(z[:, D:], vg_ref[...], vb_ref[...]).astype(BF16)
    n_chunks = TM // GM_CHUNK
    for hd in range(GM_HEADS):
        cols = slice(hd * GM_CHUNK, (hd + 1) * GM_CHUNK)
        rhs = jnp.concatenate([v[n * GM_CHUNK:(n + 1) * GM_CHUNK, cols] for n in range(n_chunks)], axis=-1)
        s = jnp.dot(ws_ref[hd], rhs, preferred_element_type=F32)
        for n in range(n_chunks):
            s_scr[n * GM_CHUNK:(n + 1) * GM_CHUNK, cols] = s[:, n * GM_CHUNK:(n + 1) * GM_CHUNK] + bs_ref[hd]
    f = jnp.dot((u * s_scr[...]).astype(BF16), wout_ref[...], preferred_element_type=F32)
    x = x_ref[...]
    x1 = _layer_norm(ALPHA * x + (1.0 + mod_ref[2:3, :]) * f, g_ref[...], b_ref[...])
    x1_ref[...] = x1
    h2 = _modulate(x1, mod_ref, 3, 4)
    h2_ref[...] = h2
    logits = jnp.dot(h2, wr_ref[...], preferred_element_type=F32, precision=lax.Precision.HIGHEST) + br_ref[...]
    lane = lax.broadcasted_iota(jnp.int32, logits.shape, 1)
    neg = jnp.float32(-jnp.inf)
    logits = jnp.where(lane < N_EXPERTS, logits, neg)
    v1 = jnp.max(logits, axis=-1, keepdims=True)
    i1 = jnp.min(jnp.where(logits == v1, lane, 128), axis=-1, keepdims=True)
    rest = jnp.where(lane == i1, neg, logits)
    v2 = jnp.max(rest, axis=-1, keepdims=True)
    i2 = jnp.min(jnp.where(rest == v2, lane, 128), axis=-1, keepdims=True)
    e2 = jnp.exp(v2 - v1)
    p1 = 1.0 / (1.0 + e2)
    p2 = e2 / (1.0 + e2)
    idx_ref[...] = jnp.where(lane == 0, i1, jnp.where(lane == 1, i2, 0))
    p_ref[...] = jnp.where(lane == 0, p1, jnp.where(lane == 1, p2, 0.0))


def _gmlp(h1, x, mod, w_in, b_in, vg, vb, w_s, b_s, w_out, ln_g, ln_b, w_r, b_r):
    return pl.pallas_call(
        _gmlp_kernel,
        grid=(N_ROWS // TM,),
        in_specs=[_row_spec(TM), _row_spec(TM), _mod_spec(TM),
                  _const_spec((D, 2 * D)), _const_spec((1, 2 * D)), _const_spec((1, D)), _const_spec((1, D)),
                  _const_spec((GM_HEADS, GM_CHUNK, GM_CHUNK)), _const_spec((GM_HEADS, GM_CHUNK, GM_CHUNK)),
                  _const_spec((D, D)), _const_spec((1, D)), _const_spec((1, D)),
                  _const_spec((D, 128)), _const_spec((1, 128))],
        out_specs=[_row_spec(TM), _row_spec(TM), _row_spec(TM, 128), _row_spec(TM, 128)],
        out_shape=[jax.ShapeDtypeStruct((N_ROWS, D), F32), jax.ShapeDtypeStruct((N_ROWS, D), F32),
                   jax.ShapeDtypeStruct((N_ROWS, 128), jnp.int32), jax.ShapeDtypeStruct((N_ROWS, 128), F32)],
        scratch_shapes=[pltpu.VMEM((TM, D), F32)],
        compiler_params=_cparams(("parallel",)),
        name="gmlp",
    )(h1, x, mod, w_in, b_in.reshape(1, 2 * D), vg.reshape(1, D), vb.reshape(1, D), w_s, b_s, w_out,
      ln_g.reshape(1, D), ln_b.reshape(1, D), w_r, b_r)


def _moe_kernel(te_ref, first_ref, used_ref, x_ref, wg_ref, wu_ref, wd_ref, y_ref, acc_ref, xb_ref, wgb, wub, wdb):
    t, f = pl.program_id(0), pl.program_id(1)
    live = t < used_ref[0]

    @pl.when(jnp.logical_and(live, first_ref[t] == 1))
    def _():
        wgb[f] = wg_ref[...].astype(BF16)
        wub[f] = wu_ref[...].astype(BF16)
        wdb[f] = wd_ref[...].astype(BF16)

    @pl.when(jnp.logical_and(live, f == 0))
    def _():
        xb_ref[...] = x_ref[...].astype(BF16)

    @pl.when(live)
    def _():
        x = xb_ref[...]
        a = jnp.dot(x, wgb[f], preferred_element_type=F32)
        b = jnp.dot(x, wub[f], preferred_element_type=F32)
        p = jnp.dot((jax.nn.silu(a) * b).astype(BF16), wdb[f], preferred_element_type=F32)

        @pl.when(f == 0)
        def _():
            acc_ref[...] = p

        @pl.when(f > 0)
        def _():
            acc_ref[...] += p

    @pl.when(f == pl.num_programs(1) - 1)
    def _():
        y_ref[...] = jnp.where(live, acc_ref[...], 0.0)


def _moe(layer, tile_expert, tile_first, n_used, xs, wg, wu, wd):
    nf = EXPERT_DIM // MOE_TF

    def chunk(t, f, first, used):
        return jnp.where(jnp.logical_and(t < used[0], first[t] == 1), f, nf - 1)

    def row(t, used):
        return jnp.minimum(t, used[0] - 1)

    grid_spec = pltpu.PrefetchScalarGridSpec(
        num_scalar_prefetch=3,
        grid=(MOE_TILES, nf),
        in_specs=[pl.BlockSpec((MOE_TM, D), lambda t, f, te, first, used: (row(t, used), 0)),
                  pl.BlockSpec((None, None, D, MOE_TF),
                               lambda t, f, te, first, used: (layer, te[t], 0, chunk(t, f, first, used))),
                  pl.BlockSpec((None, None, D, MOE_TF),
                               lambda t, f, te, first, used: (layer, te[t], 0, chunk(t, f, first, used))),
                  pl.BlockSpec((None, None, MOE_TF, D),
                               lambda t, f, te, first, used: (layer, te[t], chunk(t, f, first, used), 0))],
        out_specs=pl.BlockSpec((MOE_TM, D), lambda t, f, te, first, used: (t, 0)),
        scratch_shapes=[pltpu.VMEM((MOE_TM, D), F32), pltpu.VMEM((MOE_TM, D), BF16),
                        pltpu.VMEM((nf, D, MOE_TF), BF16), pltpu.VMEM((nf, D, MOE_TF), BF16),
                        pltpu.VMEM((nf, MOE_TF, D), BF16)],
    )
    return pl.pallas_call(
        _moe_kernel,
        grid_spec=grid_spec,
        out_shape=jax.ShapeDtypeStruct((MOE_ROWS, D), F32),
        compiler_params=_cparams(("arbitrary", "arbitrary")),
        name="moe",
    )(tile_expert, tile_first, n_used, xs, wg, wu, wd)


def _take_rows(a, rows):
    return a.at[rows].get(mode="promise_in_bounds")


def _route(idx):
    e = idx.reshape(-1)
    onehot = (e[:, None] == jnp.arange(N_EXPERTS, dtype=jnp.int32)[None, :]).astype(jnp.int32)
    counts = jnp.sum(onehot, axis=0)
    rank = jnp.sum((jnp.cumsum(onehot, axis=0) - 1) * onehot, axis=1)
    padded = ((counts + MOE_TM - 1) // MOE_TM) * MOE_TM
    ends = jnp.cumsum(padded)
    starts = ends - padded
    pos = starts[e] + rank
    token = jnp.arange(N_ROWS * TOP_K, dtype=jnp.int32) // TOP_K
    src = jnp.zeros((MOE_ROWS,), jnp.int32).at[pos].set(token)
    n_used = (ends[-1] // MOE_TM).astype(jnp.int32)
    tile_start = jnp.minimum(jnp.arange(MOE_TILES, dtype=jnp.int32), n_used - 1) * MOE_TM
    tile_expert = jnp.sum((tile_start[:, None] >= ends[None, :]).astype(jnp.int32), axis=1)
    tile_expert = jnp.minimum(tile_expert, N_EXPERTS - 1).astype(jnp.int32)
    prev = jnp.concatenate([jnp.full((1,), -1, jnp.int32), tile_expert[:-1]])
    tile_first = (tile_expert != prev).astype(jnp.int32)
    return src, pos.reshape(N_ROWS, TOP_K), tile_expert, tile_first, n_used.reshape(1)


def _combine_kernel(ya_ref, yb_ref, p_ref, x1_ref, mod_ref, g_ref, b_ref, x2_ref):
    g = p_ref[:, 0:1] * ya_ref[...] + p_ref[:, 1:2] * yb_ref[...]
    _finish(x1_ref[...], g, mod_ref, g_ref, b_ref, x2_ref)


def _combine(ya, yb, p, x1, mod, ln_g, ln_b):
    return pl.pallas_call(
        _combine_kernel,
        grid=(N_ROWS // TM,),
        in_specs=[_row_spec(TM), _row_spec(TM), _row_spec(TM, 128), _row_spec(TM), _mod_spec(TM),
                  _const_spec((1, D)), _const_spec((1, D))],
        out_specs=_row_spec(TM),
        out_shape=jax.ShapeDtypeStruct((N_ROWS, D), F32),
        compiler_params=_cparams(("parallel",)),
        name="combine",
    )(ya, yb, p, x1, mod, ln_g.reshape(1, D), ln_b.reshape(1, D))


def _grid_pos_embed():
    rows = DEC_SEQ // GRID_W
    r, col = jnp.meshgrid(jnp.arange(rows, dtype=F32), jnp.arange(GRID_W, dtype=F32), indexing="ij")
    quarter = D // 4
    freq = 1.0 / (10000.0 ** (jnp.arange(quarter, dtype=F32) / quarter))

    def emb(p):
        ang = p.reshape(-1)[:, None] * freq[None, :]
        return jnp.concatenate([jnp.sin(ang), jnp.cos(ang)], axis=-1)

    return jnp.concatenate([emb(r), emb(col)], axis=-1)


def kernel(x_prompt, x_sample, c, state_ssm_re, state_ssm_im, c_ctx, w_ada, b_ada, ln_g, ln_b, s5_a_re, s5_a_im, s5_log_step, s5_b_re, s5_b_im, s5_c_re, s5_c_im, s5_d, s5_w_glu, gm_w_in, gm_b_in, gm_ln_g, gm_ln_b, gm_w_s, gm_b_s, gm_w_out, ffn_w_gate, ffn_w_up, ffn_w_down, moe_w_router, moe_b_router, moe_w_gate, moe_w_up, moe_w_down):
    cond8 = jnp.concatenate([c_ctx[None, :], c, jnp.zeros((N_COND - 1 - DEC_BATCH, D), F32)], axis=0)
    mods = _ada(cond8, w_ada, b_ada).reshape(DEPTH, N_COND, 6, D)

    x = _prep(x_prompt, x_sample, _grid_pos_embed())
    h = None

    fin_re, fin_im = [], []
    for i in range(DEPTH):
        j = i // 2
        mod = mods[i]
        nmod = mods[min(i + 1, DEPTH - 1)]
        if i % 2 == 0:
            params = _s5_params(s5_a_re[j], s5_a_im[j], s5_log_step[j], s5_b_re[j], s5_b_im[j], s5_c_re[j], s5_c_im[j])
            ysum, fin = _s5(x, mod, params, _s5_state_in(state_ssm_re[:, j], state_ssm_im[:, j]))
            fr, fi = _s5_state_out(fin)
            fin_re.append(fr)
            fin_im.append(fi)
            x1, h2 = _glu(ysum, x, mod, s5_d[j], s5_w_glu[j].astype(BF16), ln_g[i, 0], ln_b[i, 0])
            x, h = _ffn(h2, x1, mod, nmod, ffn_w_gate[j].astype(BF16), ffn_w_up[j].astype(BF16),
                        ffn_w_down[j].astype(BF16), ln_g[i, 1], ln_b[i, 1])
        else:
            b_s = jnp.broadcast_to(gm_b_s[j][:, :, None], (GM_HEADS, GM_CHUNK, GM_CHUNK)).astype(F32)
            w_r = jnp.pad(moe_w_router[j].astype(F32), ((0, 0), (0, 128 - N_EXPERTS)))
            b_r = jnp.pad(moe_b_router[j].astype(F32), (0, 128 - N_EXPERTS)).reshape(1, 128)
            x1, h2, idx, p = _gmlp(h, x, mod, gm_w_in[j].astype(BF16), gm_b_in[j], gm_ln_g[j], gm_ln_b[j],
                                   gm_w_s[j].astype(BF16), b_s, gm_w_out[j].astype(BF16), ln_g[i, 0], ln_b[i, 0], w_r, b_r)
            src, pos, tile_expert, tile_first, n_used = _route(idx[:, :TOP_K])
            ys = _moe(j, tile_expert, tile_first, n_used, _take_rows(h2, src), moe_w_gate, moe_w_up, moe_w_down)
            x = _combine(_take_rows(ys, pos[:, 0]), _take_rows(ys, pos[:, 1]), p, x1, mod, ln_g[i, 1], ln_b[i, 1])

    y_prompt = x[:N_P].reshape(BATCH, SEQ, D)
    y_sample = x[N_P:].reshape(DEC_BATCH, DEC_SEQ, D)
    new_re = jnp.stack(fin_re, axis=1).astype(x_prompt.dtype)
    new_im = jnp.stack(fin_im, axis=1).astype(x_prompt.dtype)
    return (y_prompt, y_sample, new_re, new_im)
```

```python
import jax
import jax.numpy as jnp
from jax import lax
from jax.experimental import pallas as pl
from jax.experimental.pallas import tpu as pltpu

F32 = jnp.float32
BF16 = jnp.bfloat16

D = 1024
BATCH, SEQ = 32, 256
DEC_BATCH, DEC_SEQ = 4, 1024
DEPTH = 4
GRID_W = 64
N_P = BATCH * SEQ
N_S = DEC_BATCH * DEC_SEQ
N_ROWS = N_P + N_S
N_COND = 8

S5_CG, S5_P, S5_G = 16, 64, 64
S5_T = 16
S5_Q = S5_G // 2
P_CHUNKS = SEQ // S5_T
S_CHUNKS = DEC_SEQ // S5_T
S5_ROWS_P = P_CHUNKS * BATCH
S5_ROWS_S = S_CHUNKS * DEC_BATCH
S5_ROWS = S5_ROWS_P + S5_ROWS_S

GM_CHUNK, GM_HEADS = 128, 8
FFN_DIM = 2816
N_EXPERTS, TOP_K, EXPERT_DIM = 8, 2, 3584
ALPHA = (2.0 * DEPTH) ** 0.25
LN_EPS = 1e-5

TM = 512
ROW_SUB = TM
MOE_TM = 512
MOE_TF = 512
MOE_SUB = 256
MOE_XSPLIT = 2
MOE_TILES = (N_ROWS * TOP_K) // MOE_TM + N_EXPERTS
MOE_ROWS = MOE_TILES * MOE_TM
VMEM_LIMIT = 56 * 1024 * 1024


def _cparams(sem):
    return pltpu.CompilerParams(dimension_semantics=sem, vmem_limit_bytes=VMEM_LIMIT)


def _cond_of_tile(t, tm):
    row0 = t * tm
    return jnp.where(row0 < N_P, 0, 1 + (row0 - N_P) // DEC_SEQ)


def _mod_spec(tm):
    return pl.BlockSpec((None, 6, D), lambda t: (_cond_of_tile(t, tm), 0, 0))


def _row_spec(tm, width=D):
    return pl.BlockSpec((tm, width), lambda t: (t, 0))


def _const_spec(shape):
    return pl.BlockSpec(shape, lambda t: (0,) * len(shape))


def _layer_norm(r, g, b):
    mu = jnp.mean(r, axis=-1, keepdims=True)
    c = r - mu
    var = jnp.mean(c * c, axis=-1, keepdims=True)
    return c * lax.rsqrt(var + LN_EPS) * g + b


def _modulate(x, mod_ref, shift_row, scale_row):
    return x * (1.0 + mod_ref[scale_row:scale_row + 1, :]) + mod_ref[shift_row:shift_row + 1, :]


def _ada_kernel(c_ref, w_ref, b_ref, o_ref):
    c = c_ref[...]
    o_ref[...] = jnp.dot(jax.nn.silu(c), w_ref[...], preferred_element_type=F32) + b_ref[...]


def _ada(cond8, w_ada, b_ada):
    tn = 1536
    return pl.pallas_call(
        _ada_kernel,
        grid=(DEPTH, 6 * D // tn),
        in_specs=[pl.BlockSpec((N_COND, D), lambda i, n: (0, 0)),
                  pl.BlockSpec((None, D, tn), lambda i, n: (i, 0, n)),
                  pl.BlockSpec((None, 1, tn), lambda i, n: (i, 0, n))],
        out_specs=pl.BlockSpec((None, N_COND, tn), lambda i, n: (i, 0, n)),
        out_shape=jax.ShapeDtypeStruct((DEPTH, N_COND, 6 * D), F32),
        compiler_params=_cparams(("arbitrary", "arbitrary")),
        name="ada",
    )(cond8, w_ada, b_ada.reshape(DEPTH, 1, 6 * D))


def _prep_kernel(xp_ref, xs_ref, pos_ref, xo_ref):
    is_latent = pl.program_id(0) * TM >= N_P
    xo_ref[...] = jnp.where(is_latent, xs_ref[...] + pos_ref[...], xp_ref[...])


def _prep(x_prompt, x_sample, pos):
    p_tiles = N_P // TM

    def latent_tile(t):
        return jnp.maximum(t - p_tiles, 0)

    return pl.pallas_call(
        _prep_kernel,
        grid=(N_ROWS // TM,),
        in_specs=[pl.BlockSpec((TM, D), lambda t: (jnp.minimum(t, p_tiles - 1), 0)),
                  pl.BlockSpec((TM, D), lambda t: (latent_tile(t), 0)),
                  pl.BlockSpec((TM, D), lambda t: (latent_tile(t) % (DEC_SEQ // TM), 0))],
        out_specs=_row_spec(TM),
        out_shape=jax.ShapeDtypeStruct((N_ROWS, D), F32),
        compiler_params=_cparams(("parallel",)),
        name="prep",
    )(x_prompt.reshape(N_P, D), x_sample.reshape(N_S, D), pos)


S5_BLK = 4
S5_W = 2 * S5_T * S5_CG
S5_GR = 2 * S5_CG
_NT = (((1,), (1,)), ((), ()))


def _granule_transpose4(a, lane):
    low = lane < 64
    a0 = jnp.where(low, a[0], pltpu.roll(a[2], 64, 1))
    a2 = jnp.where(low, pltpu.roll(a[0], 64, 1), a[2])
    a1 = jnp.where(low, a[1], pltpu.roll(a[3], 64, 1))
    a3 = jnp.where(low, pltpu.roll(a[1], 64, 1), a[3])
    even = (lane // S5_GR) % 2 == 0
    return [jnp.where(even, a0, pltpu.roll(a1, 32, 1)), jnp.where(even, pltpu.roll(a0, 96, 1), a1),
            jnp.where(even, a2, pltpu.roll(a3, 32, 1)), jnp.where(even, pltpu.roll(a2, 96, 1), a3)]


def _s5_chunk_matrices(p4, bb_ref, c_ref, pw_ref, min_scr, mi_scr, p_scr):
    half = 2 * S5_P
    gran = lax.broadcasted_iota(jnp.int32, (S5_W, 128), 1) // S5_GR
    rep = (lax.broadcasted_iota(jnp.int32, (S5_GR, S5_W), 1) % S5_GR
           == lax.broadcasted_iota(jnp.int32, (S5_GR, S5_W), 0)).astype(BF16)
    mout = []
    for k in range(2):
        bb_re, bb_im = bb_ref[p4, k, 0], bb_ref[p4, k, 1]
        c_re, c_im = c_ref[p4, k, 0], c_ref[p4, k, 1]
        out_blocks = []
        for r in range(S5_T):
            t_in = S5_T - 1 - r if k == 0 else r
            t_out = r + 1 if k == 0 else S5_T - r
            rows = slice(S5_GR * r, S5_GR * (r + 1))
            pr, pi = pw_ref[p4, k, 0, t_in:t_in + 1, :], pw_ref[p4, k, 1, t_in:t_in + 1, :]
            min_scr[k, rows, 0:half] = bb_re * pr - bb_im * pi
            min_scr[k, rows, half:2 * half] = bb_re * pi + bb_im * pr
            pr, pi = pw_ref[p4, k, 0, t_out:t_out + 1, :], pw_ref[p4, k, 1, t_out:t_out + 1, :]
            out_blocks.append(jnp.concatenate([c_re * pr - c_im * pi, -(c_re * pi + c_im * pr)], axis=-1))
        mout.append(jnp.concatenate(out_blocks, axis=0).astype(BF16))
        c2 = jnp.concatenate([c_re, -c_im], axis=-1)
        z = lax.dot_general(min_scr[k], c2, _NT, preferred_element_type=F32, precision=lax.Precision.HIGHEST)
        zrep = jnp.dot(z.astype(BF16), rep, preferred_element_type=F32)
        zeros = jnp.zeros((S5_W, S5_W), F32)
        p_scr[0:S5_W, :] = zrep if k == 0 else zeros
        p_scr[S5_W:2 * S5_W, :] = zeros if k == 0 else zrep
        for j in range(S5_W // 128):
            cols = slice(128 * j, 128 * (j + 1))
            acc = None
            for s4 in range(4):
                s = 4 * j + s4
                start = S5_GR * (S5_T - 1 - s) if k == 0 else S5_W - S5_GR * s
                win = p_scr[start:start + S5_W, cols]
                acc = win if acc is None else jnp.where(gran == s4, win, acc)
            mi_scr[k, :, cols] = acc.astype(BF16)
    return mout


def _s5_scan(k, s_scrs, h_scrs, row0, nb, nchunks, hre, him, ar, ai):
    for n in (range(nchunks - 1, -1, -1) if k == 1 else range(nchunks)):
        rows = pl.ds(row0 + n, nb, stride=nchunks)
        h_scrs[0][rows, :] = hre
        h_scrs[1][rows, :] = him
        sre = s_scrs[0][rows, :]
        sim = s_scrs[1][rows, :]
        hre, him = ar * hre - ai * him + sre, ar * him + ai * hre + sim
    return hre, him


def _s5_kernel(x_ref, mod_ref, bb_ref, c_ref, pw_ref, h0_ref, y_ref, fin_ref,
               u_scr, yp_scr, sre_scr, sim_scr, hre_scr, him_scr, min_scr, mi_scr, p_scr):
    half = 2 * S5_P
    lane = lax.broadcasted_iota(jnp.int32, (S5_ROWS, 128), 1)
    scale = jnp.concatenate(
        [jnp.broadcast_to(1.0 + mod_ref[0, 1:2, :], (S5_ROWS_P, 128))]
        + [jnp.broadcast_to(1.0 + mod_ref[1 + b, 1:2, :], (S_CHUNKS, 128)) for b in range(DEC_BATCH)], axis=0)
    shift = jnp.concatenate(
        [jnp.broadcast_to(mod_ref[0, 0:1, :], (S5_ROWS_P, 128))]
        + [jnp.broadcast_to(mod_ref[1 + b, 0:1, :], (S_CHUNKS, 128)) for b in range(DEC_BATCH)], axis=0)
    for j in range(S5_T // 4):
        a = [x_ref[pl.ds(4 * j + s4, S5_ROWS, stride=S5_T), :] * scale + shift for s4 in range(4)]
        b = _granule_transpose4(a, lane)
        for p4 in range(S5_BLK):
            u_scr[p4, :, 128 * j:128 * (j + 1)] = b[p4].astype(BF16)

    for p4 in range(S5_BLK):
        mout = _s5_chunk_matrices(p4, bb_ref, c_ref, pw_ref, min_scr, mi_scr, p_scr)
        u = u_scr[p4]
        u0, u1 = u[:, 0:256], u[:, 256:512]
        y = None
        for k in range(2):
            s = jnp.dot(u, min_scr[k].astype(BF16), preferred_element_type=F32)
            sre_scr[...] = s[:, 0:half]
            sim_scr[...] = s[:, half:2 * half]
            s_scrs, h_scrs = (sre_scr, sim_scr), (hre_scr, him_scr)
            ar_p = jnp.broadcast_to(pw_ref[p4, k, 0, S5_T:S5_T + 1, :], (BATCH, half))
            ai_p = jnp.broadcast_to(pw_ref[p4, k, 1, S5_T:S5_T + 1, :], (BATCH, half))
            zero = jnp.zeros((BATCH, half), F32)
            fre, fim = _s5_scan(k, s_scrs, h_scrs, 0, BATCH, P_CHUNKS, zero, zero, ar_p, ai_p)
            fin_ref[p4, k, :, 0:half] = fre
            fin_ref[p4, k, :, half:2 * half] = fim
            _s5_scan(k, s_scrs, h_scrs, S5_ROWS_P, DEC_BATCH, S_CHUNKS,
                     h0_ref[p4, k, :, 0:half], h0_ref[p4, k, :, half:2 * half], ar_p[:DEC_BATCH], ai_p[:DEC_BATCH])
            hin = jnp.concatenate([hre_scr[...], him_scr[...]], axis=-1).astype(BF16)
            readout = lax.dot_general(hin, mout[k], _NT, preferred_element_type=F32)
            m00 = jnp.dot(u0, mi_scr[k, 0:256, 0:256], preferred_element_type=F32)
            m11 = jnp.dot(u1, mi_scr[k, 256:512, 256:512], preferred_element_type=F32)
            if k == 0:
                m11 = m11 + jnp.dot(u0, mi_scr[k, 0:256, 256:512], preferred_element_type=F32)
            else:
                m00 = m00 + jnp.dot(u1, mi_scr[k, 256:512, 0:256], preferred_element_type=F32)
            yk = jnp.concatenate([m00, m11], axis=-1) + readout
            y = yk if y is None else y + yk
        yp_scr[p4] = y

    for j in range(S5_T // 4):
        b = [yp_scr[p4, :, 128 * j:128 * (j + 1)] for p4 in range(S5_BLK)]
        a = _granule_transpose4(b, lane)
        for s4 in range(4):
            y_ref[pl.ds(4 * j + s4, S5_ROWS, stride=S5_T), :] = a[s4]


def _s5(x, mods, params, h0):
    bb, cc, pw = params
    nblk = S5_Q // S5_BLK
    return pl.pallas_call(
        _s5_kernel,
        grid=(nblk,),
        in_specs=[pl.BlockSpec((N_ROWS, 128), lambda q: (0, q)),
                  pl.BlockSpec((N_COND, 6, 128), lambda q: (0, 0, q)),
                  pl.BlockSpec((S5_BLK, 2, 2, S5_GR, 128), lambda q: (q, 0, 0, 0, 0)),
                  pl.BlockSpec((S5_BLK, 2, 2, S5_GR, 128), lambda q: (q, 0, 0, 0, 0)),
                  pl.BlockSpec((S5_BLK, 2, 2, S5_T + 1, 128), lambda q: (q, 0, 0, 0, 0)),
                  pl.BlockSpec((S5_BLK, 2, DEC_BATCH, 256), lambda q: (q, 0, 0, 0))],
        out_specs=[pl.BlockSpec((N_ROWS, 128), lambda q: (0, q)),
                   pl.BlockSpec((S5_BLK, 2, BATCH, 256), lambda q: (q, 0, 0, 0))],
        out_shape=[jax.ShapeDtypeStruct((N_ROWS, D), F32),
                   jax.ShapeDtypeStruct((S5_Q, 2, BATCH, 256), F32)],
        scratch_shapes=[pltpu.VMEM((S5_BLK, S5_ROWS, S5_W), BF16), pltpu.VMEM((S5_BLK, S5_ROWS, S5_W), F32),
                        pltpu.VMEM((S5_ROWS, 128), F32), pltpu.VMEM((S5_ROWS, 128), F32),
                        pltpu.VMEM((S5_ROWS, 128), F32), pltpu.VMEM((S5_ROWS, 128), F32),
                        pltpu.VMEM((2, S5_W, 256), F32),
                        pltpu.VMEM((2, S5_W, S5_W), BF16), pltpu.VMEM((2 * S5_W, S5_W), F32)],
        compiler_params=_cparams(("parallel",)),
        name="s5",
    )(x, mods, bb, cc, pw, h0)


def _s5_params(a_re, a_im, log_step, b_re, b_im, c_re, c_im):
    lam_re = jnp.minimum(a_re.astype(F32), -1e-4)
    lam_im = a_im.astype(F32)
    dt = jnp.exp(log_step.astype(F32))[..., None]
    j = jnp.arange(S5_T + 1, dtype=F32)[:, None, None, None]
    mag = jnp.exp(lam_re * dt * j)
    pw_re = mag * jnp.cos(lam_im * dt * j)
    pw_im = mag * jnp.sin(lam_im * dt * j)
    nr, ni = pw_re[1] - 1.0, pw_im[1]
    den = lam_re * lam_re + lam_im * lam_im
    q_re = (nr * lam_re + ni * lam_im) / den
    q_im = (ni * lam_re - nr * lam_im) / den
    b_re, b_im = b_re.astype(F32), b_im.astype(F32)
    bb_re = q_re[..., None] * b_re - q_im[..., None] * b_im
    bb_im = q_re[..., None] * b_im + q_im[..., None] * b_re
    eye = jnp.eye(2, dtype=F32)

    def pair_rows(m, spec):
        m = m.reshape(2, S5_Q, 2, *m.shape[2:])
        return jnp.einsum(spec, m, eye).reshape(S5_Q, 2, S5_GR, 2 * S5_P)

    bb = jnp.stack([pair_rows(bb_re, "kqgpc,gh->qkgchp"), pair_rows(bb_im, "kqgpc,gh->qkgchp")], axis=2)
    cc = jnp.stack([pair_rows(c_re.astype(F32), "kqgcp,gh->qkgchp"),
                    pair_rows(c_im.astype(F32), "kqgcp,gh->qkgchp")], axis=2)

    def pw_lay(p):
        return p.reshape(S5_T + 1, 2, S5_Q, 2 * S5_P).transpose(2, 1, 0, 3)
    pw = jnp.stack([pw_lay(pw_re), pw_lay(pw_im)], axis=2)
    return bb, cc, pw


def _s5_state_in(st_re, st_im):
    def lay(s):
        return s.astype(F32).reshape(DEC_BATCH, 2, S5_Q, 2 * S5_P).transpose(2, 1, 0, 3)
    return jnp.concatenate([lay(st_re), lay(st_im)], axis=-1)


def _s5_state_out(fin):
    def lay(s):
        return s.transpose(2, 1, 0, 3).reshape(BATCH, 2, S5_G, S5_P)
    return lay(fin[..., :2 * S5_P]), lay(fin[..., 2 * S5_P:])


def _glu_kernel(ys_ref, x_ref, mod_ref, d_ref, w_ref, g_ref, b_ref, x1_ref, h2_ref):
    for r in range(TM // ROW_SUB):
        rows = slice(r * ROW_SUB, (r + 1) * ROW_SUB)
        x = x_ref[rows, :]
        h = _modulate(x, mod_ref, 0, 1)
        y = jax.nn.gelu(ys_ref[rows, :] + d_ref[...] * h).astype(BF16)
        z = jnp.dot(y, w_ref[...], preferred_element_type=F32)
        f = z[:, :D] * jax.nn.sigmoid(z[:, D:])
        x1 = _layer_norm(ALPHA * x + (1.0 + mod_ref[2:3, :]) * f, g_ref[...], b_ref[...])
        x1_ref[rows, :] = x1
        h2_ref[rows, :] = _modulate(x1, mod_ref, 3, 4).astype(BF16)


def _glu(ysum, x, mod, d, w_glu, ln_g, ln_b):
    return pl.pallas_call(
        _glu_kernel,
        grid=(N_ROWS // TM,),
        in_specs=[_row_spec(TM), _row_spec(TM), _mod_spec(TM), _const_spec((1, D)),
                  _const_spec((D, 2 * D)), _const_spec((1, D)), _const_spec((1, D))],
        out_specs=[_row_spec(TM), _row_spec(TM)],
        out_shape=[jax.ShapeDtypeStruct((N_ROWS, D), F32), jax.ShapeDtypeStruct((N_ROWS, D), BF16)],
        compiler_params=_cparams(("parallel",)),
        name="glu",
    )(ysum, x, mod, d.reshape(1, D), w_glu, ln_g.reshape(1, D), ln_b.reshape(1, D))


def _finish(x1, g, mod_ref, g_ref, b_ref, x2_ref):
    x2 = _layer_norm(ALPHA * x1 + (1.0 + mod_ref[5:6, :]) * g, g_ref[...], b_ref[...])
    x2_ref[...] = x2
    return x2


FFN_TF = FFN_DIM // 2


def _ffn_kernel(h_ref, x1_ref, mod_ref, nmod_ref, wg_ref, wu_ref, wd_ref, g_ref, b_ref, x2_ref, hn_ref):
    h = h_ref[...]
    acc = None
    for c in range(FFN_DIM // FFN_TF):
        sl = slice(c * FFN_TF, (c + 1) * FFN_TF)
        a = jnp.dot(h, wg_ref[:, sl], preferred_element_type=F32)
        b = jnp.dot(h, wu_ref[:, sl], preferred_element_type=F32)
        p = jnp.dot((jax.nn.silu(a) * b).astype(BF16), wd_ref[sl, :], preferred_element_type=F32)
        acc = p if acc is None else acc + p
    x2 = _finish(x1_ref[...], acc, mod_ref, g_ref, b_ref, x2_ref)
    hn_ref[...] = _modulate(x2, nmod_ref, 0, 1).astype(BF16)


def _ffn(h2, x1, mod, nmod, wg, wu, wd, ln_g, ln_b):
    single = pl.Buffered(1)
    return pl.pallas_call(
        _ffn_kernel,
        grid=(N_ROWS // TM,),
        in_specs=[_row_spec(TM), _row_spec(TM), _mod_spec(TM), _mod_spec(TM),
                  pl.BlockSpec((D, FFN_DIM), lambda t: (0, 0), pipeline_mode=single),
                  pl.BlockSpec((D, FFN_DIM), lambda t: (0, 0), pipeline_mode=single),
                  pl.BlockSpec((FFN_DIM, D), lambda t: (0, 0), pipeline_mode=single),
                  _const_spec((1, D)), _const_spec((1, D))],
        out_specs=[_row_spec(TM), _row_spec(TM)],
        out_shape=[jax.ShapeDtypeStruct((N_ROWS, D), F32), jax.ShapeDtypeStruct((N_ROWS, D), BF16)],
        compiler_params=_cparams(("parallel",)),
        name="ffn",
    )(h2, x1, mod, nmod, wg, wu, wd, ln_g.reshape(1, D), ln_b.reshape(1, D))


def _gmlp_kernel(h_ref, x_ref, mod_ref, win_ref, bin_ref, vg_ref, vb_ref, ws_ref, bs_ref, wout_ref,
                 g_ref, b_ref, wr_ref, br_ref, x1_ref, h2_ref, idx_ref, p_ref, s_scr):
    for r in range(TM // ROW_SUB):
        rows = slice(r * ROW_SUB, (r + 1) * ROW_SUB)
        z = jax.nn.gelu(jnp.dot(h_ref[rows, :], win_ref[...], preferred_element_type=F32) + bin_ref[...])
        u = z[:, :D]
        v = _layer_norm(z[:, D:], vg_ref[...], vb_ref[...]).astype(BF16)
        n_chunks = ROW_SUB // GM_CHUNK
        for hd in range(GM_HEADS):
            cols = slice(hd * GM_CHUNK, (hd + 1) * GM_CHUNK)
            rhs = jnp.concatenate([v[n * GM_CHUNK:(n + 1) * GM_CHUNK, cols] for n in range(n_chunks)], axis=-1)
            s = jnp.dot(ws_ref[hd], rhs, preferred_element_type=F32)
            for n in range(n_chunks):
                lo = r * ROW_SUB + n * GM_CHUNK
                s_scr[lo:lo + GM_CHUNK, cols] = s[:, n * GM_CHUNK:(n + 1) * GM_CHUNK] + bs_ref[hd]
        f = jnp.dot((u * s_scr[rows, :]).astype(BF16), wout_ref[...], preferred_element_type=F32)
        x1 = _layer_norm(ALPHA * x_ref[rows, :] + (1.0 + mod_ref[2:3, :]) * f, g_ref[...], b_ref[...])
        x1_ref[rows, :] = x1
        h2 = _modulate(x1, mod_ref, 3, 4)
        h2_ref[rows, :] = h2
        h_hi = h2.astype(BF16)
        h_lo = (h2 - h_hi.astype(F32)).astype(BF16)
        w_hi, w_lo = wr_ref[0], wr_ref[1]
        logits = (jnp.dot(h_hi, w_hi, preferred_element_type=F32) + jnp.dot(h_lo, w_hi, preferred_element_type=F32)
                  + jnp.dot(h_hi, w_lo, preferred_element_type=F32)) + br_ref[...]
        lane = lax.broadcasted_iota(jnp.int32, logits.shape, 1)
        neg = jnp.float32(-jnp.inf)
        logits = jnp.where(lane < N_EXPERTS, logits, neg)
        v1 = jnp.max(logits, axis=-1, keepdims=True)
        i1 = jnp.min(jnp.where(logits == v1, lane, 128), axis=-1, keepdims=True)
        rest = jnp.where(lane == i1, neg, logits)
        v2 = jnp.max(rest, axis=-1, keepdims=True)
        i2 = jnp.min(jnp.where(rest == v2, lane, 128), axis=-1, keepdims=True)
        e2 = jnp.exp(v2 - v1)
        p1 = 1.0 / (1.0 + e2)
        p2 = e2 / (1.0 + e2)
        idx_ref[rows, :] = jnp.where(lane == 0, i1, jnp.where(lane == 1, i2, 0))
        p_ref[rows, :] = jnp.where(lane == 0, p1, jnp.where(lane == 1, p2, 0.0))


def _gmlp(h1, x, mod, w_in, b_in, vg, vb, w_s, b_s, w_out, ln_g, ln_b, w_r, b_r):
    return pl.pallas_call(
        _gmlp_kernel,
        grid=(N_ROWS // TM,),
        in_specs=[_row_spec(TM), _row_spec(TM), _mod_spec(TM),
                  _const_spec((D, 2 * D)), _const_spec((1, 2 * D)), _const_spec((1, D)), _const_spec((1, D)),
                  _const_spec((GM_HEADS, GM_CHUNK, GM_CHUNK)), _const_spec((GM_HEADS, GM_CHUNK, GM_CHUNK)),
                  _const_spec((D, D)), _const_spec((1, D)), _const_spec((1, D)),
                  _const_spec((2, D, 128)), _const_spec((1, 128))],
        out_specs=[_row_spec(TM), _row_spec(TM), _row_spec(TM, 128), _row_spec(TM, 128)],
        out_shape=[jax.ShapeDtypeStruct((N_ROWS, D), F32), jax.ShapeDtypeStruct((N_ROWS, D), F32),
                   jax.ShapeDtypeStruct((N_ROWS, 128), jnp.int32), jax.ShapeDtypeStruct((N_ROWS, 128), F32)],
        scratch_shapes=[pltpu.VMEM((TM, D), F32)],
        compiler_params=_cparams(("parallel",)),
        name="gmlp",
    )(h1, x, mod, w_in, b_in.reshape(1, 2 * D), vg.reshape(1, D), vb.reshape(1, D), w_s, b_s, w_out,
      ln_g.reshape(1, D), ln_b.reshape(1, D), w_r, b_r)


def _moe_kernel(te_ref, first_ref, used_ref, *refs):
    x_refs = refs[:MOE_XSPLIT]
    wg_ref, wu_ref, wd_ref, y_ref, acc_ref, xb_ref, wgb, wub, wdb = refs[MOE_XSPLIT:]
    t, f = pl.program_id(0), pl.program_id(1)
    live = t < used_ref[0]

    @pl.when(jnp.logical_and(live, first_ref[t] == 1))
    def _():
        wgb[f] = wg_ref[...].astype(BF16)
        wub[f] = wu_ref[...].astype(BF16)
        wdb[f] = wd_ref[...].astype(BF16)

    for part, x_ref in enumerate(x_refs):
        @pl.when(jnp.logical_and(jnp.logical_and(live, f == 0), t // (MOE_TILES // MOE_XSPLIT) == part))
        def _(x_ref=x_ref):
            xb_ref[...] = x_ref[...].astype(BF16)

    @pl.when(live)
    def _():
        for r in range(MOE_TM // MOE_SUB):
            rows = slice(r * MOE_SUB, (r + 1) * MOE_SUB)
            x = xb_ref[rows, :]
            a = jnp.dot(x, wgb[f], preferred_element_type=F32)
            b = jnp.dot(x, wub[f], preferred_element_type=F32)
            p = jnp.dot((jax.nn.silu(a) * b).astype(BF16), wdb[f], preferred_element_type=F32)
            acc_ref[rows, :] = jnp.where(f == 0, p, acc_ref[rows, :] + p)

    @pl.when(f == pl.num_programs(1) - 1)
    def _():
        y_ref[...] = jnp.where(live, acc_ref[...], 0.0)


def _moe(layer, tile_expert, tile_first, n_used, xs, wg, wu, wd):
    nf = EXPERT_DIM // MOE_TF

    def chunk(t, f, first, used):
        return jnp.where(jnp.logical_and(t < used[0], first[t] == 1), f, nf - 1)

    part_tiles = MOE_TILES // MOE_XSPLIT

    def x_spec(part):
        def index(t, f, te, first, used):
            return (jnp.clip(jnp.minimum(t, used[0] - 1) - part * part_tiles, 0, part_tiles - 1), 0)
        return pl.BlockSpec((MOE_TM, D), index)

    grid_spec = pltpu.PrefetchScalarGridSpec(
        num_scalar_prefetch=3,
        grid=(MOE_TILES, nf),
        in_specs=[x_spec(part) for part in range(MOE_XSPLIT)] + [
                  pl.BlockSpec((None, None, D, MOE_TF),
                               lambda t, f, te, first, used: (layer, te[t], 0, chunk(t, f, first, used))),
                  pl.BlockSpec((None, None, D, MOE_TF),
                               lambda t, f, te, first, used: (layer, te[t], 0, chunk(t, f, first, used))),
                  pl.BlockSpec((None, None, MOE_TF, D),
                               lambda t, f, te, first, used: (layer, te[t], chunk(t, f, first, used), 0))],
        out_specs=pl.BlockSpec((MOE_TM, D), lambda t, f, te, first, used: (t, 0)),
        scratch_shapes=[pltpu.VMEM((MOE_TM, D), F32), pltpu.VMEM((MOE_TM, D), BF16),
                        pltpu.VMEM((nf, D, MOE_TF), BF16), pltpu.VMEM((nf, D, MOE_TF), BF16),
                        pltpu.VMEM((nf, MOE_TF, D), BF16)],
    )
    return pl.pallas_call(
        _moe_kernel,
        grid_spec=grid_spec,
        out_shape=jax.ShapeDtypeStruct((MOE_ROWS, D), F32),
        compiler_params=_cparams(("arbitrary", "arbitrary")),
        name="moe",
    )(tile_expert, tile_first, n_used, *xs, wg, wu, wd)


def _take_rows(a, rows):
    return a.at[rows].get(mode="promise_in_bounds")


def _route(idx):
    e = idx.reshape(-1)
    onehot = (e[:, None] == jnp.arange(N_EXPERTS, dtype=jnp.int32)[None, :]).astype(jnp.int32)
    counts = jnp.sum(onehot, axis=0)
    rank = jnp.sum((jnp.cumsum(onehot, axis=0) - 1) * onehot, axis=1)
    padded = ((counts + MOE_TM - 1) // MOE_TM) * MOE_TM
    ends = jnp.cumsum(padded)
    starts = ends - padded
    pos = starts[e] + rank
    token = jnp.arange(N_ROWS * TOP_K, dtype=jnp.int32) // TOP_K
    src = jnp.zeros((MOE_ROWS,), jnp.int32).at[pos].set(token)
    n_used = (ends[-1] // MOE_TM).astype(jnp.int32)
    tile_start = jnp.minimum(jnp.arange(MOE_TILES, dtype=jnp.int32), n_used - 1) * MOE_TM
    tile_expert = jnp.sum((tile_start[:, None] >= ends[None, :]).astype(jnp.int32), axis=1)
    tile_expert = jnp.minimum(tile_expert, N_EXPERTS - 1).astype(jnp.int32)
    prev = jnp.concatenate([jnp.full((1,), -1, jnp.int32), tile_expert[:-1]])
    tile_first = (tile_expert != prev).astype(jnp.int32)
    return src, pos.reshape(N_ROWS, TOP_K), tile_expert, tile_first, n_used.reshape(1)


def _combine_kernel(ya_ref, yb_ref, p_ref, x1_ref, mod_ref, g_ref, b_ref, x2_ref):
    g = p_ref[:, 0:1] * ya_ref[...] + p_ref[:, 1:2] * yb_ref[...]
    _finish(x1_ref[...], g, mod_ref, g_ref, b_ref, x2_ref)


def _combine(ya, yb, p, x1, mod, ln_g, ln_b):
    return pl.pallas_call(
        _combine_kernel,
        grid=(N_ROWS // TM,),
        in_specs=[_row_spec(TM), _row_spec(TM), _row_spec(TM, 128), _row_spec(TM), _mod_spec(TM),
                  _const_spec((1, D)), _const_spec((1, D))],
        out_specs=_row_spec(TM),
        out_shape=jax.ShapeDtypeStruct((N_ROWS, D), F32),
        compiler_params=_cparams(("parallel",)),
        name="combine",
    )(ya, yb, p, x1, mod, ln_g.reshape(1, D), ln_b.reshape(1, D))


def _grid_pos_embed():
    rows = DEC_SEQ // GRID_W
    r, col = jnp.meshgrid(jnp.arange(rows, dtype=F32), jnp.arange(GRID_W, dtype=F32), indexing="ij")
    quarter = D // 4
    freq = 1.0 / (10000.0 ** (jnp.arange(quarter, dtype=F32) / quarter))

    def emb(p):
        ang = p.reshape(-1)[:, None] * freq[None, :]
        return jnp.concatenate([jnp.sin(ang), jnp.cos(ang)], axis=-1)

    return jnp.concatenate([emb(r), emb(col)], axis=-1)


def kernel(x_prompt, x_sample, c, state_ssm_re, state_ssm_im, c_ctx, w_ada, b_ada, ln_g, ln_b, s5_a_re, s5_a_im, s5_log_step, s5_b_re, s5_b_im, s5_c_re, s5_c_im, s5_d, s5_w_glu, gm_w_in, gm_b_in, gm_ln_g, gm_ln_b, gm_w_s, gm_b_s, gm_w_out, ffn_w_gate, ffn_w_up, ffn_w_down, moe_w_router, moe_b_router, moe_w_gate, moe_w_up, moe_w_down):
    cond8 = jnp.concatenate([c_ctx[None, :], c, jnp.zeros((N_COND - 1 - DEC_BATCH, D), F32)], axis=0)
    mods = _ada(cond8, w_ada, b_ada).reshape(DEPTH, N_COND, 6, D)

    x = _prep(x_prompt, x_sample, _grid_pos_embed())
    h = None

    fin_re, fin_im = [], []
    for i in range(DEPTH):
        j = i // 2
        mod = mods[i]
        nmod = mods[min(i + 1, DEPTH - 1)]
        if i % 2 == 0:
            params = _s5_params(s5_a_re[j], s5_a_im[j], s5_log_step[j], s5_b_re[j], s5_b_im[j], s5_c_re[j], s5_c_im[j])
            ysum, fin = _s5(x, mod, params, _s5_state_in(state_ssm_re[:, j], state_ssm_im[:, j]))
            fr, fi = _s5_state_out(fin)
            fin_re.append(fr)
            fin_im.append(fi)
            x1, h2 = _glu(ysum, x, mod, s5_d[j], s5_w_glu[j].astype(BF16), ln_g[i, 0], ln_b[i, 0])
            x, h = _ffn(h2, x1, mod, nmod, ffn_w_gate[j].astype(BF16), ffn_w_up[j].astype(BF16),
                        ffn_w_down[j].astype(BF16), ln_g[i, 1], ln_b[i, 1])
        else:
            b_s = jnp.broadcast_to(gm_b_s[j][:, :, None], (GM_HEADS, GM_CHUNK, GM_CHUNK)).astype(F32)
            w_r = jnp.pad(moe_w_router[j].astype(F32), ((0, 0), (0, 128 - N_EXPERTS)))
            w_r_hi = w_r.astype(BF16)
            w_r = jnp.stack([w_r_hi, (w_r - w_r_hi.astype(F32)).astype(BF16)], axis=0)
            b_r = jnp.pad(moe_b_router[j].astype(F32), (0, 128 - N_EXPERTS)).reshape(1, 128)
            x1, h2, idx, p = _gmlp(h, x, mod, gm_w_in[j].astype(BF16), gm_b_in[j], gm_ln_g[j], gm_ln_b[j],
                                   gm_w_s[j].astype(BF16), b_s, gm_w_out[j].astype(BF16), ln_g[i, 0], ln_b[i, 0], w_r, b_r)
            src, pos, tile_expert, tile_first, n_used = _route(idx[:, :TOP_K])
            part_rows = MOE_ROWS // MOE_XSPLIT
            xs = [_take_rows(h2, src[part * part_rows:(part + 1) * part_rows]) for part in range(MOE_XSPLIT)]
            ys = _moe(j, tile_expert, tile_first, n_used, xs, moe_w_gate, moe_w_up, moe_w_down)
            x = _combine(_take_rows(ys, pos[:, 0]), _take_rows(ys, pos[:, 1]), p, x1, mod, ln_g[i, 1], ln_b[i, 1])

    y_prompt = x[:N_P].reshape(BATCH, SEQ, D)
    y_sample = x[N_P:].reshape(DEC_BATCH, DEC_SEQ, D)
    new_re = jnp.stack(fin_re, axis=1).astype(x_prompt.dtype)
    new_im = jnp.stack(fin_im, axis=1).astype(x_prompt.dtype)
    return (y_prompt, y_sample, new_re, new_im)
```

```python
import jax
import jax.numpy as jnp
from jax import lax
from jax.experimental import pallas as pl
from jax.experimental.pallas import tpu as pltpu

F32 = jnp.float32
BF16 = jnp.bfloat16

D = 1024
BATCH, SEQ = 32, 256
DEC_BATCH, DEC_SEQ = 4, 1024
DEPTH = 4
GRID_W = 64
N_P = BATCH * SEQ
N_S = DEC_BATCH * DEC_SEQ
N_ROWS = N_P + N_S
N_COND = 8

S5_CG, S5_P, S5_G = 16, 64, 64
S5_T = 16
S5_Q = S5_G // 2
P_CHUNKS = SEQ // S5_T
S_CHUNKS = DEC_SEQ // S5_T
S5_ROWS_P = P_CHUNKS * BATCH
S5_ROWS_S = S_CHUNKS * DEC_BATCH
S5_ROWS = S5_ROWS_P + S5_ROWS_S

GM_CHUNK, GM_HEADS = 128, 8
FFN_DIM = 2816
N_EXPERTS, TOP_K, EXPERT_DIM = 8, 2, 3584
ALPHA = (2.0 * DEPTH) ** 0.25
LN_EPS = 1e-5

TM = 512
ROW_SUB = TM
MOE_TM = 512
MOE_TF = 512
MOE_TILES = (N_ROWS * TOP_K) // MOE_TM + N_EXPERTS
MOE_ROWS = MOE_TILES * MOE_TM
VMEM_LIMIT = 56 * 1024 * 1024


def _cparams(sem):
    return pltpu.CompilerParams(dimension_semantics=sem, vmem_limit_bytes=VMEM_LIMIT)


def _cond_of_tile(t, tm):
    row0 = t * tm
    return jnp.where(row0 < N_P, 0, 1 + (row0 - N_P) // DEC_SEQ)


def _mod_spec(tm):
    return pl.BlockSpec((None, 6, D), lambda t: (_cond_of_tile(t, tm), 0, 0))


def _row_spec(tm, width=D):
    return pl.BlockSpec((tm, width), lambda t: (t, 0))


def _const_spec(shape):
    return pl.BlockSpec(shape, lambda t: (0,) * len(shape))


def _layer_norm(r, g, b):
    mu = jnp.mean(r, axis=-1, keepdims=True)
    c = r - mu
    var = jnp.mean(c * c, axis=-1, keepdims=True)
    return c * lax.rsqrt(var + LN_EPS) * g + b


def _modulate(x, mod_ref, shift_row, scale_row):
    return x * (1.0 + mod_ref[scale_row:scale_row + 1, :]) + mod_ref[shift_row:shift_row + 1, :]


def _ada_kernel(c_ref, w_ref, b_ref, o_ref):
    c = c_ref[...]
    o_ref[...] = jnp.dot(jax.nn.silu(c), w_ref[...], preferred_element_type=F32) + b_ref[...]


def _ada(cond8, w_ada, b_ada):
    tn = 1536
    return pl.pallas_call(
        _ada_kernel,
        grid=(DEPTH, 6 * D // tn),
        in_specs=[pl.BlockSpec((N_COND, D), lambda i, n: (0, 0)),
                  pl.BlockSpec((None, D, tn), lambda i, n: (i, 0, n)),
                  pl.BlockSpec((None, 1, tn), lambda i, n: (i, 0, n))],
        out_specs=pl.BlockSpec((None, N_COND, tn), lambda i, n: (i, 0, n)),
        out_shape=jax.ShapeDtypeStruct((DEPTH, N_COND, 6 * D), F32),
        compiler_params=_cparams(("arbitrary", "arbitrary")),
        name="ada",
    )(cond8, w_ada, b_ada.reshape(DEPTH, 1, 6 * D))


def _prep_kernel(xp_ref, xs_ref, pos_ref, xo_ref):
    is_latent = pl.program_id(0) * TM >= N_P
    xo_ref[...] = jnp.where(is_latent, xs_ref[...] + pos_ref[...], xp_ref[...])


def _prep(x_prompt, x_sample, pos):
    p_tiles = N_P // TM

    def latent_tile(t):
        return jnp.maximum(t - p_tiles, 0)

    return pl.pallas_call(
        _prep_kernel,
        grid=(N_ROWS // TM,),
        in_specs=[pl.BlockSpec((TM, D), lambda t: (jnp.minimum(t, p_tiles - 1), 0)),
                  pl.BlockSpec((TM, D), lambda t: (latent_tile(t), 0)),
                  pl.BlockSpec((TM, D), lambda t: (latent_tile(t) % (DEC_SEQ // TM), 0))],
        out_specs=_row_spec(TM),
        out_shape=jax.ShapeDtypeStruct((N_ROWS, D), F32),
        compiler_params=_cparams(("parallel",)),
        name="prep",
    )(x_prompt.reshape(N_P, D), x_sample.reshape(N_S, D), pos)


S5_BLK = 4
S5_W = 2 * S5_T * S5_CG
S5_GR = 2 * S5_CG
_NT = (((1,), (1,)), ((), ()))


def _granule_transpose4(a, lane):
    low = lane < 64
    a0 = jnp.where(low, a[0], pltpu.roll(a[2], 64, 1))
    a2 = jnp.where(low, pltpu.roll(a[0], 64, 1), a[2])
    a1 = jnp.where(low, a[1], pltpu.roll(a[3], 64, 1))
    a3 = jnp.where(low, pltpu.roll(a[1], 64, 1), a[3])
    even = (lane // S5_GR) % 2 == 0
    return [jnp.where(even, a0, pltpu.roll(a1, 32, 1)), jnp.where(even, pltpu.roll(a0, 96, 1), a1),
            jnp.where(even, a2, pltpu.roll(a3, 32, 1)), jnp.where(even, pltpu.roll(a2, 96, 1), a3)]


def _s5_chunk_matrices(p4, bb_ref, c_ref, pw_ref, min_scr, mi_scr, p_scr):
    half = 2 * S5_P
    gran = lax.broadcasted_iota(jnp.int32, (S5_W, 128), 1) // S5_GR
    rep = (lax.broadcasted_iota(jnp.int32, (S5_GR, S5_W), 1) % S5_GR
           == lax.broadcasted_iota(jnp.int32, (S5_GR, S5_W), 0)).astype(BF16)
    mout = []
    for k in range(2):
        bb_re, bb_im = bb_ref[p4, k, 0], bb_ref[p4, k, 1]
        c_re, c_im = c_ref[p4, k, 0], c_ref[p4, k, 1]
        out_blocks = []
        for r in range(S5_T):
            t_in = S5_T - 1 - r if k == 0 else r
            t_out = r + 1 if k == 0 else S5_T - r
            rows = slice(S5_GR * r, S5_GR * (r + 1))
            pr, pi = pw_ref[p4, k, 0, t_in:t_in + 1, :], pw_ref[p4, k, 1, t_in:t_in + 1, :]
            min_scr[k, rows, 0:half] = bb_re * pr - bb_im * pi
            min_scr[k, rows, half:2 * half] = bb_re * pi + bb_im * pr
            pr, pi = pw_ref[p4, k, 0, t_out:t_out + 1, :], pw_ref[p4, k, 1, t_out:t_out + 1, :]
            out_blocks.append(jnp.concatenate([c_re * pr - c_im * pi, -(c_re * pi + c_im * pr)], axis=-1))
        mout.append(jnp.concatenate(out_blocks, axis=0).astype(BF16))
        c2 = jnp.concatenate([c_re, -c_im], axis=-1)
        c2_hi = c2.astype(BF16)
        c2_lo = (c2 - c2_hi.astype(F32)).astype(BF16)
        m = min_scr[k]
        m_hi = m.astype(BF16)
        m_lo = (m - m_hi.astype(F32)).astype(BF16)
        z = (lax.dot_general(m_hi, c2_hi, _NT, preferred_element_type=F32)
             + lax.dot_general(m_lo, c2_hi, _NT, preferred_element_type=F32)
             + lax.dot_general(m_hi, c2_lo, _NT, preferred_element_type=F32))
        zrep = jnp.dot(z.astype(BF16), rep, preferred_element_type=F32)
        zeros = jnp.zeros((S5_W, S5_W), F32)
        p_scr[0:S5_W, :] = zrep if k == 0 else zeros
        p_scr[S5_W:2 * S5_W, :] = zeros if k == 0 else zrep
        for j in range(S5_W // 128):
            cols = slice(128 * j, 128 * (j + 1))
            acc = None
            for s4 in range(4):
                s = 4 * j + s4
                start = S5_GR * (S5_T - 1 - s) if k == 0 else S5_W - S5_GR * s
                win = p_scr[start:start + S5_W, cols]
                acc = win if acc is None else jnp.where(gran == s4, win, acc)
            mi_scr[k, :, cols] = acc.astype(BF16)
    return mout


def _s5_scan(k, s_scrs, h_scrs, row0, nb, nchunks, hre, him, ar, ai):
    for n in (range(nchunks - 1, -1, -1) if k == 1 else range(nchunks)):
        rows = pl.ds(row0 + n, nb, stride=nchunks)
        h_scrs[0][rows, :] = hre
        h_scrs[1][rows, :] = him
        sre = s_scrs[0][rows, :]
        sim = s_scrs[1][rows, :]
        hre, him = ar * hre - ai * him + sre, ar * him + ai * hre + sim
    return hre, him


def _s5_kernel(x_ref, mod_ref, bb_ref, c_ref, pw_ref, h0_ref, y_ref, fin_ref,
               u_scr, yp_scr, sre_scr, sim_scr, hre_scr, him_scr, min_scr, mi_scr, p_scr):
    half = 2 * S5_P
    lane = lax.broadcasted_iota(jnp.int32, (S5_ROWS, 128), 1)
    scale = jnp.concatenate(
        [jnp.broadcast_to(1.0 + mod_ref[0, 1:2, :], (S5_ROWS_P, 128))]
        + [jnp.broadcast_to(1.0 + mod_ref[1 + b, 1:2, :], (S_CHUNKS, 128)) for b in range(DEC_BATCH)], axis=0)
    shift = jnp.concatenate(
        [jnp.broadcast_to(mod_ref[0, 0:1, :], (S5_ROWS_P, 128))]
        + [jnp.broadcast_to(mod_ref[1 + b, 0:1, :], (S_CHUNKS, 128)) for b in range(DEC_BATCH)], axis=0)
    for j in range(S5_T // 4):
        a = [x_ref[pl.ds(4 * j + s4, S5_ROWS, stride=S5_T), :] * scale + shift for s4 in range(4)]
        b = _granule_transpose4(a, lane)
        for p4 in range(S5_BLK):
            u_scr[p4, :, 128 * j:128 * (j + 1)] = b[p4].astype(BF16)

    for p4 in range(S5_BLK):
        mout = _s5_chunk_matrices(p4, bb_ref, c_ref, pw_ref, min_scr, mi_scr, p_scr)
        u = u_scr[p4]
        u0, u1 = u[:, 0:256], u[:, 256:512]
        y = None
        for k in range(2):
            s = jnp.dot(u, min_scr[k].astype(BF16), preferred_element_type=F32)
            sre_scr[...] = s[:, 0:half]
            sim_scr[...] = s[:, half:2 * half]
            s_scrs, h_scrs = (sre_scr, sim_scr), (hre_scr, him_scr)
            ar_p = jnp.broadcast_to(pw_ref[p4, k, 0, S5_T:S5_T + 1, :], (BATCH, half))
            ai_p = jnp.broadcast_to(pw_ref[p4, k, 1, S5_T:S5_T + 1, :], (BATCH, half))
            zero = jnp.zeros((BATCH, half), F32)
            fre, fim = _s5_scan(k, s_scrs, h_scrs, 0, BATCH, P_CHUNKS, zero, zero, ar_p, ai_p)
            fin_ref[p4, k, :, 0:half] = fre
            fin_ref[p4, k, :, half:2 * half] = fim
            _s5_scan(k, s_scrs, h_scrs, S5_ROWS_P, DEC_BATCH, S_CHUNKS,
                     h0_ref[p4, k, :, 0:half], h0_ref[p4, k, :, half:2 * half], ar_p[:DEC_BATCH], ai_p[:DEC_BATCH])
            hin = jnp.concatenate([hre_scr[...], him_scr[...]], axis=-1).astype(BF16)
            readout = lax.dot_general(hin, mout[k], _NT, preferred_element_type=F32)
            m00 = jnp.dot(u0, mi_scr[k, 0:256, 0:256], preferred_element_type=F32)
            m11 = jnp.dot(u1, mi_scr[k, 256:512, 256:512], preferred_element_type=F32)
            if k == 0:
                m11 = m11 + jnp.dot(u0, mi_scr[k, 0:256, 256:512], preferred_element_type=F32)
            else:
                m00 = m00 + jnp.dot(u1, mi_scr[k, 256:512, 0:256], preferred_element_type=F32)
            yk = jnp.concatenate([m00, m11], axis=-1) + readout
            y = yk if y is None else y + yk
        yp_scr[p4] = y

    for j in range(S5_T // 4):
        b = [yp_scr[p4, :, 128 * j:128 * (j + 1)] for p4 in range(S5_BLK)]
        a = _granule_transpose4(b, lane)
        for s4 in range(4):
            y_ref[pl.ds(4 * j + s4, S5_ROWS, stride=S5_T), :] = a[s4]


def _s5(x, mods, params, h0):
    bb, cc, pw = params
    nblk = S5_Q // S5_BLK
    return pl.pallas_call(
        _s5_kernel,
        grid=(nblk,),
        in_specs=[pl.BlockSpec((N_ROWS, 128), lambda q: (0, q)),
                  pl.BlockSpec((N_COND, 6, 128), lambda q: (0, 0, q)),
                  pl.BlockSpec((S5_BLK, 2, 2, S5_GR, 128), lambda q: (q, 0, 0, 0, 0)),
                  pl.BlockSpec((S5_BLK, 2, 2, S5_GR, 128), lambda q: (q, 0, 0, 0, 0)),
                  pl.BlockSpec((S5_BLK, 2, 2, S5_T + 1, 128), lambda q: (q, 0, 0, 0, 0)),
                  pl.BlockSpec((S5_BLK, 2, DEC_BATCH, 256), lambda q: (q, 0, 0, 0))],
        out_specs=[pl.BlockSpec((N_ROWS, 128), lambda q: (0, q)),
                   pl.BlockSpec((S5_BLK, 2, BATCH, 256), lambda q: (q, 0, 0, 0))],
        out_shape=[jax.ShapeDtypeStruct((N_ROWS, D), F32),
                   jax.ShapeDtypeStruct((S5_Q, 2, BATCH, 256), F32)],
        scratch_shapes=[pltpu.VMEM((S5_BLK, S5_ROWS, S5_W), BF16), pltpu.VMEM((S5_BLK, S5_ROWS, S5_W), F32),
                        pltpu.VMEM((S5_ROWS, 128), F32), pltpu.VMEM((S5_ROWS, 128), F32),
                        pltpu.VMEM((S5_ROWS, 128), F32), pltpu.VMEM((S5_ROWS, 128), F32),
                        pltpu.VMEM((2, S5_W, 256), F32),
                        pltpu.VMEM((2, S5_W, S5_W), BF16), pltpu.VMEM((2 * S5_W, S5_W), F32)],
        compiler_params=_cparams(("parallel",)),
        name="s5",
    )(x, mods, bb, cc, pw, h0)


def _s5_params(a_re, a_im, log_step, b_re, b_im, c_re, c_im):
    lam_re = jnp.minimum(a_re.astype(F32), -1e-4)
    lam_im = a_im.astype(F32)
    dt = jnp.exp(log_step.astype(F32))[..., None]
    j = jnp.arange(S5_T + 1, dtype=F32)[:, None, None, None]
    mag = jnp.exp(lam_re * dt * j)
    pw_re = mag * jnp.cos(lam_im * dt * j)
    pw_im = mag * jnp.sin(lam_im * dt * j)
    nr, ni = pw_re[1] - 1.0, pw_im[1]
    den = lam_re * lam_re + lam_im * lam_im
    q_re = (nr * lam_re + ni * lam_im) / den
    q_im = (ni * lam_re - nr * lam_im) / den
    b_re, b_im = b_re.astype(F32), b_im.astype(F32)
    bb_re = q_re[..., None] * b_re - q_im[..., None] * b_im
    bb_im = q_re[..., None] * b_im + q_im[..., None] * b_re
    eye = jnp.eye(2, dtype=F32)

    def pair_rows(m, spec):
        m = m.reshape(2, S5_Q, 2, *m.shape[2:])
        return jnp.einsum(spec, m, eye).reshape(S5_Q, 2, S5_GR, 2 * S5_P)

    bb = jnp.stack([pair_rows(bb_re, "kqgpc,gh->qkgchp"), pair_rows(bb_im, "kqgpc,gh->qkgchp")], axis=2)
    cc = jnp.stack([pair_rows(c_re.astype(F32), "kqgcp,gh->qkgchp"),
                    pair_rows(c_im.astype(F32), "kqgcp,gh->qkgchp")], axis=2)

    def pw_lay(p):
        return p.reshape(S5_T + 1, 2, S5_Q, 2 * S5_P).transpose(2, 1, 0, 3)
    pw = jnp.stack([pw_lay(pw_re), pw_lay(pw_im)], axis=2)
    return bb, cc, pw


def _s5_state_in(st_re, st_im):
    def lay(s):
        return s.astype(F32).reshape(DEC_BATCH, 2, S5_Q, 2 * S5_P).transpose(2, 1, 0, 3)
    return jnp.concatenate([lay(st_re), lay(st_im)], axis=-1)


def _s5_state_out(fin):
    def lay(s):
        return s.transpose(2, 1, 0, 3).reshape(BATCH, 2, S5_G, S5_P)
    return lay(fin[..., :2 * S5_P]), lay(fin[..., 2 * S5_P:])


def _glu_kernel(ys_ref, x_ref, mod_ref, d_ref, w_ref, g_ref, b_ref, x1_ref, h2_ref):
    for r in range(TM // ROW_SUB):
        rows = slice(r * ROW_SUB, (r + 1) * ROW_SUB)
        x = x_ref[rows, :]
        h = _modulate(x, mod_ref, 0, 1)
        y = jax.nn.gelu(ys_ref[rows, :] + d_ref[...] * h).astype(BF16)
        z = jnp.dot(y, w_ref[...], preferred_element_type=F32)
        f = z[:, :D] * jax.nn.sigmoid(z[:, D:])
        x1 = _layer_norm(ALPHA * x + (1.0 + mod_ref[2:3, :]) * f, g_ref[...], b_ref[...])
        x1_ref[rows, :] = x1
        h2_ref[rows, :] = _modulate(x1, mod_ref, 3, 4).astype(BF16)


def _glu(ysum, x, mod, d, w_glu, ln_g, ln_b):
    return pl.pallas_call(
        _glu_kernel,
        grid=(N_ROWS // TM,),
        in_specs=[_row_spec(TM), _row_spec(TM), _mod_spec(TM), _const_spec((1, D)),
                  _const_spec((D, 2 * D)), _const_spec((1, D)), _const_spec((1, D))],
        out_specs=[_row_spec(TM), _row_spec(TM)],
        out_shape=[jax.ShapeDtypeStruct((N_ROWS, D), F32), jax.ShapeDtypeStruct((N_ROWS, D), BF16)],
        compiler_params=_cparams(("parallel",)),
        name="glu",
    )(ysum, x, mod, d.reshape(1, D), w_glu, ln_g.reshape(1, D), ln_b.reshape(1, D))


def _finish(x1, g, mod_ref, g_ref, b_ref, x2_ref):
    x2 = _layer_norm(ALPHA * x1 + (1.0 + mod_ref[5:6, :]) * g, g_ref[...], b_ref[...])
    x2_ref[...] = x2
    return x2


FFN_TF = FFN_DIM // 2


def _ffn_kernel(h_ref, x1_ref, mod_ref, nmod_ref, wg_ref, wu_ref, wd_ref, g_ref, b_ref, x2_ref, hn_ref):
    h = h_ref[...]
    acc = None
    for c in range(FFN_DIM // FFN_TF):
        sl = slice(c * FFN_TF, (c + 1) * FFN_TF)
        a = jnp.dot(h, wg_ref[:, sl], preferred_element_type=F32)
        b = jnp.dot(h, wu_ref[:, sl], preferred_element_type=F32)
        p = jnp.dot((jax.nn.silu(a) * b).astype(BF16), wd_ref[sl, :], preferred_element_type=F32)
        acc = p if acc is None else acc + p
    x2 = _finish(x1_ref[...], acc, mod_ref, g_ref, b_ref, x2_ref)
    hn_ref[...] = _modulate(x2, nmod_ref, 0, 1).astype(BF16)


def _ffn(h2, x1, mod, nmod, wg, wu, wd, ln_g, ln_b):
    single = pl.Buffered(1)
    return pl.pallas_call(
        _ffn_kernel,
        grid=(N_ROWS // TM,),
        in_specs=[_row_spec(TM), _row_spec(TM), _mod_spec(TM), _mod_spec(TM),
                  pl.BlockSpec((D, FFN_DIM), lambda t: (0, 0), pipeline_mode=single),
                  pl.BlockSpec((D, FFN_DIM), lambda t: (0, 0), pipeline_mode=single),
                  pl.BlockSpec((FFN_DIM, D), lambda t: (0, 0), pipeline_mode=single),
                  _const_spec((1, D)), _const_spec((1, D))],
        out_specs=[_row_spec(TM), _row_spec(TM)],
        out_shape=[jax.ShapeDtypeStruct((N_ROWS, D), F32), jax.ShapeDtypeStruct((N_ROWS, D), BF16)],
        compiler_params=_cparams(("parallel",)),
        name="ffn",
    )(h2, x1, mod, nmod, wg, wu, wd, ln_g.reshape(1, D), ln_b.reshape(1, D))


def _gmlp_kernel(h_ref, x_ref, mod_ref, win_ref, bin_ref, vg_ref, vb_ref, ws_ref, bs_ref, wout_ref,
                 g_ref, b_ref, wr_ref, br_ref, x1_ref, h2_ref, idx_ref, p_ref, s_scr):
    for r in range(TM // ROW_SUB):
        rows = slice(r * ROW_SUB, (r + 1) * ROW_SUB)
        z = jax.nn.gelu(jnp.dot(h_ref[rows, :], win_ref[...], preferred_element_type=F32) + bin_ref[...])
        u = z[:, :D]
        v = _layer_norm(z[:, D:], vg_ref[...], vb_ref[...]).astype(BF16)
        n_chunks = ROW_SUB // GM_CHUNK
        for hd in range(GM_HEADS):
            cols = slice(hd * GM_CHUNK, (hd + 1) * GM_CHUNK)
            rhs = jnp.concatenate([v[n * GM_CHUNK:(n + 1) * GM_CHUNK, cols] for n in range(n_chunks)], axis=-1)
            s = jnp.dot(ws_ref[hd], rhs, preferred_element_type=F32)
            for n in range(n_chunks):
                lo = r * ROW_SUB + n * GM_CHUNK
                s_scr[lo:lo + GM_CHUNK, cols] = s[:, n * GM_CHUNK:(n + 1) * GM_CHUNK] + bs_ref[hd]
        f = jnp.dot((u * s_scr[rows, :]).astype(BF16), wout_ref[...], preferred_element_type=F32)
        x1 = _layer_norm(ALPHA * x_ref[rows, :] + (1.0 + mod_ref[2:3, :]) * f, g_ref[...], b_ref[...])
        x1_ref[rows, :] = x1
        h2 = _modulate(x1, mod_ref, 3, 4)
        h2_ref[rows, :] = h2
        h_hi = h2.astype(BF16)
        h_lo = (h2 - h_hi.astype(F32)).astype(BF16)
        w_hi, w_lo = wr_ref[0], wr_ref[1]
        logits = (jnp.dot(h_hi, w_hi, preferred_element_type=F32) + jnp.dot(h_lo, w_hi, preferred_element_type=F32)
                  + jnp.dot(h_hi, w_lo, preferred_element_type=F32)) + br_ref[...]
        lane = lax.broadcasted_iota(jnp.int32, logits.shape, 1)
        neg = jnp.float32(-jnp.inf)
        logits = jnp.where(lane < N_EXPERTS, logits, neg)
        v1 = jnp.max(logits, axis=-1, keepdims=True)
        i1 = jnp.min(jnp.where(logits == v1, lane, 128), axis=-1, keepdims=True)
        rest = jnp.where(lane == i1, neg, logits)
        v2 = jnp.max(rest, axis=-1, keepdims=True)
        i2 = jnp.min(jnp.where(rest == v2, lane, 128), axis=-1, keepdims=True)
        e2 = jnp.exp(v2 - v1)
        p1 = 1.0 / (1.0 + e2)
        p2 = e2 / (1.0 + e2)
        idx_ref[rows, :] = jnp.where(lane == 0, i1, jnp.where(lane == 1, i2, 0))
        p_ref[rows, :] = jnp.where(lane == 0, p1, jnp.where(lane == 1, p2, 0.0))


def _gmlp(h1, x, mod, w_in, b_in, vg, vb, w_s, b_s, w_out, ln_g, ln_b, w_r, b_r):
    return pl.pallas_call(
        _gmlp_kernel,
        grid=(N_ROWS // TM,),
        in_specs=[_row_spec(TM), _row_spec(TM), _mod_spec(TM),
                  _const_spec((D, 2 * D)), _const_spec((1, 2 * D)), _const_spec((1, D)), _const_spec((1, D)),
                  _const_spec((GM_HEADS, GM_CHUNK, GM_CHUNK)), _const_spec((GM_HEADS, GM_CHUNK, GM_CHUNK)),
                  _const_spec((D, D)), _const_spec((1, D)), _const_spec((1, D)),
                  _const_spec((2, D, 128)), _const_spec((1, 128))],
        out_specs=[_row_spec(TM), _row_spec(TM), _row_spec(TM, 128), _row_spec(TM, 128)],
        out_shape=[jax.ShapeDtypeStruct((N_ROWS, D), F32), jax.ShapeDtypeStruct((N_ROWS, D), F32),
                   jax.ShapeDtypeStruct((N_ROWS, 128), jnp.int32), jax.ShapeDtypeStruct((N_ROWS, 128), F32)],
        scratch_shapes=[pltpu.VMEM((TM, D), F32)],
        compiler_params=_cparams(("parallel",)),
        name="gmlp",
    )(h1, x, mod, w_in, b_in.reshape(1, 2 * D), vg.reshape(1, D), vb.reshape(1, D), w_s, b_s, w_out,
      ln_g.reshape(1, D), ln_b.reshape(1, D), w_r, b_r)


def _moe_kernel(te_ref, first_ref, used_ref, x_ref, wg_ref, wu_ref, wd_ref, y_ref,
                acc_ref, xb_ref, act_ref, wgb, wub, wdb):
    t, f = pl.program_id(0), pl.program_id(1)
    nf = EXPERT_DIM // MOE_TF
    live = t < used_ref[0]
    first = first_ref[t] == 1

    def swiglu_chunk(x, c):
        a = jnp.dot(x, wgb[c], preferred_element_type=F32)
        b = jnp.dot(x, wub[c], preferred_element_type=F32)
        return (jax.nn.silu(a) * b).astype(BF16)

    @pl.when(jnp.logical_and(live, first))
    def _():
        rows_f = pl.ds(pl.multiple_of(f * MOE_TF, MOE_TF), MOE_TF)
        wgb[f] = wg_ref[...].astype(BF16)
        wub[f] = wu_ref[...].astype(BF16)
        wdb[rows_f, :] = wd_ref[...].astype(BF16)

        @pl.when(f == 0)
        def _():
            xb_ref[...] = x_ref[...].astype(BF16)

        p = jnp.dot(swiglu_chunk(xb_ref[...], f), wdb[rows_f, :], preferred_element_type=F32)
        acc_ref[...] = jnp.where(f == 0, p, acc_ref[...] + p)

        @pl.when(f == nf - 1)
        def _():
            y_ref[...] = acc_ref[...]

    @pl.when(jnp.logical_and(jnp.logical_and(live, jnp.logical_not(first)), f == 0))
    def _():
        x = x_ref[...].astype(BF16)
        for c in range(nf):
            act_ref[:, c * MOE_TF:(c + 1) * MOE_TF] = swiglu_chunk(x, c)
        y_ref[...] = jnp.dot(act_ref[...], wdb[...], preferred_element_type=F32)

    @pl.when(jnp.logical_and(jnp.logical_not(live), f == nf - 1))
    def _():
        y_ref[...] = jnp.zeros_like(y_ref)


def _moe(layer, tile_expert, tile_first, n_used, xs, wg, wu, wd):
    nf = EXPERT_DIM // MOE_TF

    def chunk(t, f, first, used):
        return jnp.where(jnp.logical_and(t < used[0], first[t] == 1), f, nf - 1)

    grid_spec = pltpu.PrefetchScalarGridSpec(
        num_scalar_prefetch=3,
        grid=(MOE_TILES, nf),
        in_specs=[pl.BlockSpec((MOE_TM, D), lambda t, f, te, first, used: (jnp.minimum(t, used[0] - 1), 0)),
                  pl.BlockSpec((None, None, D, MOE_TF),
                               lambda t, f, te, first, used: (layer, te[t], 0, chunk(t, f, first, used))),
                  pl.BlockSpec((None, None, D, MOE_TF),
                               lambda t, f, te, first, used: (layer, te[t], 0, chunk(t, f, first, used))),
                  pl.BlockSpec((None, None, MOE_TF, D),
                               lambda t, f, te, first, used: (layer, te[t], chunk(t, f, first, used), 0))],
        out_specs=pl.BlockSpec((MOE_TM, D), lambda t, f, te, first, used: (t, 0)),
        scratch_shapes=[pltpu.VMEM((MOE_TM, D), F32), pltpu.VMEM((MOE_TM, D), BF16),
                        pltpu.VMEM((MOE_TM, EXPERT_DIM), BF16),
                        pltpu.VMEM((nf, D, MOE_TF), BF16), pltpu.VMEM((nf, D, MOE_TF), BF16),
                        pltpu.VMEM((EXPERT_DIM, D), BF16)],
    )
    return pl.pallas_call(
        _moe_kernel,
        grid_spec=grid_spec,
        out_shape=jax.ShapeDtypeStruct((MOE_ROWS, D), F32),
        compiler_params=_cparams(("arbitrary", "arbitrary")),
        name="moe",
    )(tile_expert, tile_first, n_used, xs, wg, wu, wd)


def _take_rows(a, rows):
    return a.at[rows].get(mode="promise_in_bounds")


def _route(idx):
    e = idx.reshape(-1)
    onehot = (e[:, None] == jnp.arange(N_EXPERTS, dtype=jnp.int32)[None, :]).astype(jnp.int32)
    counts = jnp.sum(onehot, axis=0)
    rank = jnp.sum((jnp.cumsum(onehot, axis=0) - 1) * onehot, axis=1)
    padded = ((counts + MOE_TM - 1) // MOE_TM) * MOE_TM
    ends = jnp.cumsum(padded)
    starts = ends - padded
    pos = starts[e] + rank
    token = jnp.arange(N_ROWS * TOP_K, dtype=jnp.int32) // TOP_K
    src = jnp.zeros((MOE_ROWS,), jnp.int32).at[pos].set(token)
    n_used = (ends[-1] // MOE_TM).astype(jnp.int32)
    tile_start = jnp.minimum(jnp.arange(MOE_TILES, dtype=jnp.int32), n_used - 1) * MOE_TM
    tile_expert = jnp.sum((tile_start[:, None] >= ends[None, :]).astype(jnp.int32), axis=1)
    tile_expert = jnp.minimum(tile_expert, N_EXPERTS - 1).astype(jnp.int32)
    prev = jnp.concatenate([jnp.full((1,), -1, jnp.int32), tile_expert[:-1]])
    tile_first = (tile_expert != prev).astype(jnp.int32)
    return src, pos.reshape(N_ROWS, TOP_K), tile_expert, tile_first, n_used.reshape(1)


def _combine_kernel(ya_ref, yb_ref, p_ref, x1_ref, mod_ref, g_ref, b_ref, x2_ref):
    g = p_ref[:, 0:1] * ya_ref[...] + p_ref[:, 1:2] * yb_ref[...]
    _finish(x1_ref[...], g, mod_ref, g_ref, b_ref, x2_ref)


def _combine(ya, yb, p, x1, mod, ln_g, ln_b):
    return pl.pallas_call(
        _combine_kernel,
        grid=(N_ROWS // TM,),
        in_specs=[_row_spec(TM), _row_spec(TM), _row_spec(TM, 128), _row_spec(TM), _mod_spec(TM),
                  _const_spec((1, D)), _const_spec((1, D))],
        out_specs=_row_spec(TM),
        out_shape=jax.ShapeDtypeStruct((N_ROWS, D), F32),
        compiler_params=_cparams(("parallel",)),
        name="combine",
    )(ya, yb, p, x1, mod, ln_g.reshape(1, D), ln_b.reshape(1, D))


def _grid_pos_embed():
    rows = DEC_SEQ // GRID_W
    r, col = jnp.meshgrid(jnp.arange(rows, dtype=F32), jnp.arange(GRID_W, dtype=F32), indexing="ij")
    quarter = D // 4
    freq = 1.0 / (10000.0 ** (jnp.arange(quarter, dtype=F32) / quarter))

    def emb(p):
        ang = p.reshape(-1)[:, None] * freq[None, :]
        return jnp.concatenate([jnp.sin(ang), jnp.cos(ang)], axis=-1)

    return jnp.concatenate([emb(r), emb(col)], axis=-1)


def kernel(x_prompt, x_sample, c, state_ssm_re, state_ssm_im, c_ctx, w_ada, b_ada, ln_g, ln_b, s5_a_re, s5_a_im, s5_log_step, s5_b_re, s5_b_im, s5_c_re, s5_c_im, s5_d, s5_w_glu, gm_w_in, gm_b_in, gm_ln_g, gm_ln_b, gm_w_s, gm_b_s, gm_w_out, ffn_w_gate, ffn_w_up, ffn_w_down, moe_w_router, moe_b_router, moe_w_gate, moe_w_up, moe_w_down):
    cond8 = jnp.concatenate([c_ctx[None, :], c, jnp.zeros((N_COND - 1 - DEC_BATCH, D), F32)], axis=0)
    mods = _ada(cond8, w_ada, b_ada).reshape(DEPTH, N_COND, 6, D)

    x = _prep(x_prompt, x_sample, _grid_pos_embed())
    h = None

    fin_re, fin_im = [], []
    for i in range(DEPTH):
        j = i // 2
        mod = mods[i]
        nmod = mods[min(i + 1, DEPTH - 1)]
        if i % 2 == 0:
            params = _s5_params(s5_a_re[j], s5_a_im[j], s5_log_step[j], s5_b_re[j], s5_b_im[j], s5_c_re[j], s5_c_im[j])
            ysum, fin = _s5(x, mod, params, _s5_state_in(state_ssm_re[:, j], state_ssm_im[:, j]))
            fr, fi = _s5_state_out(fin)
            fin_re.append(fr)
            fin_im.append(fi)
            x1, h2 = _glu(ysum, x, mod, s5_d[j], s5_w_glu[j].astype(BF16), ln_g[i, 0], ln_b[i, 0])
            x, h = _ffn(h2, x1, mod, nmod, ffn_w_gate[j].astype(BF16), ffn_w_up[j].astype(BF16),
                        ffn_w_down[j].astype(BF16), ln_g[i, 1], ln_b[i, 1])
        else:
            b_s = jnp.broadcast_to(gm_b_s[j][:, :, None], (GM_HEADS, GM_CHUNK, GM_CHUNK)).astype(F32)
            w_r = jnp.pad(moe_w_router[j].astype(F32), ((0, 0), (0, 128 - N_EXPERTS)))
            w_r_hi = w_r.astype(BF16)
            w_r = jnp.stack([w_r_hi, (w_r - w_r_hi.astype(F32)).astype(BF16)], axis=0)
            b_r = jnp.pad(moe_b_router[j].astype(F32), (0, 128 - N_EXPERTS)).reshape(1, 128)
            x1, h2, idx, p = _gmlp(h, x, mod, gm_w_in[j].astype(BF16), gm_b_in[j], gm_ln_g[j], gm_ln_b[j],
                                   gm_w_s[j].astype(BF16), b_s, gm_w_out[j].astype(BF16), ln_g[i, 0], ln_b[i, 0], w_r, b_r)
            src, pos, tile_expert, tile_first, n_used = _route(idx[:, :TOP_K])
            ys = _moe(j, tile_expert, tile_first, n_used, _take_rows(h2, src), moe_w_gate, moe_w_up, moe_w_down)
            x = _combine(_take_rows(ys, pos[:, 0]), _take_rows(ys, pos[:, 1]), p, x1, mod, ln_g[i, 1], ln_b[i, 1])

    y_prompt = x[:N_P].reshape(BATCH, SEQ, D)
    y_sample = x[N_P:].reshape(DEC_BATCH, DEC_SEQ, D)
    new_re = jnp.stack(fin_re, axis=1).astype(x_prompt.dtype)
    new_im = jnp.stack(fin_im, axis=1).astype(x_prompt.dtype)
    return (y_prompt, y_sample, new_re, new_im)
```

```python
import jax
import jax.numpy as jnp
from jax import lax
from jax.experimental import pallas as pl
from jax.experimental.pallas import tpu as pltpu

F32 = jnp.float32
BF16 = jnp.bfloat16

D = 1024
BATCH, SEQ = 32, 256
DEC_BATCH, DEC_SEQ = 4, 1024
DEPTH = 4
GRID_W = 64
N_P = BATCH * SEQ
N_S = DEC_BATCH * DEC_SEQ
N_ROWS = N_P + N_S
N_COND = 8

S5_CG, S5_P, S5_G = 16, 64, 64
S5_T = 16
S5_Q = S5_G // 2
P_CHUNKS = SEQ // S5_T
S_CHUNKS = DEC_SEQ // S5_T
S5_ROWS_P = P_CHUNKS * BATCH
S5_ROWS_S = S_CHUNKS * DEC_BATCH
S5_ROWS = S5_ROWS_P + S5_ROWS_S

GM_CHUNK, GM_HEADS = 128, 8
FFN_DIM = 2816
N_EXPERTS, TOP_K, EXPERT_DIM = 8, 2, 3584
ALPHA = (2.0 * DEPTH) ** 0.25
LN_EPS = 1e-5

TM = 512
ROW_SUB = TM
MOE_TM = 512
MOE_TF = 512
MOE_TILES = (N_ROWS * TOP_K) // MOE_TM + N_EXPERTS
MOE_ROWS = MOE_TILES * MOE_TM
MOE_UNITS = N_EXPERTS * (EXPERT_DIM // MOE_TF) + MOE_TILES
MOE_STREAM, MOE_CACHED, MOE_ZERO, MOE_IDLE = 1, 2, 0, 3
VMEM_LIMIT = 56 * 1024 * 1024


def _cparams(sem):
    return pltpu.CompilerParams(dimension_semantics=sem, vmem_limit_bytes=VMEM_LIMIT)


def _cond_of_tile(t, tm):
    row0 = t * tm
    return jnp.where(row0 < N_P, 0, 1 + (row0 - N_P) // DEC_SEQ)


def _mod_spec(tm):
    return pl.BlockSpec((None, 6, D), lambda t: (_cond_of_tile(t, tm), 0, 0))


def _row_spec(tm, width=D):
    return pl.BlockSpec((tm, width), lambda t: (t, 0))


def _const_spec(shape):
    return pl.BlockSpec(shape, lambda t: (0,) * len(shape))


def _layer_norm(r, g, b):
    mu = jnp.mean(r, axis=-1, keepdims=True)
    c = r - mu
    var = jnp.mean(c * c, axis=-1, keepdims=True)
    return c * lax.rsqrt(var + LN_EPS) * g + b


def _modulate(x, mod_ref, shift_row, scale_row):
    return x * (1.0 + mod_ref[scale_row:scale_row + 1, :]) + mod_ref[shift_row:shift_row + 1, :]


def _ada_kernel(c_ref, w_ref, b_ref, o_ref):
    c = c_ref[...]
    o_ref[...] = jnp.dot(jax.nn.silu(c), w_ref[...], preferred_element_type=F32) + b_ref[...]


def _ada(cond8, w_ada, b_ada):
    tn = 1536
    return pl.pallas_call(
        _ada_kernel,
        grid=(DEPTH, 6 * D // tn),
        in_specs=[pl.BlockSpec((N_COND, D), lambda i, n: (0, 0)),
                  pl.BlockSpec((None, D, tn), lambda i, n: (i, 0, n)),
                  pl.BlockSpec((None, 1, tn), lambda i, n: (i, 0, n))],
        out_specs=pl.BlockSpec((None, N_COND, tn), lambda i, n: (i, 0, n)),
        out_shape=jax.ShapeDtypeStruct((DEPTH, N_COND, 6 * D), F32),
        compiler_params=_cparams(("arbitrary", "arbitrary")),
        name="ada",
    )(cond8, w_ada, b_ada.reshape(DEPTH, 1, 6 * D))


def _prep_kernel(xp_ref, xs_ref, pos_ref, xo_ref):
    is_latent = pl.program_id(0) * TM >= N_P
    xo_ref[...] = jnp.where(is_latent, xs_ref[...] + pos_ref[...], xp_ref[...])


def _prep(x_prompt, x_sample, pos):
    p_tiles = N_P // TM

    def latent_tile(t):
        return jnp.maximum(t - p_tiles, 0)

    return pl.pallas_call(
        _prep_kernel,
        grid=(N_ROWS // TM,),
        in_specs=[pl.BlockSpec((TM, D), lambda t: (jnp.minimum(t, p_tiles - 1), 0)),
                  pl.BlockSpec((TM, D), lambda t: (latent_tile(t), 0)),
                  pl.BlockSpec((TM, D), lambda t: (latent_tile(t) % (DEC_SEQ // TM), 0))],
        out_specs=_row_spec(TM),
        out_shape=jax.ShapeDtypeStruct((N_ROWS, D), F32),
        compiler_params=_cparams(("parallel",)),
        name="prep",
    )(x_prompt.reshape(N_P, D), x_sample.reshape(N_S, D), pos)


S5_BLK = 4
S5_W = 2 * S5_T * S5_CG
S5_GR = 2 * S5_CG
_NT = (((1,), (1,)), ((), ()))


def _granule_transpose4(a, lane):
    low = lane < 64
    a0 = jnp.where(low, a[0], pltpu.roll(a[2], 64, 1))
    a2 = jnp.where(low, pltpu.roll(a[0], 64, 1), a[2])
    a1 = jnp.where(low, a[1], pltpu.roll(a[3], 64, 1))
    a3 = jnp.where(low, pltpu.roll(a[1], 64, 1), a[3])
    even = (lane // S5_GR) % 2 == 0
    return [jnp.where(even, a0, pltpu.roll(a1, 32, 1)), jnp.where(even, pltpu.roll(a0, 96, 1), a1),
            jnp.where(even, a2, pltpu.roll(a3, 32, 1)), jnp.where(even, pltpu.roll(a2, 96, 1), a3)]


def _s5_chunk_matrices(p4, bb_ref, c_ref, pw_ref, min_scr, mi_scr, p_scr):
    half = 2 * S5_P
    gran = lax.broadcasted_iota(jnp.int32, (S5_W, 128), 1) // S5_GR
    rep = (lax.broadcasted_iota(jnp.int32, (S5_GR, S5_W), 1) % S5_GR
           == lax.broadcasted_iota(jnp.int32, (S5_GR, S5_W), 0)).astype(BF16)
    mout = []
    for k in range(2):
        bb_re, bb_im = bb_ref[p4, k, 0], bb_ref[p4, k, 1]
        c_re, c_im = c_ref[p4, k, 0], c_ref[p4, k, 1]
        out_blocks = []
        for r in range(S5_T):
            t_in = S5_T - 1 - r if k == 0 else r
            t_out = r + 1 if k == 0 else S5_T - r
            rows = slice(S5_GR * r, S5_GR * (r + 1))
            pr, pi = pw_ref[p4, k, 0, t_in:t_in + 1, :], pw_ref[p4, k, 1, t_in:t_in + 1, :]
            min_scr[k, rows, 0:half] = bb_re * pr - bb_im * pi
            min_scr[k, rows, half:2 * half] = bb_re * pi + bb_im * pr
            pr, pi = pw_ref[p4, k, 0, t_out:t_out + 1, :], pw_ref[p4, k, 1, t_out:t_out + 1, :]
            out_blocks.append(jnp.concatenate([c_re * pr - c_im * pi, -(c_re * pi + c_im * pr)], axis=-1))
        mout.append(jnp.concatenate(out_blocks, axis=0).astype(BF16))
        c2 = jnp.concatenate([c_re, -c_im], axis=-1)
        c2_hi = c2.astype(BF16)
        c2_lo = (c2 - c2_hi.astype(F32)).astype(BF16)
        m = min_scr[k]
        m_hi = m.astype(BF16)
        m_lo = (m - m_hi.astype(F32)).astype(BF16)
        z = (lax.dot_general(m_hi, c2_hi, _NT, preferred_element_type=F32)
             + lax.dot_general(m_lo, c2_hi, _NT, preferred_element_type=F32)
             + lax.dot_general(m_hi, c2_lo, _NT, preferred_element_type=F32))
        zrep = jnp.dot(z.astype(BF16), rep, preferred_element_type=F32)
        zeros = jnp.zeros((S5_W, S5_W), F32)
        p_scr[0:S5_W, :] = zrep if k == 0 else zeros
        p_scr[S5_W:2 * S5_W, :] = zeros if k == 0 else zrep
        for j in range(S5_W // 128):
            cols = slice(128 * j, 128 * (j + 1))
            acc = None
            for s4 in range(4):
                s = 4 * j + s4
                start = S5_GR * (S5_T - 1 - s) if k == 0 else S5_W - S5_GR * s
                win = p_scr[start:start + S5_W, cols]
                acc = win if acc is None else jnp.where(gran == s4, win, acc)
            mi_scr[k, :, cols] = acc.astype(BF16)
    return mout


def _s5_scan(k, s_scrs, h_scrs, row0, nb, nchunks, hre, him, ar, ai):
    for n in (range(nchunks - 1, -1, -1) if k == 1 else range(nchunks)):
        rows = pl.ds(row0 + n, nb, stride=nchunks)
        h_scrs[0][rows, :] = hre
        h_scrs[1][rows, :] = him
        sre = s_scrs[0][rows, :]
        sim = s_scrs[1][rows, :]
        hre, him = ar * hre - ai * him + sre, ar * him + ai * hre + sim
    return hre, him


def _s5_kernel(x_ref, mod_ref, bb_ref, c_ref, pw_ref, h0_ref, y_ref, fin_ref,
               u_scr, yp_scr, sre_scr, sim_scr, hre_scr, him_scr, min_scr, mi_scr, p_scr):
    half = 2 * S5_P
    lane = lax.broadcasted_iota(jnp.int32, (S5_ROWS, 128), 1)
    scale = jnp.concatenate(
        [jnp.broadcast_to(1.0 + mod_ref[0, 1:2, :], (S5_ROWS_P, 128))]
        + [jnp.broadcast_to(1.0 + mod_ref[1 + b, 1:2, :], (S_CHUNKS, 128)) for b in range(DEC_BATCH)], axis=0)
    shift = jnp.concatenate(
        [jnp.broadcast_to(mod_ref[0, 0:1, :], (S5_ROWS_P, 128))]
        + [jnp.broadcast_to(mod_ref[1 + b, 0:1, :], (S_CHUNKS, 128)) for b in range(DEC_BATCH)], axis=0)
    for j in range(S5_T // 4):
        a = [x_ref[pl.ds(4 * j + s4, S5_ROWS, stride=S5_T), :] * scale + shift for s4 in range(4)]
        b = _granule_transpose4(a, lane)
        for p4 in range(S5_BLK):
            u_scr[p4, :, 128 * j:128 * (j + 1)] = b[p4].astype(BF16)

    for p4 in range(S5_BLK):
        mout = _s5_chunk_matrices(p4, bb_ref, c_ref, pw_ref, min_scr, mi_scr, p_scr)
        u = u_scr[p4]
        u0, u1 = u[:, 0:256], u[:, 256:512]
        y = None
        for k in range(2):
            s = jnp.dot(u, min_scr[k].astype(BF16), preferred_element_type=F32)
            sre_scr[...] = s[:, 0:half]
            sim_scr[...] = s[:, half:2 * half]
            s_scrs, h_scrs = (sre_scr, sim_scr), (hre_scr, him_scr)
            ar_p = jnp.broadcast_to(pw_ref[p4, k, 0, S5_T:S5_T + 1, :], (BATCH, half))
            ai_p = jnp.broadcast_to(pw_ref[p4, k, 1, S5_T:S5_T + 1, :], (BATCH, half))
            zero = jnp.zeros((BATCH, half), F32)
            fre, fim = _s5_scan(k, s_scrs, h_scrs, 0, BATCH, P_CHUNKS, zero, zero, ar_p, ai_p)
            fin_ref[p4, k, :, 0:half] = fre
            fin_ref[p4, k, :, half:2 * half] = fim
            _s5_scan(k, s_scrs, h_scrs, S5_ROWS_P, DEC_BATCH, S_CHUNKS,
                     h0_ref[p4, k, :, 0:half], h0_ref[p4, k, :, half:2 * half], ar_p[:DEC_BATCH], ai_p[:DEC_BATCH])
            hin = jnp.concatenate([hre_scr[...], him_scr[...]], axis=-1).astype(BF16)
            readout = lax.dot_general(hin, mout[k], _NT, preferred_element_type=F32)
            m00 = jnp.dot(u0, mi_scr[k, 0:256, 0:256], preferred_element_type=F32)
            m11 = jnp.dot(u1, mi_scr[k, 256:512, 256:512], preferred_element_type=F32)
            if k == 0:
                m11 = m11 + jnp.dot(u0, mi_scr[k, 0:256, 256:512], preferred_element_type=F32)
            else:
                m00 = m00 + jnp.dot(u1, mi_scr[k, 256:512, 0:256], preferred_element_type=F32)
            yk = jnp.concatenate([m00, m11], axis=-1) + readout
            y = yk if y is None else y + yk
        yp_scr[p4] = y

    for j in range(S5_T // 4):
        b = [yp_scr[p4, :, 128 * j:128 * (j + 1)] for p4 in range(S5_BLK)]
        a = _granule_transpose4(b, lane)
        for s4 in range(4):
            y_ref[pl.ds(4 * j + s4, S5_ROWS, stride=S5_T), :] = a[s4]


def _s5(x, mods, params, h0):
    bb, cc, pw = params
    nblk = S5_Q // S5_BLK
    return pl.pallas_call(
        _s5_kernel,
        grid=(nblk,),
        in_specs=[pl.BlockSpec((N_ROWS, 128), lambda q: (0, q)),
                  pl.BlockSpec((N_COND, 6, 128), lambda q: (0, 0, q)),
                  pl.BlockSpec((S5_BLK, 2, 2, S5_GR, 128), lambda q: (q, 0, 0, 0, 0)),
                  pl.BlockSpec((S5_BLK, 2, 2, S5_GR, 128), lambda q: (q, 0, 0, 0, 0)),
                  pl.BlockSpec((S5_BLK, 2, 2, S5_T + 1, 128), lambda q: (q, 0, 0, 0, 0)),
                  pl.BlockSpec((S5_BLK, 2, DEC_BATCH, 256), lambda q: (q, 0, 0, 0))],
        out_specs=[pl.BlockSpec((N_ROWS, 128), lambda q: (0, q)),
                   pl.BlockSpec((S5_BLK, 2, BATCH, 256), lambda q: (q, 0, 0, 0))],
        out_shape=[jax.ShapeDtypeStruct((N_ROWS, D), F32),
                   jax.ShapeDtypeStruct((S5_Q, 2, BATCH, 256), F32)],
        scratch_shapes=[pltpu.VMEM((S5_BLK, S5_ROWS, S5_W), BF16), pltpu.VMEM((S5_BLK, S5_ROWS, S5_W), F32),
                        pltpu.VMEM((S5_ROWS, 128), F32), pltpu.VMEM((S5_ROWS, 128), F32),
                        pltpu.VMEM((S5_ROWS, 128), F32), pltpu.VMEM((S5_ROWS, 128), F32),
                        pltpu.VMEM((2, S5_W, 256), F32),
                        pltpu.VMEM((2, S5_W, S5_W), BF16), pltpu.VMEM((2 * S5_W, S5_W), F32)],
        compiler_params=_cparams(("parallel",)),
        name="s5",
    )(x, mods, bb, cc, pw, h0)


def _s5_params(a_re, a_im, log_step, b_re, b_im, c_re, c_im):
    lam_re = jnp.minimum(a_re.astype(F32), -1e-4)
    lam_im = a_im.astype(F32)
    dt = jnp.exp(log_step.astype(F32))[..., None]
    j = jnp.arange(S5_T + 1, dtype=F32)[:, None, None, None]
    mag = jnp.exp(lam_re * dt * j)
    pw_re = mag * jnp.cos(lam_im * dt * j)
    pw_im = mag * jnp.sin(lam_im * dt * j)
    nr, ni = pw_re[1] - 1.0, pw_im[1]
    den = lam_re * lam_re + lam_im * lam_im
    q_re = (nr * lam_re + ni * lam_im) / den
    q_im = (ni * lam_re - nr * lam_im) / den
    b_re, b_im = b_re.astype(F32), b_im.astype(F32)
    bb_re = q_re[..., None] * b_re - q_im[..., None] * b_im
    bb_im = q_re[..., None] * b_im + q_im[..., None] * b_re
    eye = jnp.eye(2, dtype=F32)

    def pair_rows(m, spec):
        m = m.reshape(2, S5_Q, 2, *m.shape[2:])
        return jnp.einsum(spec, m, eye).reshape(S5_Q, 2, S5_GR, 2 * S5_P)

    bb = jnp.stack([pair_rows(bb_re, "kqgpc,gh->qkgchp"), pair_rows(bb_im, "kqgpc,gh->qkgchp")], axis=2)
    cc = jnp.stack([pair_rows(c_re.astype(F32), "kqgcp,gh->qkgchp"),
                    pair_rows(c_im.astype(F32), "kqgcp,gh->qkgchp")], axis=2)

    def pw_lay(p):
        return p.reshape(S5_T + 1, 2, S5_Q, 2 * S5_P).transpose(2, 1, 0, 3)
    pw = jnp.stack([pw_lay(pw_re), pw_lay(pw_im)], axis=2)
    return bb, cc, pw


def _s5_state_in(st_re, st_im):
    def lay(s):
        return s.astype(F32).reshape(DEC_BATCH, 2, S5_Q, 2 * S5_P).transpose(2, 1, 0, 3)
    return jnp.concatenate([lay(st_re), lay(st_im)], axis=-1)


def _s5_state_out(fin):
    def lay(s):
        return s.transpose(2, 1, 0, 3).reshape(BATCH, 2, S5_G, S5_P)
    return lay(fin[..., :2 * S5_P]), lay(fin[..., 2 * S5_P:])


def _glu_kernel(ys_ref, x_ref, mod_ref, d_ref, w_ref, g_ref, b_ref, x1_ref, h2_ref):
    for r in range(TM // ROW_SUB):
        rows = slice(r * ROW_SUB, (r + 1) * ROW_SUB)
        x = x_ref[rows, :]
        h = _modulate(x, mod_ref, 0, 1)
        y = jax.nn.gelu(ys_ref[rows, :] + d_ref[...] * h).astype(BF16)
        z = jnp.dot(y, w_ref[...], preferred_element_type=F32)
        f = z[:, :D] * jax.nn.sigmoid(z[:, D:])
        x1 = _layer_norm(ALPHA * x + (1.0 + mod_ref[2:3, :]) * f, g_ref[...], b_ref[...])
        x1_ref[rows, :] = x1
        h2_ref[rows, :] = _modulate(x1, mod_ref, 3, 4).astype(BF16)


def _glu(ysum, x, mod, d, w_glu, ln_g, ln_b):
    return pl.pallas_call(
        _glu_kernel,
        grid=(N_ROWS // TM,),
        in_specs=[_row_spec(TM), _row_spec(TM), _mod_spec(TM), _const_spec((1, D)),
                  _const_spec((D, 2 * D)), _const_spec((1, D)), _const_spec((1, D))],
        out_specs=[_row_spec(TM), _row_spec(TM)],
        out_shape=[jax.ShapeDtypeStruct((N_ROWS, D), F32), jax.ShapeDtypeStruct((N_ROWS, D), BF16)],
        compiler_params=_cparams(("parallel",)),
        name="glu",
    )(ysum, x, mod, d.reshape(1, D), w_glu, ln_g.reshape(1, D), ln_b.reshape(1, D))


def _finish(x1, g, mod_ref, g_ref, b_ref, x2_ref):
    x2 = _layer_norm(ALPHA * x1 + (1.0 + mod_ref[5:6, :]) * g, g_ref[...], b_ref[...])
    x2_ref[...] = x2
    return x2


FFN_TF = FFN_DIM // 2


def _ffn_kernel(h_ref, x1_ref, mod_ref, nmod_ref, wg_ref, wu_ref, wd_ref, g_ref, b_ref, x2_ref, hn_ref):
    h = h_ref[...]
    acc = None
    for c in range(FFN_DIM // FFN_TF):
        sl = slice(c * FFN_TF, (c + 1) * FFN_TF)
        a = jnp.dot(h, wg_ref[:, sl], preferred_element_type=F32)
        b = jnp.dot(h, wu_ref[:, sl], preferred_element_type=F32)
        p = jnp.dot((jax.nn.silu(a) * b).astype(BF16), wd_ref[sl, :], preferred_element_type=F32)
        acc = p if acc is None else acc + p
    x2 = _finish(x1_ref[...], acc, mod_ref, g_ref, b_ref, x2_ref)
    hn_ref[...] = _modulate(x2, nmod_ref, 0, 1).astype(BF16)


def _ffn(h2, x1, mod, nmod, wg, wu, wd, ln_g, ln_b):
    single = pl.Buffered(1)
    return pl.pallas_call(
        _ffn_kernel,
        grid=(N_ROWS // TM,),
        in_specs=[_row_spec(TM), _row_spec(TM), _mod_spec(TM), _mod_spec(TM),
                  pl.BlockSpec((D, FFN_DIM), lambda t: (0, 0), pipeline_mode=single),
                  pl.BlockSpec((D, FFN_DIM), lambda t: (0, 0), pipeline_mode=single),
                  pl.BlockSpec((FFN_DIM, D), lambda t: (0, 0), pipeline_mode=single),
                  _const_spec((1, D)), _const_spec((1, D))],
        out_specs=[_row_spec(TM), _row_spec(TM)],
        out_shape=[jax.ShapeDtypeStruct((N_ROWS, D), F32), jax.ShapeDtypeStruct((N_ROWS, D), BF16)],
        compiler_params=_cparams(("parallel",)),
        name="ffn",
    )(h2, x1, mod, nmod, wg, wu, wd, ln_g.reshape(1, D), ln_b.reshape(1, D))


def _gmlp_kernel(h_ref, x_ref, mod_ref, win_ref, bin_ref, vg_ref, vb_ref, ws_ref, bs_ref, wout_ref,
                 g_ref, b_ref, wr_ref, br_ref, x1_ref, h2_ref, idx_ref, p_ref, s_scr):
    for r in range(TM // ROW_SUB):
        rows = slice(r * ROW_SUB, (r + 1) * ROW_SUB)
        z = jax.nn.gelu(jnp.dot(h_ref[rows, :], win_ref[...], preferred_element_type=F32) + bin_ref[...])
        u = z[:, :D]
        v = _layer_norm(z[:, D:], vg_ref[...], vb_ref[...]).astype(BF16)
        n_chunks = ROW_SUB // GM_CHUNK
        for hd in range(GM_HEADS):
            cols = slice(hd * GM_CHUNK, (hd + 1) * GM_CHUNK)
            rhs = jnp.concatenate([v[n * GM_CHUNK:(n + 1) * GM_CHUNK, cols] for n in range(n_chunks)], axis=-1)
            s = jnp.dot(ws_ref[hd], rhs, preferred_element_type=F32)
            for n in range(n_chunks):
                lo = r * ROW_SUB + n * GM_CHUNK
                s_scr[lo:lo + GM_CHUNK, cols] = s[:, n * GM_CHUNK:(n + 1) * GM_CHUNK] + bs_ref[hd]
        f = jnp.dot((u * s_scr[rows, :]).astype(BF16), wout_ref[...], preferred_element_type=F32)
        x1 = _layer_norm(ALPHA * x_ref[rows, :] + (1.0 + mod_ref[2:3, :]) * f, g_ref[...], b_ref[...])
        x1_ref[rows, :] = x1
        h2 = _modulate(x1, mod_ref, 3, 4)
        h2_ref[rows, :] = h2
        h_hi = h2.astype(BF16)
        h_lo = (h2 - h_hi.astype(F32)).astype(BF16)
        w_hi, w_lo = wr_ref[0], wr_ref[1]
        logits = (jnp.dot(h_hi, w_hi, preferred_element_type=F32) + jnp.dot(h_lo, w_hi, preferred_element_type=F32)
                  + jnp.dot(h_hi, w_lo, preferred_element_type=F32)) + br_ref[...]
        lane = lax.broadcasted_iota(jnp.int32, logits.shape, 1)
        neg = jnp.float32(-jnp.inf)
        logits = jnp.where(lane < N_EXPERTS, logits, neg)
        v1 = jnp.max(logits, axis=-1, keepdims=True)
        i1 = jnp.min(jnp.where(logits == v1, lane, 128), axis=-1, keepdims=True)
        rest = jnp.where(lane == i1, neg, logits)
        v2 = jnp.max(rest, axis=-1, keepdims=True)
        i2 = jnp.min(jnp.where(rest == v2, lane, 128), axis=-1, keepdims=True)
        e2 = jnp.exp(v2 - v1)
        p1 = 1.0 / (1.0 + e2)
        p2 = e2 / (1.0 + e2)
        idx_ref[rows, :] = jnp.where(lane == 0, i1, jnp.where(lane == 1, i2, 0))
        p_ref[rows, :] = jnp.where(lane == 0, p1, jnp.where(lane == 1, p2, 0.0))


def _gmlp(h1, x, mod, w_in, b_in, vg, vb, w_s, b_s, w_out, ln_g, ln_b, w_r, b_r):
    return pl.pallas_call(
        _gmlp_kernel,
        grid=(N_ROWS // TM,),
        in_specs=[_row_spec(TM), _row_spec(TM), _mod_spec(TM),
                  _const_spec((D, 2 * D)), _const_spec((1, 2 * D)), _const_spec((1, D)), _const_spec((1, D)),
                  _const_spec((GM_HEADS, GM_CHUNK, GM_CHUNK)), _const_spec((GM_HEADS, GM_CHUNK, GM_CHUNK)),
                  _const_spec((D, D)), _const_spec((1, D)), _const_spec((1, D)),
                  _const_spec((2, D, 128)), _const_spec((1, 128))],
        out_specs=[_row_spec(TM), _row_spec(TM), _row_spec(TM, 128), _row_spec(TM, 128)],
        out_shape=[jax.ShapeDtypeStruct((N_ROWS, D), F32), jax.ShapeDtypeStruct((N_ROWS, D), F32),
                   jax.ShapeDtypeStruct((N_ROWS, 128), jnp.int32), jax.ShapeDtypeStruct((N_ROWS, 128), F32)],
        scratch_shapes=[pltpu.VMEM((TM, D), F32)],
        compiler_params=_cparams(("parallel",)),
        name="gmlp",
    )(h1, x, mod, w_in, b_in.reshape(1, 2 * D), vg.reshape(1, D), vb.reshape(1, D), w_s, b_s, w_out,
      ln_g.reshape(1, D), ln_b.reshape(1, D), w_r, b_r)


def _moe_kernel(kind_ref, chunk_ref, ux_ref, uo_ref, ue_ref, x_ref, wg_ref, wu_ref, wd_ref, y_ref,
                acc_ref, xb_ref, act_ref, wgb, wub, wdb):
    u = pl.program_id(0)
    nf = EXPERT_DIM // MOE_TF
    kind = kind_ref[u]
    c = chunk_ref[u]

    def swiglu_chunk(x, k):
        a = jnp.dot(x, wgb[k], preferred_element_type=F32)
        b = jnp.dot(x, wub[k], preferred_element_type=F32)
        return (jax.nn.silu(a) * b).astype(BF16)

    @pl.when(kind == MOE_STREAM)
    def _():
        rows_c = pl.ds(pl.multiple_of(c * MOE_TF, MOE_TF), MOE_TF)
        wgb[c] = wg_ref[...].astype(BF16)
        wub[c] = wu_ref[...].astype(BF16)
        wdb[rows_c, :] = wd_ref[...].astype(BF16)

        @pl.when(c == 0)
        def _():
            xb_ref[...] = x_ref[...].astype(BF16)

        p = jnp.dot(swiglu_chunk(xb_ref[...], c), wdb[rows_c, :], preferred_element_type=F32)
        acc_ref[...] = jnp.where(c == 0, p, acc_ref[...] + p)

        @pl.when(c == nf - 1)
        def _():
            y_ref[...] = acc_ref[...]

    @pl.when(kind == MOE_CACHED)
    def _():
        x = x_ref[...].astype(BF16)
        for k in range(nf):
            act_ref[:, k * MOE_TF:(k + 1) * MOE_TF] = swiglu_chunk(x, k)
        y_ref[...] = jnp.dot(act_ref[...], wdb[...], preferred_element_type=F32)

    @pl.when(kind == MOE_ZERO)
    def _():
        y_ref[...] = jnp.zeros_like(y_ref)


def _moe(layer, units, xs, wg, wu, wd):
    nf = EXPERT_DIM // MOE_TF
    grid_spec = pltpu.PrefetchScalarGridSpec(
        num_scalar_prefetch=5,
        grid=(MOE_UNITS,),
        in_specs=[pl.BlockSpec((MOE_TM, D), lambda u, kind, chunk, ux, uo, ue: (ux[u], 0)),
                  pl.BlockSpec((None, None, D, MOE_TF),
                               lambda u, kind, chunk, ux, uo, ue: (layer, ue[u], 0, chunk[u])),
                  pl.BlockSpec((None, None, D, MOE_TF),
                               lambda u, kind, chunk, ux, uo, ue: (layer, ue[u], 0, chunk[u])),
                  pl.BlockSpec((None, None, MOE_TF, D),
                               lambda u, kind, chunk, ux, uo, ue: (layer, ue[u], chunk[u], 0))],
        out_specs=pl.BlockSpec((MOE_TM, D), lambda u, kind, chunk, ux, uo, ue: (uo[u], 0)),
        scratch_shapes=[pltpu.VMEM((MOE_TM, D), F32), pltpu.VMEM((MOE_TM, D), BF16),
                        pltpu.VMEM((MOE_TM, EXPERT_DIM), BF16),
                        pltpu.VMEM((nf, D, MOE_TF), BF16), pltpu.VMEM((nf, D, MOE_TF), BF16),
                        pltpu.VMEM((EXPERT_DIM, D), BF16)],
    )
    return pl.pallas_call(
        _moe_kernel,
        grid_spec=grid_spec,
        out_shape=jax.ShapeDtypeStruct((MOE_ROWS, D), F32),
        compiler_params=_cparams(("arbitrary",)),
        name="moe",
    )(*units, xs, wg, wu, wd)


def _take_rows(a, rows):
    return a.at[rows].get(mode="promise_in_bounds")


def _route(idx):
    e = idx.reshape(-1)
    onehot = (e[:, None] == jnp.arange(N_EXPERTS, dtype=jnp.int32)[None, :]).astype(jnp.int32)
    counts = jnp.sum(onehot, axis=0)
    rank = jnp.sum((jnp.cumsum(onehot, axis=0) - 1) * onehot, axis=1)
    padded = ((counts + MOE_TM - 1) // MOE_TM) * MOE_TM
    ends = jnp.cumsum(padded)
    starts = ends - padded
    pos = starts[e] + rank
    token = jnp.arange(N_ROWS * TOP_K, dtype=jnp.int32) // TOP_K
    src = jnp.zeros((MOE_ROWS,), jnp.int32).at[pos].set(token)
    n_used = (ends[-1] // MOE_TM).astype(jnp.int32)
    tile_start = jnp.minimum(jnp.arange(MOE_TILES, dtype=jnp.int32), n_used - 1) * MOE_TM
    tile_expert = jnp.sum((tile_start[:, None] >= ends[None, :]).astype(jnp.int32), axis=1)
    tile_expert = jnp.minimum(tile_expert, N_EXPERTS - 1).astype(jnp.int32)
    prev = jnp.concatenate([jnp.full((1,), -1, jnp.int32), tile_expert[:-1]])
    tile_first = tile_expert != prev
    nf = EXPERT_DIM // MOE_TF
    tiles = jnp.arange(MOE_TILES, dtype=jnp.int32)
    live = tiles < n_used
    cost = jnp.where(live, jnp.where(tile_first, nf, 1), 0).astype(jnp.int32)
    off = jnp.cumsum(cost) - cost
    n_units = jnp.sum(cost)
    u = jnp.arange(MOE_UNITS, dtype=jnp.int32)
    tile_u = jnp.sum(jnp.logical_and(off[None, :] <= u[:, None], live[None, :]).astype(jnp.int32), axis=1) - 1
    first_u = tile_first[tile_u]
    work = u < n_units
    spare = u - n_units
    unit_kind = jnp.where(work, jnp.where(first_u, MOE_STREAM, MOE_CACHED),
                          jnp.where(spare < MOE_TILES - n_used, MOE_ZERO, MOE_IDLE)).astype(jnp.int32)
    unit_chunk = jnp.where(jnp.logical_and(work, first_u), u - off[tile_u], nf - 1).astype(jnp.int32)
    unit_out = jnp.where(work, tile_u, jnp.minimum(n_used + spare, MOE_TILES - 1)).astype(jnp.int32)
    unit_x = tile_u.astype(jnp.int32)
    unit_expert = tile_expert[tile_u]
    return src, pos.reshape(N_ROWS, TOP_K), (unit_kind, unit_chunk, unit_x, unit_out, unit_expert)


def _combine_kernel(ya_ref, yb_ref, p_ref, x1_ref, mod_ref, g_ref, b_ref, x2_ref):
    g = p_ref[:, 0:1] * ya_ref[...] + p_ref[:, 1:2] * yb_ref[...]
    _finish(x1_ref[...], g, mod_ref, g_ref, b_ref, x2_ref)


def _combine(ya, yb, p, x1, mod, ln_g, ln_b):
    return pl.pallas_call(
        _combine_kernel,
        grid=(N_ROWS // TM,),
        in_specs=[_row_spec(TM), _row_spec(TM), _row_spec(TM, 128), _row_spec(TM), _mod_spec(TM),
                  _const_spec((1, D)), _const_spec((1, D))],
        out_specs=_row_spec(TM),
        out_shape=jax.ShapeDtypeStruct((N_ROWS, D), F32),
        compiler_params=_cparams(("parallel",)),
        name="combine",
    )(ya, yb, p, x1, mod, ln_g.reshape(1, D), ln_b.reshape(1, D))


def _grid_pos_embed():
    rows = DEC_SEQ // GRID_W
    r, col = jnp.meshgrid(jnp.arange(rows, dtype=F32), jnp.arange(GRID_W, dtype=F32), indexing="ij")
    quarter = D // 4
    freq = 1.0 / (10000.0 ** (jnp.arange(quarter, dtype=F32) / quarter))

    def emb(p):
        ang = p.reshape(-1)[:, None] * freq[None, :]
        return jnp.concatenate([jnp.sin(ang), jnp.cos(ang)], axis=-1)

    return jnp.concatenate([emb(r), emb(col)], axis=-1)


def kernel(x_prompt, x_sample, c, state_ssm_re, state_ssm_im, c_ctx, w_ada, b_ada, ln_g, ln_b, s5_a_re, s5_a_im, s5_log_step, s5_b_re, s5_b_im, s5_c_re, s5_c_im, s5_d, s5_w_glu, gm_w_in, gm_b_in, gm_ln_g, gm_ln_b, gm_w_s, gm_b_s, gm_w_out, ffn_w_gate, ffn_w_up, ffn_w_down, moe_w_router, moe_b_router, moe_w_gate, moe_w_up, moe_w_down):
    cond8 = jnp.concatenate([c_ctx[None, :], c, jnp.zeros((N_COND - 1 - DEC_BATCH, D), F32)], axis=0)
    mods = _ada(cond8, w_ada, b_ada).reshape(DEPTH, N_COND, 6, D)

    x = _prep(x_prompt, x_sample, _grid_pos_embed())
    h = None

    fin_re, fin_im = [], []
    for i in range(DEPTH):
        j = i // 2
        mod = mods[i]
        nmod = mods[min(i + 1, DEPTH - 1)]
        if i % 2 == 0:
            params = _s5_params(s5_a_re[j], s5_a_im[j], s5_log_step[j], s5_b_re[j], s5_b_im[j], s5_c_re[j], s5_c_im[j])
            ysum, fin = _s5(x, mod, params, _s5_state_in(state_ssm_re[:, j], state_ssm_im[:, j]))
            fr, fi = _s5_state_out(fin)
            fin_re.append(fr)
            fin_im.append(fi)
            x1, h2 = _glu(ysum, x, mod, s5_d[j], s5_w_glu[j].astype(BF16), ln_g[i, 0], ln_b[i, 0])
            x, h = _ffn(h2, x1, mod, nmod, ffn_w_gate[j].astype(BF16), ffn_w_up[j].astype(BF16),
                        ffn_w_down[j].astype(BF16), ln_g[i, 1], ln_b[i, 1])
        else:
            b_s = jnp.broadcast_to(gm_b_s[j][:, :, None], (GM_HEADS, GM_CHUNK, GM_CHUNK)).astype(F32)
            w_r = jnp.pad(moe_w_router[j].astype(F32), ((0, 0), (0, 128 - N_EXPERTS)))
            w_r_hi = w_r.astype(BF16)
            w_r = jnp.stack([w_r_hi, (w_r - w_r_hi.astype(F32)).astype(BF16)], axis=0)
            b_r = jnp.pad(moe_b_router[j].astype(F32), (0, 128 - N_EXPERTS)).reshape(1, 128)
            x1, h2, idx, p = _gmlp(h, x, mod, gm_w_in[j].astype(BF16), gm_b_in[j], gm_ln_g[j], gm_ln_b[j],
                                   gm_w_s[j].astype(BF16), b_s, gm_w_out[j].astype(BF16), ln_g[i, 0], ln_b[i, 0], w_r, b_r)
            src, pos, units = _route(idx[:, :TOP_K])
            ys = _moe(j, units, _take_rows(h2, src), moe_w_gate, moe_w_up, moe_w_down)
            x = _combine(_take_rows(ys, pos[:, 0]), _take_rows(ys, pos[:, 1]), p, x1, mod, ln_g[i, 1], ln_b[i, 1])

    y_prompt = x[:N_P].reshape(BATCH, SEQ, D)
    y_sample = x[N_P:].reshape(DEC_BATCH, DEC_SEQ, D)
    new_re = jnp.stack(fin_re, axis=1).astype(x_prompt.dtype)
    new_im = jnp.stack(fin_im, axis=1).astype(x_prompt.dtype)
    return (y_prompt, y_sample, new_re, new_im)
```

```python
import jax
import jax.numpy as jnp
from jax import lax
from jax.experimental import pallas as pl
from jax.experimental.pallas import tpu as pltpu

F32 = jnp.float32
BF16 = jnp.bfloat16

D = 1024
BATCH, SEQ = 32, 256
DEC_BATCH, DEC_SEQ = 4, 1024
DEPTH = 4
GRID_W = 64
N_P = BATCH * SEQ
N_S = DEC_BATCH * DEC_SEQ
N_ROWS = N_P + N_S
N_COND = 8

S5_CG, S5_P, S5_G = 16, 64, 64
S5_T = 16
S5_Q = S5_G // 2
P_CHUNKS = SEQ // S5_T
S_CHUNKS = DEC_SEQ // S5_T
S5_ROWS_P = P_CHUNKS * BATCH
S5_ROWS_S = S_CHUNKS * DEC_BATCH
S5_ROWS = S5_ROWS_P + S5_ROWS_S

GM_CHUNK, GM_HEADS = 128, 8
FFN_DIM = 2816
N_EXPERTS, TOP_K, EXPERT_DIM = 8, 2, 3584
ALPHA = (2.0 * DEPTH) ** 0.25
LN_EPS = 1e-5

TM = 512
ROW_SUB = TM
MOE_TM = 512
MOE_TF = 512
MOE_TILES = (N_ROWS * TOP_K) // MOE_TM + N_EXPERTS
MOE_ROWS = MOE_TILES * MOE_TM
MOE_UNITS = N_EXPERTS * (EXPERT_DIM // MOE_TF) + MOE_TILES
MOE_STREAM, MOE_CACHED, MOE_ZERO, MOE_IDLE = 1, 2, 0, 3
MOE_DRAIN = -2
VMEM_LIMIT = 56 * 1024 * 1024


def _cparams(sem):
    return pltpu.CompilerParams(dimension_semantics=sem, vmem_limit_bytes=VMEM_LIMIT)


def _cond_of_tile(t, tm):
    row0 = t * tm
    return jnp.where(row0 < N_P, 0, 1 + (row0 - N_P) // DEC_SEQ)


def _mod_spec(tm):
    return pl.BlockSpec((None, 6, D), lambda t: (_cond_of_tile(t, tm), 0, 0))


def _row_spec(tm, width=D):
    return pl.BlockSpec((tm, width), lambda t: (t, 0))


def _const_spec(shape):
    return pl.BlockSpec(shape, lambda t: (0,) * len(shape))


def _layer_norm(r, g, b):
    mu = jnp.mean(r, axis=-1, keepdims=True)
    c = r - mu
    var = jnp.mean(c * c, axis=-1, keepdims=True)
    return c * lax.rsqrt(var + LN_EPS) * g + b


def _modulate(x, mod_ref, shift_row, scale_row):
    return x * (1.0 + mod_ref[scale_row:scale_row + 1, :]) + mod_ref[shift_row:shift_row + 1, :]


def _ada_kernel(c_ref, w_ref, b_ref, o_ref):
    c = c_ref[...]
    o_ref[...] = jnp.dot(jax.nn.silu(c), w_ref[...], preferred_element_type=F32) + b_ref[...]


def _ada(cond8, w_ada, b_ada):
    tn = 1536
    return pl.pallas_call(
        _ada_kernel,
        grid=(DEPTH, 6 * D // tn),
        in_specs=[pl.BlockSpec((N_COND, D), lambda i, n: (0, 0)),
                  pl.BlockSpec((None, D, tn), lambda i, n: (i, 0, n)),
                  pl.BlockSpec((None, 1, tn), lambda i, n: (i, 0, n))],
        out_specs=pl.BlockSpec((None, N_COND, tn), lambda i, n: (i, 0, n)),
        out_shape=jax.ShapeDtypeStruct((DEPTH, N_COND, 6 * D), F32),
        compiler_params=_cparams(("arbitrary", "arbitrary")),
        name="ada",
    )(cond8, w_ada, b_ada.reshape(DEPTH, 1, 6 * D))


def _prep_kernel(xp_ref, xs_ref, pos_ref, xo_ref):
    is_latent = pl.program_id(0) * TM >= N_P
    xo_ref[...] = jnp.where(is_latent, xs_ref[...] + pos_ref[...], xp_ref[...])


def _prep(x_prompt, x_sample, pos):
    p_tiles = N_P // TM

    def latent_tile(t):
        return jnp.maximum(t - p_tiles, 0)

    return pl.pallas_call(
        _prep_kernel,
        grid=(N_ROWS // TM,),
        in_specs=[pl.BlockSpec((TM, D), lambda t: (jnp.minimum(t, p_tiles - 1), 0)),
                  pl.BlockSpec((TM, D), lambda t: (latent_tile(t), 0)),
                  pl.BlockSpec((TM, D), lambda t: (latent_tile(t) % (DEC_SEQ // TM), 0))],
        out_specs=_row_spec(TM),
        out_shape=jax.ShapeDtypeStruct((N_ROWS, D), F32),
        compiler_params=_cparams(("parallel",)),
        name="prep",
    )(x_prompt.reshape(N_P, D), x_sample.reshape(N_S, D), pos)


S5_BLK = 4
S5_W = 2 * S5_T * S5_CG
S5_GR = 2 * S5_CG
_NT = (((1,), (1,)), ((), ()))


def _granule_transpose4(a, lane):
    low = lane < 64
    a0 = jnp.where(low, a[0], pltpu.roll(a[2], 64, 1))
    a2 = jnp.where(low, pltpu.roll(a[0], 64, 1), a[2])
    a1 = jnp.where(low, a[1], pltpu.roll(a[3], 64, 1))
    a3 = jnp.where(low, pltpu.roll(a[1], 64, 1), a[3])
    even = (lane // S5_GR) % 2 == 0
    return [jnp.where(even, a0, pltpu.roll(a1, 32, 1)), jnp.where(even, pltpu.roll(a0, 96, 1), a1),
            jnp.where(even, a2, pltpu.roll(a3, 32, 1)), jnp.where(even, pltpu.roll(a2, 96, 1), a3)]


def _s5_chunk_matrices(p4, bb_ref, c_ref, pw_ref, min_scr, mi_scr, p_scr):
    half = 2 * S5_P
    gran = lax.broadcasted_iota(jnp.int32, (S5_W, 128), 1) // S5_GR
    rep = (lax.broadcasted_iota(jnp.int32, (S5_GR, S5_W), 1) % S5_GR
           == lax.broadcasted_iota(jnp.int32, (S5_GR, S5_W), 0)).astype(BF16)
    mout = []
    for k in range(2):
        bb_re, bb_im = bb_ref[p4, k, 0], bb_ref[p4, k, 1]
        c_re, c_im = c_ref[p4, k, 0], c_ref[p4, k, 1]
        out_blocks = []
        for r in range(S5_T):
            t_in = S5_T - 1 - r if k == 0 else r
            t_out = r + 1 if k == 0 else S5_T - r
            rows = slice(S5_GR * r, S5_GR * (r + 1))
            pr, pi = pw_ref[p4, k, 0, t_in:t_in + 1, :], pw_ref[p4, k, 1, t_in:t_in + 1, :]
            min_scr[k, rows, 0:half] = bb_re * pr - bb_im * pi
            min_scr[k, rows, half:2 * half] = bb_re * pi + bb_im * pr
            pr, pi = pw_ref[p4, k, 0, t_out:t_out + 1, :], pw_ref[p4, k, 1, t_out:t_out + 1, :]
            out_blocks.append(jnp.concatenate([c_re * pr - c_im * pi, -(c_re * pi + c_im * pr)], axis=-1))
        mout.append(jnp.concatenate(out_blocks, axis=0).astype(BF16))
        c2 = jnp.concatenate([c_re, -c_im], axis=-1)
        c2_hi = c2.astype(BF16)
        c2_lo = (c2 - c2_hi.astype(F32)).astype(BF16)
        m = min_scr[k]
        m_hi = m.astype(BF16)
        m_lo = (m - m_hi.astype(F32)).astype(BF16)
        z = (lax.dot_general(m_hi, c2_hi, _NT, preferred_element_type=F32)
             + lax.dot_general(m_lo, c2_hi, _NT, preferred_element_type=F32)
             + lax.dot_general(m_hi, c2_lo, _NT, preferred_element_type=F32))
        zrep = jnp.dot(z.astype(BF16), rep, preferred_element_type=F32)
        zeros = jnp.zeros((S5_W, S5_W), F32)
        p_scr[0:S5_W, :] = zrep if k == 0 else zeros
        p_scr[S5_W:2 * S5_W, :] = zeros if k == 0 else zrep
        for j in range(S5_W // 128):
            cols = slice(128 * j, 128 * (j + 1))
            acc = None
            for s4 in range(4):
                s = 4 * j + s4
                start = S5_GR * (S5_T - 1 - s) if k == 0 else S5_W - S5_GR * s
                win = p_scr[start:start + S5_W, cols]
                acc = win if acc is None else jnp.where(gran == s4, win, acc)
            mi_scr[k, :, cols] = acc.astype(BF16)
    return mout


def _s5_scan(k, s_scrs, h_scrs, row0, nb, nchunks, hre, him, ar, ai):
    for n in (range(nchunks - 1, -1, -1) if k == 1 else range(nchunks)):
        rows = pl.ds(row0 + n, nb, stride=nchunks)
        h_scrs[0][rows, :] = hre
        h_scrs[1][rows, :] = him
        sre = s_scrs[0][rows, :]
        sim = s_scrs[1][rows, :]
        hre, him = ar * hre - ai * him + sre, ar * him + ai * hre + sim
    return hre, him


def _s5_kernel(x_ref, mod_ref, bb_ref, c_ref, pw_ref, h0_ref, y_ref, fin_ref,
               u_scr, yp_scr, sre_scr, sim_scr, hre_scr, him_scr, min_scr, mi_scr, p_scr):
    half = 2 * S5_P
    lane = lax.broadcasted_iota(jnp.int32, (S5_ROWS, 128), 1)
    scale = jnp.concatenate(
        [jnp.broadcast_to(1.0 + mod_ref[0, 1:2, :], (S5_ROWS_P, 128))]
        + [jnp.broadcast_to(1.0 + mod_ref[1 + b, 1:2, :], (S_CHUNKS, 128)) for b in range(DEC_BATCH)], axis=0)
    shift = jnp.concatenate(
        [jnp.broadcast_to(mod_ref[0, 0:1, :], (S5_ROWS_P, 128))]
        + [jnp.broadcast_to(mod_ref[1 + b, 0:1, :], (S_CHUNKS, 128)) for b in range(DEC_BATCH)], axis=0)
    for j in range(S5_T // 4):
        a = [x_ref[pl.ds(4 * j + s4, S5_ROWS, stride=S5_T), :] * scale + shift for s4 in range(4)]
        b = _granule_transpose4(a, lane)
        for p4 in range(S5_BLK):
            u_scr[p4, :, 128 * j:128 * (j + 1)] = b[p4].astype(BF16)

    for p4 in range(S5_BLK):
        mout = _s5_chunk_matrices(p4, bb_ref, c_ref, pw_ref, min_scr, mi_scr, p_scr)
        u = u_scr[p4]
        u0, u1 = u[:, 0:256], u[:, 256:512]
        y = None
        for k in range(2):
            s = jnp.dot(u, min_scr[k].astype(BF16), preferred_element_type=F32)
            sre_scr[...] = s[:, 0:half]
            sim_scr[...] = s[:, half:2 * half]
            s_scrs, h_scrs = (sre_scr, sim_scr), (hre_scr, him_scr)
            ar_p = jnp.broadcast_to(pw_ref[p4, k, 0, S5_T:S5_T + 1, :], (BATCH, half))
            ai_p = jnp.broadcast_to(pw_ref[p4, k, 1, S5_T:S5_T + 1, :], (BATCH, half))
            zero = jnp.zeros((BATCH, half), F32)
            fre, fim = _s5_scan(k, s_scrs, h_scrs, 0, BATCH, P_CHUNKS, zero, zero, ar_p, ai_p)
            fin_ref[p4, k, :, 0:half] = fre
            fin_ref[p4, k, :, half:2 * half] = fim
            _s5_scan(k, s_scrs, h_scrs, S5_ROWS_P, DEC_BATCH, S_CHUNKS,
                     h0_ref[p4, k, :, 0:half], h0_ref[p4, k, :, half:2 * half], ar_p[:DEC_BATCH], ai_p[:DEC_BATCH])
            hin = jnp.concatenate([hre_scr[...], him_scr[...]], axis=-1).astype(BF16)
            readout = lax.dot_general(hin, mout[k], _NT, preferred_element_type=F32)
            m00 = jnp.dot(u0, mi_scr[k, 0:256, 0:256], preferred_element_type=F32)
            m11 = jnp.dot(u1, mi_scr[k, 256:512, 256:512], preferred_element_type=F32)
            if k == 0:
                m11 = m11 + jnp.dot(u0, mi_scr[k, 0:256, 256:512], preferred_element_type=F32)
            else:
                m00 = m00 + jnp.dot(u1, mi_scr[k, 256:512, 0:256], preferred_element_type=F32)
            yk = jnp.concatenate([m00, m11], axis=-1) + readout
            y = yk if y is None else y + yk
        yp_scr[p4] = y

    for j in range(S5_T // 4):
        b = [yp_scr[p4, :, 128 * j:128 * (j + 1)] for p4 in range(S5_BLK)]
        a = _granule_transpose4(b, lane)
        for s4 in range(4):
            y_ref[pl.ds(4 * j + s4, S5_ROWS, stride=S5_T), :] = a[s4]


def _s5(x, mods, params, h0):
    bb, cc, pw = params
    nblk = S5_Q // S5_BLK
    return pl.pallas_call(
        _s5_kernel,
        grid=(nblk,),
        in_specs=[pl.BlockSpec((N_ROWS, 128), lambda q: (0, q)),
                  pl.BlockSpec((N_COND, 6, 128), lambda q: (0, 0, q)),
                  pl.BlockSpec((S5_BLK, 2, 2, S5_GR, 128), lambda q: (q, 0, 0, 0, 0)),
                  pl.BlockSpec((S5_BLK, 2, 2, S5_GR, 128), lambda q: (q, 0, 0, 0, 0)),
                  pl.BlockSpec((S5_BLK, 2, 2, S5_T + 1, 128), lambda q: (q, 0, 0, 0, 0)),
                  pl.BlockSpec((S5_BLK, 2, DEC_BATCH, 256), lambda q: (q, 0, 0, 0))],
        out_specs=[pl.BlockSpec((N_ROWS, 128), lambda q: (0, q)),
                   pl.BlockSpec((S5_BLK, 2, BATCH, 256), lambda q: (q, 0, 0, 0))],
        out_shape=[jax.ShapeDtypeStruct((N_ROWS, D), F32),
                   jax.ShapeDtypeStruct((S5_Q, 2, BATCH, 256), F32)],
        scratch_shapes=[pltpu.VMEM((S5_BLK, S5_ROWS, S5_W), BF16), pltpu.VMEM((S5_BLK, S5_ROWS, S5_W), F32),
                        pltpu.VMEM((S5_ROWS, 128), F32), pltpu.VMEM((S5_ROWS, 128), F32),
                        pltpu.VMEM((S5_ROWS, 128), F32), pltpu.VMEM((S5_ROWS, 128), F32),
                        pltpu.VMEM((2, S5_W, 256), F32),
                        pltpu.VMEM((2, S5_W, S5_W), BF16), pltpu.VMEM((2 * S5_W, S5_W), F32)],
        compiler_params=_cparams(("parallel",)),
        name="s5",
    )(x, mods, bb, cc, pw, h0)


def _s5_params(a_re, a_im, log_step, b_re, b_im, c_re, c_im):
    lam_re = jnp.minimum(a_re.astype(F32), -1e-4)
    lam_im = a_im.astype(F32)
    dt = jnp.exp(log_step.astype(F32))[..., None]
    j = jnp.arange(S5_T + 1, dtype=F32)[:, None, None, None]
    mag = jnp.exp(lam_re * dt * j)
    pw_re = mag * jnp.cos(lam_im * dt * j)
    pw_im = mag * jnp.sin(lam_im * dt * j)
    nr, ni = pw_re[1] - 1.0, pw_im[1]
    den = lam_re * lam_re + lam_im * lam_im
    q_re = (nr * lam_re + ni * lam_im) / den
    q_im = (ni * lam_re - nr * lam_im) / den
    b_re, b_im = b_re.astype(F32), b_im.astype(F32)
    bb_re = q_re[..., None] * b_re - q_im[..., None] * b_im
    bb_im = q_re[..., None] * b_im + q_im[..., None] * b_re
    eye = jnp.eye(2, dtype=F32)

    def pair_rows(m, spec):
        m = m.reshape(2, S5_Q, 2, *m.shape[2:])
        return jnp.einsum(spec, m, eye).reshape(S5_Q, 2, S5_GR, 2 * S5_P)

    bb = jnp.stack([pair_rows(bb_re, "kqgpc,gh->qkgchp"), pair_rows(bb_im, "kqgpc,gh->qkgchp")], axis=2)
    cc = jnp.stack([pair_rows(c_re.astype(F32), "kqgcp,gh->qkgchp"),
                    pair_rows(c_im.astype(F32), "kqgcp,gh->qkgchp")], axis=2)

    def pw_lay(p):
        return p.reshape(S5_T + 1, 2, S5_Q, 2 * S5_P).transpose(2, 1, 0, 3)
    pw = jnp.stack([pw_lay(pw_re), pw_lay(pw_im)], axis=2)
    return bb, cc, pw


def _s5_state_in(st_re, st_im):
    def lay(s):
        return s.astype(F32).reshape(DEC_BATCH, 2, S5_Q, 2 * S5_P).transpose(2, 1, 0, 3)
    return jnp.concatenate([lay(st_re), lay(st_im)], axis=-1)


def _s5_state_out(fin):
    def lay(s):
        return s.transpose(2, 1, 0, 3).reshape(BATCH, 2, S5_G, S5_P)
    return lay(fin[..., :2 * S5_P]), lay(fin[..., 2 * S5_P:])


def _glu_kernel(ys_ref, x_ref, mod_ref, d_ref, w_ref, g_ref, b_ref, x1_ref, h2_ref):
    for r in range(TM // ROW_SUB):
        rows = slice(r * ROW_SUB, (r + 1) * ROW_SUB)
        x = x_ref[rows, :]
        h = _modulate(x, mod_ref, 0, 1)
        y = jax.nn.gelu(ys_ref[rows, :] + d_ref[...] * h).astype(BF16)
        z = jnp.dot(y, w_ref[...], preferred_element_type=F32)
        f = z[:, :D] * jax.nn.sigmoid(z[:, D:])
        x1 = _layer_norm(ALPHA * x + (1.0 + mod_ref[2:3, :]) * f, g_ref[...], b_ref[...])
        x1_ref[rows, :] = x1
        h2_ref[rows, :] = _modulate(x1, mod_ref, 3, 4).astype(BF16)


def _glu(ysum, x, mod, d, w_glu, ln_g, ln_b):
    return pl.pallas_call(
        _glu_kernel,
        grid=(N_ROWS // TM,),
        in_specs=[_row_spec(TM), _row_spec(TM), _mod_spec(TM), _const_spec((1, D)),
                  _const_spec((D, 2 * D)), _const_spec((1, D)), _const_spec((1, D))],
        out_specs=[_row_spec(TM), _row_spec(TM)],
        out_shape=[jax.ShapeDtypeStruct((N_ROWS, D), F32), jax.ShapeDtypeStruct((N_ROWS, D), BF16)],
        compiler_params=_cparams(("parallel",)),
        name="glu",
    )(ysum, x, mod, d.reshape(1, D), w_glu, ln_g.reshape(1, D), ln_b.reshape(1, D))


def _finish(x1, g, mod_ref, g_ref, b_ref, x2_ref):
    x2 = _layer_norm(ALPHA * x1 + (1.0 + mod_ref[5:6, :]) * g, g_ref[...], b_ref[...])
    x2_ref[...] = x2
    return x2


FFN_TF = FFN_DIM // 2


def _ffn_kernel(h_ref, x1_ref, mod_ref, nmod_ref, wg_ref, wu_ref, wd_ref, g_ref, b_ref, x2_ref, hn_ref):
    h = h_ref[...]
    acc = None
    for c in range(FFN_DIM // FFN_TF):
        sl = slice(c * FFN_TF, (c + 1) * FFN_TF)
        a = jnp.dot(h, wg_ref[:, sl], preferred_element_type=F32)
        b = jnp.dot(h, wu_ref[:, sl], preferred_element_type=F32)
        p = jnp.dot((jax.nn.silu(a) * b).astype(BF16), wd_ref[sl, :], preferred_element_type=F32)
        acc = p if acc is None else acc + p
    x2 = _finish(x1_ref[...], acc, mod_ref, g_ref, b_ref, x2_ref)
    hn_ref[...] = _modulate(x2, nmod_ref, 0, 1).astype(BF16)


def _ffn(h2, x1, mod, nmod, wg, wu, wd, ln_g, ln_b):
    single = pl.Buffered(1)
    return pl.pallas_call(
        _ffn_kernel,
        grid=(N_ROWS // TM,),
        in_specs=[_row_spec(TM), _row_spec(TM), _mod_spec(TM), _mod_spec(TM),
                  pl.BlockSpec((D, FFN_DIM), lambda t: (0, 0), pipeline_mode=single),
                  pl.BlockSpec((D, FFN_DIM), lambda t: (0, 0), pipeline_mode=single),
                  pl.BlockSpec((FFN_DIM, D), lambda t: (0, 0), pipeline_mode=single),
                  _const_spec((1, D)), _const_spec((1, D))],
        out_specs=[_row_spec(TM), _row_spec(TM)],
        out_shape=[jax.ShapeDtypeStruct((N_ROWS, D), F32), jax.ShapeDtypeStruct((N_ROWS, D), BF16)],
        compiler_params=_cparams(("parallel",)),
        name="ffn",
    )(h2, x1, mod, nmod, wg, wu, wd, ln_g.reshape(1, D), ln_b.reshape(1, D))


def _gmlp_kernel(h_ref, x_ref, mod_ref, win_ref, bin_ref, vg_ref, vb_ref, ws_ref, bs_ref, wout_ref,
                 g_ref, b_ref, wr_ref, br_ref, x1_ref, h2_ref, idx_ref, p_ref, s_scr):
    for r in range(TM // ROW_SUB):
        rows = slice(r * ROW_SUB, (r + 1) * ROW_SUB)
        z = jax.nn.gelu(jnp.dot(h_ref[rows, :], win_ref[...], preferred_element_type=F32) + bin_ref[...])
        u = z[:, :D]
        v = _layer_norm(z[:, D:], vg_ref[...], vb_ref[...]).astype(BF16)
        n_chunks = ROW_SUB // GM_CHUNK
        for hd in range(GM_HEADS):
            cols = slice(hd * GM_CHUNK, (hd + 1) * GM_CHUNK)
            rhs = jnp.concatenate([v[n * GM_CHUNK:(n + 1) * GM_CHUNK, cols] for n in range(n_chunks)], axis=-1)
            s = jnp.dot(ws_ref[hd], rhs, preferred_element_type=F32)
            for n in range(n_chunks):
                lo = r * ROW_SUB + n * GM_CHUNK
                s_scr[lo:lo + GM_CHUNK, cols] = s[:, n * GM_CHUNK:(n + 1) * GM_CHUNK] + bs_ref[hd]
        f = jnp.dot((u * s_scr[rows, :]).astype(BF16), wout_ref[...], preferred_element_type=F32)
        x1 = _layer_norm(ALPHA * x_ref[rows, :] + (1.0 + mod_ref[2:3, :]) * f, g_ref[...], b_ref[...])
        x1_ref[rows, :] = x1
        h2 = _modulate(x1, mod_ref, 3, 4)
        h2_ref[rows, :] = h2
        h_hi = h2.astype(BF16)
        h_lo = (h2 - h_hi.astype(F32)).astype(BF16)
        w_hi, w_lo = wr_ref[0], wr_ref[1]
        logits = (jnp.dot(h_hi, w_hi, preferred_element_type=F32) + jnp.dot(h_lo, w_hi, preferred_element_type=F32)
                  + jnp.dot(h_hi, w_lo, preferred_element_type=F32)) + br_ref[...]
        lane = lax.broadcasted_iota(jnp.int32, logits.shape, 1)
        neg = jnp.float32(-jnp.inf)
        logits = jnp.where(lane < N_EXPERTS, logits, neg)
        v1 = jnp.max(logits, axis=-1, keepdims=True)
        i1 = jnp.min(jnp.where(logits == v1, lane, 128), axis=-1, keepdims=True)
        rest = jnp.where(lane == i1, neg, logits)
        v2 = jnp.max(rest, axis=-1, keepdims=True)
        i2 = jnp.min(jnp.where(rest == v2, lane, 128), axis=-1, keepdims=True)
        e2 = jnp.exp(v2 - v1)
        p1 = 1.0 / (1.0 + e2)
        p2 = e2 / (1.0 + e2)
        idx_ref[rows, :] = jnp.where(lane == 0, i1, jnp.where(lane == 1, i2, 0))
        p_ref[rows, :] = jnp.where(lane == 0, p1, jnp.where(lane == 1, p2, 0.0))


def _gmlp(h1, x, mod, w_in, b_in, vg, vb, w_s, b_s, w_out, ln_g, ln_b, w_r, b_r):
    return pl.pallas_call(
        _gmlp_kernel,
        grid=(N_ROWS // TM,),
        in_specs=[_row_spec(TM), _row_spec(TM), _mod_spec(TM),
                  _const_spec((D, 2 * D)), _const_spec((1, 2 * D)), _const_spec((1, D)), _const_spec((1, D)),
                  _const_spec((GM_HEADS, GM_CHUNK, GM_CHUNK)), _const_spec((GM_HEADS, GM_CHUNK, GM_CHUNK)),
                  _const_spec((D, D)), _const_spec((1, D)), _const_spec((1, D)),
                  _const_spec((2, D, 128)), _const_spec((1, 128))],
        out_specs=[_row_spec(TM), _row_spec(TM), _row_spec(TM, 128), _row_spec(TM, 128)],
        out_shape=[jax.ShapeDtypeStruct((N_ROWS, D), F32), jax.ShapeDtypeStruct((N_ROWS, D), F32),
                   jax.ShapeDtypeStruct((N_ROWS, 128), jnp.int32), jax.ShapeDtypeStruct((N_ROWS, 128), F32)],
        scratch_shapes=[pltpu.VMEM((TM, D), F32)],
        compiler_params=_cparams(("parallel",)),
        name="gmlp",
    )(h1, x, mod, w_in, b_in.reshape(1, 2 * D), vg.reshape(1, D), vb.reshape(1, D), w_s, b_s, w_out,
      ln_g.reshape(1, D), ln_b.reshape(1, D), w_r, b_r)


def _moe_kernel(kind_ref, chunk_ref, ux_ref, uo_ref, ue_ref, pref_ref, src_ref, h_hbm, wg_ref, wu_ref, wd_ref, y_ref,
                acc_ref, xg_ref, sem, xb_ref, act_ref, wgb, wub, wdb):
    u = pl.program_id(0)
    nf = EXPERT_DIM // MOE_TF
    kind = kind_ref[u]
    c = chunk_ref[u]
    tile = ux_ref[u]
    slot = tile % 2
    begins_tile = jnp.logical_or(kind == MOE_CACHED, jnp.logical_and(kind == MOE_STREAM, c == 0))

    def row_copy(t, i, dst_slot):
        row = src_ref[t * MOE_TM + i]
        return pltpu.make_async_copy(h_hbm.at[pl.ds(row, 1), :], xg_ref.at[dst_slot, pl.ds(i, 1), :], sem.at[dst_slot])

    @pl.when(u == 0)
    def _():
        def start_row(i, carry):
            row_copy(0, i, 0).start()
            return carry
        lax.fori_loop(0, MOE_TM, start_row, 0)

    def wait_rows(dst_slot):
        pltpu.make_async_copy(h_hbm.at[pl.ds(0, MOE_TM), :], xg_ref.at[dst_slot], sem.at[dst_slot]).wait()

    def start_next_tile():
        nxt = pref_ref[u]
        for i in range(MOE_TM):
            row_copy(nxt, i, 1 - slot).start()

    @pl.when(begins_tile)
    def _():
        wait_rows(slot)

    @pl.when(pref_ref[u] == MOE_DRAIN)
    def _():
        wait_rows(1 - slot)

    def swiglu_chunk(x, k):
        a = jnp.dot(x, wgb[k], preferred_element_type=F32)
        b = jnp.dot(x, wub[k], preferred_element_type=F32)
        return (jax.nn.silu(a) * b).astype(BF16)

    @pl.when(kind == MOE_STREAM)
    def _():
        rows_c = pl.ds(pl.multiple_of(c * MOE_TF, MOE_TF), MOE_TF)
        wgb[c] = wg_ref[...].astype(BF16)
        wub[c] = wu_ref[...].astype(BF16)
        wdb[rows_c, :] = wd_ref[...].astype(BF16)

        @pl.when(c == 0)
        def _():
            xb_ref[...] = xg_ref[slot].astype(BF16)
            start_next_tile()

        p = jnp.dot(swiglu_chunk(xb_ref[...], c), wdb[rows_c, :], preferred_element_type=F32)
        acc_ref[...] = jnp.where(c == 0, p, acc_ref[...] + p)

        @pl.when(c == nf - 1)
        def _():
            y_ref[...] = acc_ref[...]

    @pl.when(kind == MOE_CACHED)
    def _():
        x = xg_ref[slot].astype(BF16)
        start_next_tile()
        for k in range(nf):
            act_ref[:, k * MOE_TF:(k + 1) * MOE_TF] = swiglu_chunk(x, k)
        y_ref[...] = jnp.dot(act_ref[...], wdb[...], preferred_element_type=F32)

    @pl.when(kind == MOE_ZERO)
    def _():
        y_ref[...] = jnp.zeros_like(y_ref)


def _moe(layer, units, src, h, wg, wu, wd):
    nf = EXPERT_DIM // MOE_TF

    def w_spec(shape, index):
        return pl.BlockSpec(shape, lambda u, kind, chunk, ux, uo, ue, pref, src_: index(ue[u], chunk[u]))

    grid_spec = pltpu.PrefetchScalarGridSpec(
        num_scalar_prefetch=7,
        grid=(MOE_UNITS,),
        in_specs=[pl.BlockSpec(memory_space=pl.ANY),
                  w_spec((None, None, D, MOE_TF), lambda e, k: (layer, e, 0, k)),
                  w_spec((None, None, D, MOE_TF), lambda e, k: (layer, e, 0, k)),
                  w_spec((None, None, MOE_TF, D), lambda e, k: (layer, e, k, 0))],
        out_specs=pl.BlockSpec((MOE_TM, D), lambda u, kind, chunk, ux, uo, ue, pref, src_: (uo[u], 0)),
        scratch_shapes=[pltpu.VMEM((MOE_TM, D), F32), pltpu.VMEM((2, MOE_TM, D), F32),
                        pltpu.SemaphoreType.DMA((2,)), pltpu.VMEM((MOE_TM, D), BF16),
                        pltpu.VMEM((MOE_TM, EXPERT_DIM), BF16),
                        pltpu.VMEM((nf, D, MOE_TF), BF16), pltpu.VMEM((nf, D, MOE_TF), BF16),
                        pltpu.VMEM((EXPERT_DIM, D), BF16)],
    )
    return pl.pallas_call(
        _moe_kernel,
        grid_spec=grid_spec,
        out_shape=jax.ShapeDtypeStruct((MOE_ROWS, D), F32),
        compiler_params=_cparams(("arbitrary",)),
        name="moe",
    )(*units, src, h, wg, wu, wd)


def _take_rows(a, rows):
    return a.at[rows].get(mode="promise_in_bounds")


def _route(idx):
    e = idx.reshape(-1)
    onehot = (e[:, None] == jnp.arange(N_EXPERTS, dtype=jnp.int32)[None, :]).astype(jnp.int32)
    counts = jnp.sum(onehot, axis=0)
    rank = jnp.sum((jnp.cumsum(onehot, axis=0) - 1) * onehot, axis=1)
    padded = ((counts + MOE_TM - 1) // MOE_TM) * MOE_TM
    ends = jnp.cumsum(padded)
    starts = ends - padded
    pos = starts[e] + rank
    token = jnp.arange(N_ROWS * TOP_K, dtype=jnp.int32) // TOP_K
    src = jnp.zeros((MOE_ROWS,), jnp.int32).at[pos].set(token)
    n_used = (ends[-1] // MOE_TM).astype(jnp.int32)
    tile_start = jnp.minimum(jnp.arange(MOE_TILES, dtype=jnp.int32), n_used - 1) * MOE_TM
    tile_expert = jnp.sum((tile_start[:, None] >= ends[None, :]).astype(jnp.int32), axis=1)
    tile_expert = jnp.minimum(tile_expert, N_EXPERTS - 1).astype(jnp.int32)
    prev = jnp.concatenate([jnp.full((1,), -1, jnp.int32), tile_expert[:-1]])
    tile_first = tile_expert != prev
    nf = EXPERT_DIM // MOE_TF
    tiles = jnp.arange(MOE_TILES, dtype=jnp.int32)
    live = tiles < n_used
    cost = jnp.where(live, jnp.where(tile_first, nf, 1), 0).astype(jnp.int32)
    off = jnp.cumsum(cost) - cost
    n_units = jnp.sum(cost)
    u = jnp.arange(MOE_UNITS, dtype=jnp.int32)
    tile_u = jnp.sum(jnp.logical_and(off[None, :] <= u[:, None], live[None, :]).astype(jnp.int32), axis=1) - 1
    first_u = tile_first[tile_u]
    work = u < n_units
    spare = u - n_units
    unit_kind = jnp.where(work, jnp.where(first_u, MOE_STREAM, MOE_CACHED),
                          jnp.where(spare < MOE_TILES - n_used, MOE_ZERO, MOE_IDLE)).astype(jnp.int32)
    unit_chunk = jnp.where(jnp.logical_and(work, first_u), u - off[tile_u], nf - 1).astype(jnp.int32)
    unit_out = jnp.where(work, tile_u, jnp.minimum(n_used + spare, MOE_TILES - 1)).astype(jnp.int32)
    unit_x = tile_u.astype(jnp.int32)
    unit_expert = tile_expert[tile_u]
    unit_pref = jnp.where(work, jnp.minimum(tile_u + 1, n_used - 1),
                          jnp.where(spare == 0, MOE_DRAIN, -1)).astype(jnp.int32)
    return src, pos.reshape(N_ROWS, TOP_K), (unit_kind, unit_chunk, unit_x, unit_out, unit_expert, unit_pref)


def _combine_kernel(ya_ref, yb_ref, p_ref, x1_ref, mod_ref, g_ref, b_ref, x2_ref):
    g = p_ref[:, 0:1] * ya_ref[...] + p_ref[:, 1:2] * yb_ref[...]
    _finish(x1_ref[...], g, mod_ref, g_ref, b_ref, x2_ref)


def _combine(ya, yb, p, x1, mod, ln_g, ln_b):
    return pl.pallas_call(
        _combine_kernel,
        grid=(N_ROWS // TM,),
        in_specs=[_row_spec(TM), _row_spec(TM), _row_spec(TM, 128), _row_spec(TM), _mod_spec(TM),
                  _const_spec((1, D)), _const_spec((1, D))],
        out_specs=_row_spec(TM),
        out_shape=jax.ShapeDtypeStruct((N_ROWS, D), F32),
        compiler_params=_cparams(("parallel",)),
        name="combine",
    )(ya, yb, p, x1, mod, ln_g.reshape(1, D), ln_b.reshape(1, D))


def _grid_pos_embed():
    rows = DEC_SEQ // GRID_W
    r, col = jnp.meshgrid(jnp.arange(rows, dtype=F32), jnp.arange(GRID_W, dtype=F32), indexing="ij")
    quarter = D // 4
    freq = 1.0 / (10000.0 ** (jnp.arange(quarter, dtype=F32) / quarter))

    def emb(p):
        ang = p.reshape(-1)[:, None] * freq[None, :]
        return jnp.concatenate([jnp.sin(ang), jnp.cos(ang)], axis=-1)

    return jnp.concatenate([emb(r), emb(col)], axis=-1)


def kernel(x_prompt, x_sample, c, state_ssm_re, state_ssm_im, c_ctx, w_ada, b_ada, ln_g, ln_b, s5_a_re, s5_a_im, s5_log_step, s5_b_re, s5_b_im, s5_c_re, s5_c_im, s5_d, s5_w_glu, gm_w_in, gm_b_in, gm_ln_g, gm_ln_b, gm_w_s, gm_b_s, gm_w_out, ffn_w_gate, ffn_w_up, ffn_w_down, moe_w_router, moe_b_router, moe_w_gate, moe_w_up, moe_w_down):
    cond8 = jnp.concatenate([c_ctx[None, :], c, jnp.zeros((N_COND - 1 - DEC_BATCH, D), F32)], axis=0)
    mods = _ada(cond8, w_ada, b_ada).reshape(DEPTH, N_COND, 6, D)

    x = _prep(x_prompt, x_sample, _grid_pos_embed())
    h = None

    fin_re, fin_im = [], []
    for i in range(DEPTH):
        j = i // 2
        mod = mods[i]
        nmod = mods[min(i + 1, DEPTH - 1)]
        if i % 2 == 0:
            params = _s5_params(s5_a_re[j], s5_a_im[j], s5_log_step[j], s5_b_re[j], s5_b_im[j], s5_c_re[j], s5_c_im[j])
            ysum, fin = _s5(x, mod, params, _s5_state_in(state_ssm_re[:, j], state_ssm_im[:, j]))
            fr, fi = _s5_state_out(fin)
            fin_re.append(fr)
            fin_im.append(fi)
            x1, h2 = _glu(ysum, x, mod, s5_d[j], s5_w_glu[j].astype(BF16), ln_g[i, 0], ln_b[i, 0])
            x, h = _ffn(h2, x1, mod, nmod, ffn_w_gate[j].astype(BF16), ffn_w_up[j].astype(BF16),
                        ffn_w_down[j].astype(BF16), ln_g[i, 1], ln_b[i, 1])
        else:
            b_s = jnp.broadcast_to(gm_b_s[j][:, :, None], (GM_HEADS, GM_CHUNK, GM_CHUNK)).astype(F32)
            w_r = jnp.pad(moe_w_router[j].astype(F32), ((0, 0), (0, 128 - N_EXPERTS)))
            w_r_hi = w_r.astype(BF16)
            w_r = jnp.stack([w_r_hi, (w_r - w_r_hi.astype(F32)).astype(BF16)], axis=0)
            b_r = jnp.pad(moe_b_router[j].astype(F32), (0, 128 - N_EXPERTS)).reshape(1, 128)
            x1, h2, idx, p = _gmlp(h, x, mod, gm_w_in[j].astype(BF16), gm_b_in[j], gm_ln_g[j], gm_ln_b[j],
                                   gm_w_s[j].astype(BF16), b_s, gm_w_out[j].astype(BF16), ln_g[i, 0], ln_b[i, 0], w_r, b_r)
            src, pos, units = _route(idx[:, :TOP_K])
            ys = _moe(j, units, src, h2, moe_w_gate, moe_w_up, moe_w_down)
            x = _combine(_take_rows(ys, pos[:, 0]), _take_rows(ys, pos[:, 1]), p, x1, mod, ln_g[i, 1], ln_b[i, 1])

    y_prompt = x[:N_P].reshape(BATCH, SEQ, D)
    y_sample = x[N_P:].reshape(DEC_BATCH, DEC_SEQ, D)
    new_re = jnp.stack(fin_re, axis=1).astype(x_prompt.dtype)
    new_im = jnp.stack(fin_im, axis=1).astype(x_prompt.dtype)
    return (y_prompt, y_sample, new_re, new_im)
```

```python
import jax
import jax.numpy as jnp
from jax import lax
from jax.experimental import pallas as pl
from jax.experimental.pallas import tpu as pltpu

F32 = jnp.float32
BF16 = jnp.bfloat16

D = 1024
BATCH, SEQ = 32, 256
DEC_BATCH, DEC_SEQ = 4, 1024
DEPTH = 4
GRID_W = 64
N_P = BATCH * SEQ
N_S = DEC_BATCH * DEC_SEQ
N_ROWS = N_P + N_S
N_COND = 8

S5_CG, S5_P, S5_G = 16, 64, 64
S5_T = 16
S5_Q = S5_G // 2
P_CHUNKS = SEQ // S5_T
S_CHUNKS = DEC_SEQ // S5_T
S5_ROWS_P = P_CHUNKS * BATCH
S5_ROWS_S = S_CHUNKS * DEC_BATCH
S5_ROWS = S5_ROWS_P + S5_ROWS_S

GM_CHUNK, GM_HEADS = 128, 8
FFN_DIM = 2816
N_EXPERTS, TOP_K, EXPERT_DIM = 8, 2, 3584
ALPHA = (2.0 * DEPTH) ** 0.25
LN_EPS = 1e-5

TM = 512
ROW_SUB = TM
MOE_TM = 512
MOE_TF = 512
MOE_TILES = (N_ROWS * TOP_K) // MOE_TM + N_EXPERTS
MOE_ROWS = MOE_TILES * MOE_TM
MOE_UNITS = N_EXPERTS * (EXPERT_DIM // MOE_TF) + MOE_TILES
MOE_STREAM, MOE_CACHED, MOE_ZERO, MOE_IDLE = 1, 2, 0, 3
MOE_DRAIN = -2
VMEM_LIMIT = 56 * 1024 * 1024


def _cparams(sem):
    return pltpu.CompilerParams(dimension_semantics=sem, vmem_limit_bytes=VMEM_LIMIT)


def _cond_of_tile(t, tm):
    row0 = t * tm
    return jnp.where(row0 < N_P, 0, 1 + (row0 - N_P) // DEC_SEQ)


def _mod_spec(tm):
    return pl.BlockSpec((None, 6, D), lambda t: (_cond_of_tile(t, tm), 0, 0))


def _row_spec(tm, width=D):
    return pl.BlockSpec((tm, width), lambda t: (t, 0))


def _const_spec(shape):
    return pl.BlockSpec(shape, lambda t: (0,) * len(shape))


def _layer_norm(r, g, b):
    mu = jnp.mean(r, axis=-1, keepdims=True)
    c = r - mu
    var = jnp.mean(c * c, axis=-1, keepdims=True)
    return c * lax.rsqrt(var + LN_EPS) * g + b


def _modulate(x, mod_ref, shift_row, scale_row):
    return x * (1.0 + mod_ref[scale_row:scale_row + 1, :]) + mod_ref[shift_row:shift_row + 1, :]


def _ada_kernel(c_ref, w_ref, b_ref, o_ref):
    c = c_ref[...]
    o_ref[...] = jnp.dot(jax.nn.silu(c), w_ref[...], preferred_element_type=F32) + b_ref[...]


def _ada(cond8, w_ada, b_ada):
    tn = 1536
    return pl.pallas_call(
        _ada_kernel,
        grid=(DEPTH, 6 * D // tn),
        in_specs=[pl.BlockSpec((N_COND, D), lambda i, n: (0, 0)),
                  pl.BlockSpec((None, D, tn), lambda i, n: (i, 0, n)),
                  pl.BlockSpec((None, 1, tn), lambda i, n: (i, 0, n))],
        out_specs=pl.BlockSpec((None, N_COND, tn), lambda i, n: (i, 0, n)),
        out_shape=jax.ShapeDtypeStruct((DEPTH, N_COND, 6 * D), F32),
        compiler_params=_cparams(("arbitrary", "arbitrary")),
        name="ada",
    )(cond8, w_ada, b_ada.reshape(DEPTH, 1, 6 * D))


def _prep_kernel(xp_ref, xs_ref, pos_ref, xo_ref):
    is_latent = pl.program_id(0) * TM >= N_P
    xo_ref[...] = jnp.where(is_latent, xs_ref[...] + pos_ref[...], xp_ref[...])


def _prep(x_prompt, x_sample, pos):
    p_tiles = N_P // TM

    def latent_tile(t):
        return jnp.maximum(t - p_tiles, 0)

    return pl.pallas_call(
        _prep_kernel,
        grid=(N_ROWS // TM,),
        in_specs=[pl.BlockSpec((TM, D), lambda t: (jnp.minimum(t, p_tiles - 1), 0)),
                  pl.BlockSpec((TM, D), lambda t: (latent_tile(t), 0)),
                  pl.BlockSpec((TM, D), lambda t: (latent_tile(t) % (DEC_SEQ // TM), 0))],
        out_specs=_row_spec(TM),
        out_shape=jax.ShapeDtypeStruct((N_ROWS, D), F32),
        compiler_params=_cparams(("parallel",)),
        name="prep",
    )(x_prompt.reshape(N_P, D), x_sample.reshape(N_S, D), pos)


S5_BLK = 4
S5_W = 2 * S5_T * S5_CG
S5_GR = 2 * S5_CG
_NT = (((1,), (1,)), ((), ()))


def _granule_transpose4(a, lane):
    low = lane < 64
    a0 = jnp.where(low, a[0], pltpu.roll(a[2], 64, 1))
    a2 = jnp.where(low, pltpu.roll(a[0], 64, 1), a[2])
    a1 = jnp.where(low, a[1], pltpu.roll(a[3], 64, 1))
    a3 = jnp.where(low, pltpu.roll(a[1], 64, 1), a[3])
    even = (lane // S5_GR) % 2 == 0
    return [jnp.where(even, a0, pltpu.roll(a1, 32, 1)), jnp.where(even, pltpu.roll(a0, 96, 1), a1),
            jnp.where(even, a2, pltpu.roll(a3, 32, 1)), jnp.where(even, pltpu.roll(a2, 96, 1), a3)]


def _s5_chunk_matrices(p4, bb_ref, c_ref, pw_ref, min_scr, mi_scr, p_scr):
    half = 2 * S5_P
    gran = lax.broadcasted_iota(jnp.int32, (S5_W, 128), 1) // S5_GR
    rep = (lax.broadcasted_iota(jnp.int32, (S5_GR, S5_W), 1) % S5_GR
           == lax.broadcasted_iota(jnp.int32, (S5_GR, S5_W), 0)).astype(BF16)
    mout = []
    for k in range(2):
        bb_re, bb_im = bb_ref[p4, k, 0], bb_ref[p4, k, 1]
        c_re, c_im = c_ref[p4, k, 0], c_ref[p4, k, 1]
        out_blocks = []
        for r in range(S5_T):
            t_in = S5_T - 1 - r if k == 0 else r
            t_out = r + 1 if k == 0 else S5_T - r
            rows = slice(S5_GR * r, S5_GR * (r + 1))
            pr, pi = pw_ref[p4, k, 0, t_in:t_in + 1, :], pw_ref[p4, k, 1, t_in:t_in + 1, :]
            min_scr[k, rows, 0:half] = bb_re * pr - bb_im * pi
            min_scr[k, rows, half:2 * half] = bb_re * pi + bb_im * pr
            pr, pi = pw_ref[p4, k, 0, t_out:t_out + 1, :], pw_ref[p4, k, 1, t_out:t_out + 1, :]
            out_blocks.append(jnp.concatenate([c_re * pr - c_im * pi, -(c_re * pi + c_im * pr)], axis=-1))
        mout.append(jnp.concatenate(out_blocks, axis=0).astype(BF16))
        c2 = jnp.concatenate([c_re, -c_im], axis=-1)
        c2_hi = c2.astype(BF16)
        c2_lo = (c2 - c2_hi.astype(F32)).astype(BF16)
        m = min_scr[k]
        m_hi = m.astype(BF16)
        m_lo = (m - m_hi.astype(F32)).astype(BF16)
        z = (lax.dot_general(m_hi, c2_hi, _NT, preferred_element_type=F32)
             + lax.dot_general(m_lo, c2_hi, _NT, preferred_element_type=F32)
             + lax.dot_general(m_hi, c2_lo, _NT, preferred_element_type=F32))
        zrep = jnp.dot(z.astype(BF16), rep, preferred_element_type=F32)
        zeros = jnp.zeros((S5_W, S5_W), F32)
        p_scr[0:S5_W, :] = zrep if k == 0 else zeros
        p_scr[S5_W:2 * S5_W, :] = zeros if k == 0 else zrep
        for j in range(S5_W // 128):
            cols = slice(128 * j, 128 * (j + 1))
            acc = None
            for s4 in range(4):
                s = 4 * j + s4
                start = S5_GR * (S5_T - 1 - s) if k == 0 else S5_W - S5_GR * s
                win = p_scr[start:start + S5_W, cols]
                acc = win if acc is None else jnp.where(gran == s4, win, acc)
            mi_scr[k, :, cols] = acc.astype(BF16)
    return mout


def _s5_scan(k, s_scrs, h_scrs, row0, nb, nchunks, hre, him, ar, ai):
    for n in (range(nchunks - 1, -1, -1) if k == 1 else range(nchunks)):
        rows = pl.ds(row0 + n, nb, stride=nchunks)
        h_scrs[0][rows, :] = hre
        h_scrs[1][rows, :] = him
        sre = s_scrs[0][rows, :]
        sim = s_scrs[1][rows, :]
        hre, him = ar * hre - ai * him + sre, ar * him + ai * hre + sim
    return hre, him


def _s5_kernel(x_ref, mod_ref, bb_ref, c_ref, pw_ref, h0_ref, y_ref, fin_ref,
               u_scr, yp_scr, sre_scr, sim_scr, hre_scr, him_scr, min_scr, mi_scr, p_scr):
    half = 2 * S5_P
    lane = lax.broadcasted_iota(jnp.int32, (S5_ROWS, 128), 1)
    scale = jnp.concatenate(
        [jnp.broadcast_to(1.0 + mod_ref[0, 1:2, :], (S5_ROWS_P, 128))]
        + [jnp.broadcast_to(1.0 + mod_ref[1 + b, 1:2, :], (S_CHUNKS, 128)) for b in range(DEC_BATCH)], axis=0)
    shift = jnp.concatenate(
        [jnp.broadcast_to(mod_ref[0, 0:1, :], (S5_ROWS_P, 128))]
        + [jnp.broadcast_to(mod_ref[1 + b, 0:1, :], (S_CHUNKS, 128)) for b in range(DEC_BATCH)], axis=0)
    for j in range(S5_T // 4):
        a = [x_ref[pl.ds(4 * j + s4, S5_ROWS, stride=S5_T), :] * scale + shift for s4 in range(4)]
        b = _granule_transpose4(a, lane)
        for p4 in range(S5_BLK):
            u_scr[p4, :, 128 * j:128 * (j + 1)] = b[p4].astype(BF16)

    for p4 in range(S5_BLK):
        mout = _s5_chunk_matrices(p4, bb_ref, c_ref, pw_ref, min_scr, mi_scr, p_scr)
        u = u_scr[p4]
        u0, u1 = u[:, 0:256], u[:, 256:512]
        y = None
        for k in range(2):
            s = jnp.dot(u, min_scr[k].astype(BF16), preferred_element_type=F32)
            sre_scr[...] = s[:, 0:half]
            sim_scr[...] = s[:, half:2 * half]
            s_scrs, h_scrs = (sre_scr, sim_scr), (hre_scr, him_scr)
            ar_p = jnp.broadcast_to(pw_ref[p4, k, 0, S5_T:S5_T + 1, :], (BATCH, half))
            ai_p = jnp.broadcast_to(pw_ref[p4, k, 1, S5_T:S5_T + 1, :], (BATCH, half))
            zero = jnp.zeros((BATCH, half), F32)
            fre, fim = _s5_scan(k, s_scrs, h_scrs, 0, BATCH, P_CHUNKS, zero, zero, ar_p, ai_p)
            fin_ref[p4, k, :, 0:half] = fre
            fin_ref[p4, k, :, half:2 * half] = fim
            _s5_scan(k, s_scrs, h_scrs, S5_ROWS_P, DEC_BATCH, S_CHUNKS,
                     h0_ref[p4, k, :, 0:half], h0_ref[p4, k, :, half:2 * half], ar_p[:DEC_BATCH], ai_p[:DEC_BATCH])
            hin = jnp.concatenate([hre_scr[...], him_scr[...]], axis=-1).astype(BF16)
            readout = lax.dot_general(hin, mout[k], _NT, preferred_element_type=F32)
            m00 = jnp.dot(u0, mi_scr[k, 0:256, 0:256], preferred_element_type=F32)
            m11 = jnp.dot(u1, mi_scr[k, 256:512, 256:512], preferred_element_type=F32)
            if k == 0:
                m11 = m11 + jnp.dot(u0, mi_scr[k, 0:256, 256:512], preferred_element_type=F32)
            else:
                m00 = m00 + jnp.dot(u1, mi_scr[k, 256:512, 0:256], preferred_element_type=F32)
            yk = jnp.concatenate([m00, m11], axis=-1) + readout
            y = yk if y is None else y + yk
        yp_scr[p4] = y

    for j in range(S5_T // 4):
        b = [yp_scr[p4, :, 128 * j:128 * (j + 1)] for p4 in range(S5_BLK)]
        a = _granule_transpose4(b, lane)
        for s4 in range(4):
            y_ref[pl.ds(4 * j + s4, S5_ROWS, stride=S5_T), :] = a[s4]


def _s5(x, mods, params, h0):
    bb, cc, pw = params
    nblk = S5_Q // S5_BLK
    return pl.pallas_call(
        _s5_kernel,
        grid=(nblk,),
        in_specs=[pl.BlockSpec((N_ROWS, 128), lambda q: (0, q)),
                  pl.BlockSpec((N_COND, 6, 128), lambda q: (0, 0, q)),
                  pl.BlockSpec((S5_BLK, 2, 2, S5_GR, 128), lambda q: (q, 0, 0, 0, 0)),
                  pl.BlockSpec((S5_BLK, 2, 2, S5_GR, 128), lambda q: (q, 0, 0, 0, 0)),
                  pl.BlockSpec((S5_BLK, 2, 2, S5_T + 1, 128), lambda q: (q, 0, 0, 0, 0)),
                  pl.BlockSpec((S5_BLK, 2, DEC_BATCH, 256), lambda q: (q, 0, 0, 0))],
        out_specs=[pl.BlockSpec((N_ROWS, 128), lambda q: (0, q)),
                   pl.BlockSpec((S5_BLK, 2, BATCH, 256), lambda q: (q, 0, 0, 0))],
        out_shape=[jax.ShapeDtypeStruct((N_ROWS, D), F32),
                   jax.ShapeDtypeStruct((S5_Q, 2, BATCH, 256), F32)],
        scratch_shapes=[pltpu.VMEM((S5_BLK, S5_ROWS, S5_W), BF16), pltpu.VMEM((S5_BLK, S5_ROWS, S5_W), F32),
                        pltpu.VMEM((S5_ROWS, 128), F32), pltpu.VMEM((S5_ROWS, 128), F32),
                        pltpu.VMEM((S5_ROWS, 128), F32), pltpu.VMEM((S5_ROWS, 128), F32),
                        pltpu.VMEM((2, S5_W, 256), F32),
                        pltpu.VMEM((2, S5_W, S5_W), BF16), pltpu.VMEM((2 * S5_W, S5_W), F32)],
        compiler_params=_cparams(("parallel",)),
        name="s5",
    )(x, mods, bb, cc, pw, h0)


def _s5_params(a_re, a_im, log_step, b_re, b_im, c_re, c_im):
    lam_re = jnp.minimum(a_re.astype(F32), -1e-4)
    lam_im = a_im.astype(F32)
    dt = jnp.exp(log_step.astype(F32))[..., None]
    j = jnp.arange(S5_T + 1, dtype=F32)[:, None, None, None]
    mag = jnp.exp(lam_re * dt * j)
    pw_re = mag * jnp.cos(lam_im * dt * j)
    pw_im = mag * jnp.sin(lam_im * dt * j)
    nr, ni = pw_re[1] - 1.0, pw_im[1]
    den = lam_re * lam_re + lam_im * lam_im
    q_re = (nr * lam_re + ni * lam_im) / den
    q_im = (ni * lam_re - nr * lam_im) / den
    b_re, b_im = b_re.astype(F32), b_im.astype(F32)
    bb_re = q_re[..., None] * b_re - q_im[..., None] * b_im
    bb_im = q_re[..., None] * b_im + q_im[..., None] * b_re
    eye = jnp.eye(2, dtype=F32)

    def pair_rows(m, spec):
        m = m.reshape(2, S5_Q, 2, *m.shape[2:])
        return jnp.einsum(spec, m, eye).reshape(S5_Q, 2, S5_GR, 2 * S5_P)

    bb = jnp.stack([pair_rows(bb_re, "kqgpc,gh->qkgchp"), pair_rows(bb_im, "kqgpc,gh->qkgchp")], axis=2)
    cc = jnp.stack([pair_rows(c_re.astype(F32), "kqgcp,gh->qkgchp"),
                    pair_rows(c_im.astype(F32), "kqgcp,gh->qkgchp")], axis=2)

    def pw_lay(p):
        return p.reshape(S5_T + 1, 2, S5_Q, 2 * S5_P).transpose(2, 1, 0, 3)
    pw = jnp.stack([pw_lay(pw_re), pw_lay(pw_im)], axis=2)
    return bb, cc, pw


def _s5_state_in(st_re, st_im):
    def lay(s):
        return s.astype(F32).reshape(DEC_BATCH, 2, S5_Q, 2 * S5_P).transpose(2, 1, 0, 3)
    return jnp.concatenate([lay(st_re), lay(st_im)], axis=-1)


def _s5_state_out(fin):
    def lay(s):
        return s.transpose(2, 1, 0, 3).reshape(BATCH, 2, S5_G, S5_P)
    return lay(fin[..., :2 * S5_P]), lay(fin[..., 2 * S5_P:])


def _glu_kernel(ys_ref, x_ref, mod_ref, d_ref, w_ref, g_ref, b_ref, x1_ref, h2_ref):
    for r in range(TM // ROW_SUB):
        rows = slice(r * ROW_SUB, (r + 1) * ROW_SUB)
        x = x_ref[rows, :]
        h = _modulate(x, mod_ref, 0, 1)
        y = jax.nn.gelu(ys_ref[rows, :] + d_ref[...] * h).astype(BF16)
        z = jnp.dot(y, w_ref[...], preferred_element_type=F32)
        f = z[:, :D] * jax.nn.sigmoid(z[:, D:])
        x1 = _layer_norm(ALPHA * x + (1.0 + mod_ref[2:3, :]) * f, g_ref[...], b_ref[...])
        x1_ref[rows, :] = x1
        h2_ref[rows, :] = _modulate(x1, mod_ref, 3, 4).astype(BF16)


def _glu(ysum, x, mod, d, w_glu, ln_g, ln_b):
    return pl.pallas_call(
        _glu_kernel,
        grid=(N_ROWS // TM,),
        in_specs=[_row_spec(TM), _row_spec(TM), _mod_spec(TM), _const_spec((1, D)),
                  _const_spec((D, 2 * D)), _const_spec((1, D)), _const_spec((1, D))],
        out_specs=[_row_spec(TM), _row_spec(TM)],
        out_shape=[jax.ShapeDtypeStruct((N_ROWS, D), F32), jax.ShapeDtypeStruct((N_ROWS, D), BF16)],
        compiler_params=_cparams(("parallel",)),
        name="glu",
    )(ysum, x, mod, d.reshape(1, D), w_glu, ln_g.reshape(1, D), ln_b.reshape(1, D))


def _finish(x1, g, mod_ref, g_ref, b_ref, x2_ref):
    x2 = _layer_norm(ALPHA * x1 + (1.0 + mod_ref[5:6, :]) * g, g_ref[...], b_ref[...])
    x2_ref[...] = x2
    return x2


FFN_TF = FFN_DIM // 2


def _ffn_kernel(h_ref, x1_ref, mod_ref, nmod_ref, wg_ref, wu_ref, wd_ref, g_ref, b_ref, x2_ref, hn_ref):
    h = h_ref[...]
    acc = None
    for c in range(FFN_DIM // FFN_TF):
        sl = slice(c * FFN_TF, (c + 1) * FFN_TF)
        a = jnp.dot(h, wg_ref[:, sl], preferred_element_type=F32)
        b = jnp.dot(h, wu_ref[:, sl], preferred_element_type=F32)
        p = jnp.dot((jax.nn.silu(a) * b).astype(BF16), wd_ref[sl, :], preferred_element_type=F32)
        acc = p if acc is None else acc + p
    x2 = _finish(x1_ref[...], acc, mod_ref, g_ref, b_ref, x2_ref)
    hn_ref[...] = _modulate(x2, nmod_ref, 0, 1).astype(BF16)


def _ffn(h2, x1, mod, nmod, wg, wu, wd, ln_g, ln_b):
    single = pl.Buffered(1)
    return pl.pallas_call(
        _ffn_kernel,
        grid=(N_ROWS // TM,),
        in_specs=[_row_spec(TM), _row_spec(TM), _mod_spec(TM), _mod_spec(TM),
                  pl.BlockSpec((D, FFN_DIM), lambda t: (0, 0), pipeline_mode=single),
                  pl.BlockSpec((D, FFN_DIM), lambda t: (0, 0), pipeline_mode=single),
                  pl.BlockSpec((FFN_DIM, D), lambda t: (0, 0), pipeline_mode=single),
                  _const_spec((1, D)), _const_spec((1, D))],
        out_specs=[_row_spec(TM), _row_spec(TM)],
        out_shape=[jax.ShapeDtypeStruct((N_ROWS, D), F32), jax.ShapeDtypeStruct((N_ROWS, D), BF16)],
        compiler_params=_cparams(("parallel",)),
        name="ffn",
    )(h2, x1, mod, nmod, wg, wu, wd, ln_g.reshape(1, D), ln_b.reshape(1, D))


def _gmlp_kernel(h_ref, x_ref, mod_ref, win_ref, bin_ref, vg_ref, vb_ref, ws_ref, bs_ref, wout_ref,
                 g_ref, b_ref, wr_ref, br_ref, x1_ref, h2_ref, idx_ref, p_ref, s_scr):
    for r in range(TM // ROW_SUB):
        rows = slice(r * ROW_SUB, (r + 1) * ROW_SUB)
        z = jax.nn.gelu(jnp.dot(h_ref[rows, :], win_ref[...], preferred_element_type=F32) + bin_ref[...])
        u = z[:, :D]
        v = _layer_norm(z[:, D:], vg_ref[...], vb_ref[...]).astype(BF16)
        n_chunks = ROW_SUB // GM_CHUNK
        for hd in range(GM_HEADS):
            cols = slice(hd * GM_CHUNK, (hd + 1) * GM_CHUNK)
            rhs = jnp.concatenate([v[n * GM_CHUNK:(n + 1) * GM_CHUNK, cols] for n in range(n_chunks)], axis=-1)
            s = jnp.dot(ws_ref[hd], rhs, preferred_element_type=F32)
            for n in range(n_chunks):
                lo = r * ROW_SUB + n * GM_CHUNK
                s_scr[lo:lo + GM_CHUNK, cols] = s[:, n * GM_CHUNK:(n + 1) * GM_CHUNK] + bs_ref[hd]
        f = jnp.dot((u * s_scr[rows, :]).astype(BF16), wout_ref[...], preferred_element_type=F32)
        x1 = _layer_norm(ALPHA * x_ref[rows, :] + (1.0 + mod_ref[2:3, :]) * f, g_ref[...], b_ref[...])
        x1_ref[rows, :] = x1
        h2 = _modulate(x1, mod_ref, 3, 4)
        h2_ref[rows, :] = h2
        h_hi = h2.astype(BF16)
        h_lo = (h2 - h_hi.astype(F32)).astype(BF16)
        w_hi, w_lo = wr_ref[0], wr_ref[1]
        logits = (jnp.dot(h_hi, w_hi, preferred_element_type=F32) + jnp.dot(h_lo, w_hi, preferred_element_type=F32)
                  + jnp.dot(h_hi, w_lo, preferred_element_type=F32)) + br_ref[...]
        lane = lax.broadcasted_iota(jnp.int32, logits.shape, 1)
        neg = jnp.float32(-jnp.inf)
        logits = jnp.where(lane < N_EXPERTS, logits, neg)
        v1 = jnp.max(logits, axis=-1, keepdims=True)
        i1 = jnp.min(jnp.where(logits == v1, lane, 128), axis=-1, keepdims=True)
        rest = jnp.where(lane == i1, neg, logits)
        v2 = jnp.max(rest, axis=-1, keepdims=True)
        i2 = jnp.min(jnp.where(rest == v2, lane, 128), axis=-1, keepdims=True)
        e2 = jnp.exp(v2 - v1)
        p1 = 1.0 / (1.0 + e2)
        p2 = e2 / (1.0 + e2)
        idx_ref[rows, :] = jnp.where(lane == 0, i1, jnp.where(lane == 1, i2, 0))
        p_ref[rows, :] = jnp.where(lane == 0, p1, jnp.where(lane == 1, p2, 0.0))


def _gmlp(h1, x, mod, w_in, b_in, vg, vb, w_s, b_s, w_out, ln_g, ln_b, w_r, b_r):
    return pl.pallas_call(
        _gmlp_kernel,
        grid=(N_ROWS // TM,),
        in_specs=[_row_spec(TM), _row_spec(TM), _mod_spec(TM),
                  _const_spec((D, 2 * D)), _const_spec((1, 2 * D)), _const_spec((1, D)), _const_spec((1, D)),
                  _const_spec((GM_HEADS, GM_CHUNK, GM_CHUNK)), _const_spec((GM_HEADS, GM_CHUNK, GM_CHUNK)),
                  _const_spec((D, D)), _const_spec((1, D)), _const_spec((1, D)),
                  _const_spec((2, D, 128)), _const_spec((1, 128))],
        out_specs=[_row_spec(TM), _row_spec(TM), _row_spec(TM, 128), _row_spec(TM, 128)],
        out_shape=[jax.ShapeDtypeStruct((N_ROWS, D), F32), jax.ShapeDtypeStruct((N_ROWS, D), F32),
                   jax.ShapeDtypeStruct((N_ROWS, 128), jnp.int32), jax.ShapeDtypeStruct((N_ROWS, 128), F32)],
        scratch_shapes=[pltpu.VMEM((TM, D), F32)],
        compiler_params=_cparams(("parallel",)),
        name="gmlp",
    )(h1, x, mod, w_in, b_in.reshape(1, 2 * D), vg.reshape(1, D), vb.reshape(1, D), w_s, b_s, w_out,
      ln_g.reshape(1, D), ln_b.reshape(1, D), w_r, b_r)


def _moe_kernel(kind_ref, chunk_ref, ux_ref, uo_ref, ue_ref, pref_ref, src_ref, h_hbm, wg_ref, wu_ref, wd_ref, y_ref,
                acc_ref, xg_ref, sem, xb_ref, act_ref, wgb, wub, wdb):
    u = pl.program_id(0)
    nf = EXPERT_DIM // MOE_TF
    kind = kind_ref[u]
    c = chunk_ref[u]
    tile = ux_ref[u]
    slot = tile % 2
    begins_tile = jnp.logical_or(kind == MOE_CACHED, jnp.logical_and(kind == MOE_STREAM, c == 0))

    def row_copy(t, i, dst_slot):
        row = src_ref[t * MOE_TM + i]
        return pltpu.make_async_copy(h_hbm.at[pl.ds(row, 1), :], xg_ref.at[dst_slot, pl.ds(i, 1), :], sem.at[dst_slot])

    @pl.when(u == 0)
    def _():
        def start_row(i, carry):
            row_copy(0, i, 0).start()
            return carry
        lax.fori_loop(0, MOE_TM, start_row, 0)

    def wait_rows(dst_slot):
        pltpu.make_async_copy(h_hbm.at[pl.ds(0, MOE_TM), :], xg_ref.at[dst_slot], sem.at[dst_slot]).wait()

    def start_next_tile():
        nxt = pref_ref[u]
        for i in range(MOE_TM):
            row_copy(nxt, i, 1 - slot).start(priority=i % 2)

    @pl.when(begins_tile)
    def _():
        wait_rows(slot)

    @pl.when(pref_ref[u] == MOE_DRAIN)
    def _():
        wait_rows(1 - slot)

    def swiglu_chunk(x, k):
        a = jnp.dot(x, wgb[k], preferred_element_type=F32)
        b = jnp.dot(x, wub[k], preferred_element_type=F32)
        return (jax.nn.silu(a) * b).astype(BF16)

    @pl.when(kind == MOE_STREAM)
    def _():
        rows_c = pl.ds(pl.multiple_of(c * MOE_TF, MOE_TF), MOE_TF)
        wgb[c] = wg_ref[...].astype(BF16)
        wub[c] = wu_ref[...].astype(BF16)
        wdb[rows_c, :] = wd_ref[...].astype(BF16)

        @pl.when(c == 0)
        def _():
            xb_ref[...] = xg_ref[slot].astype(BF16)
            start_next_tile()

        p = jnp.dot(swiglu_chunk(xb_ref[...], c), wdb[rows_c, :], preferred_element_type=F32)
        acc_ref[...] = jnp.where(c == 0, p, acc_ref[...] + p)

        @pl.when(c == nf - 1)
        def _():
            y_ref[...] = acc_ref[...]

    @pl.when(kind == MOE_CACHED)
    def _():
        x = xg_ref[slot].astype(BF16)
        start_next_tile()
        for k in range(nf):
            act_ref[:, k * MOE_TF:(k + 1) * MOE_TF] = swiglu_chunk(x, k)
        y_ref[...] = jnp.dot(act_ref[...], wdb[...], preferred_element_type=F32)

    @pl.when(kind == MOE_ZERO)
    def _():
        y_ref[...] = jnp.zeros_like(y_ref)


def _moe(layer, units, src, h, wg, wu, wd):
    nf = EXPERT_DIM // MOE_TF

    def w_spec(shape, index):
        return pl.BlockSpec(shape, lambda u, kind, chunk, ux, uo, ue, pref, src_: index(ue[u], chunk[u]))

    grid_spec = pltpu.PrefetchScalarGridSpec(
        num_scalar_prefetch=7,
        grid=(MOE_UNITS,),
        in_specs=[pl.BlockSpec(memory_space=pl.ANY),
                  w_spec((None, None, D, MOE_TF), lambda e, k: (layer, e, 0, k)),
                  w_spec((None, None, D, MOE_TF), lambda e, k: (layer, e, 0, k)),
                  w_spec((None, None, MOE_TF, D), lambda e, k: (layer, e, k, 0))],
        out_specs=pl.BlockSpec((MOE_TM, D), lambda u, kind, chunk, ux, uo, ue, pref, src_: (uo[u], 0)),
        scratch_shapes=[pltpu.VMEM((MOE_TM, D), F32), pltpu.VMEM((2, MOE_TM, D), F32),
                        pltpu.SemaphoreType.DMA((2,)), pltpu.VMEM((MOE_TM, D), BF16),
                        pltpu.VMEM((MOE_TM, EXPERT_DIM), BF16),
                        pltpu.VMEM((nf, D, MOE_TF), BF16), pltpu.VMEM((nf, D, MOE_TF), BF16),
                        pltpu.VMEM((EXPERT_DIM, D), BF16)],
    )
    return pl.pallas_call(
        _moe_kernel,
        grid_spec=grid_spec,
        out_shape=jax.ShapeDtypeStruct((MOE_ROWS, D), F32),
        compiler_params=_cparams(("arbitrary",)),
        name="moe",
    )(*units, src, h, wg, wu, wd)


def _take_rows(a, rows):
    return a.at[rows].get(mode="promise_in_bounds")


def _route(idx):
    e = idx.reshape(-1)
    onehot = (e[:, None] == jnp.arange(N_EXPERTS, dtype=jnp.int32)[None, :]).astype(jnp.int32)
    counts = jnp.sum(onehot, axis=0)
    rank = jnp.sum((jnp.cumsum(onehot, axis=0) - 1) * onehot, axis=1)
    padded = ((counts + MOE_TM - 1) // MOE_TM) * MOE_TM
    ends = jnp.cumsum(padded)
    starts = ends - padded
    pos = starts[e] + rank
    token = jnp.arange(N_ROWS * TOP_K, dtype=jnp.int32) // TOP_K
    src = jnp.zeros((MOE_ROWS,), jnp.int32).at[pos].set(token, unique_indices=True, mode="promise_in_bounds")
    n_used = (ends[-1] // MOE_TM).astype(jnp.int32)
    tile_start = jnp.minimum(jnp.arange(MOE_TILES, dtype=jnp.int32), n_used - 1) * MOE_TM
    tile_expert = jnp.sum((tile_start[:, None] >= ends[None, :]).astype(jnp.int32), axis=1)
    tile_expert = jnp.minimum(tile_expert, N_EXPERTS - 1).astype(jnp.int32)
    prev = jnp.concatenate([jnp.full((1,), -1, jnp.int32), tile_expert[:-1]])
    tile_first = tile_expert != prev
    nf = EXPERT_DIM // MOE_TF
    tiles = jnp.arange(MOE_TILES, dtype=jnp.int32)
    live = tiles < n_used
    cost = jnp.where(live, jnp.where(tile_first, nf, 1), 0).astype(jnp.int32)
    off = jnp.cumsum(cost) - cost
    n_units = jnp.sum(cost)
    u = jnp.arange(MOE_UNITS, dtype=jnp.int32)
    tile_u = jnp.sum(jnp.logical_and(off[None, :] <= u[:, None], live[None, :]).astype(jnp.int32), axis=1) - 1
    first_u = tile_first[tile_u]
    work = u < n_units
    spare = u - n_units
    unit_kind = jnp.where(work, jnp.where(first_u, MOE_STREAM, MOE_CACHED),
                          jnp.where(spare < MOE_TILES - n_used, MOE_ZERO, MOE_IDLE)).astype(jnp.int32)
    unit_chunk = jnp.where(jnp.logical_and(work, first_u), u - off[tile_u], nf - 1).astype(jnp.int32)
    unit_out = jnp.where(work, tile_u, jnp.minimum(n_used + spare, MOE_TILES - 1)).astype(jnp.int32)
    unit_x = tile_u.astype(jnp.int32)
    unit_expert = tile_expert[tile_u]
    unit_pref = jnp.where(work, jnp.minimum(tile_u + 1, n_used - 1),
                          jnp.where(spare == 0, MOE_DRAIN, -1)).astype(jnp.int32)
    return src, pos.reshape(N_ROWS, TOP_K), (unit_kind, unit_chunk, unit_x, unit_out, unit_expert, unit_pref)


def _combine_kernel(ya_ref, yb_ref, p_ref, x1_ref, mod_ref, g_ref, b_ref, x2_ref):
    g = p_ref[:, 0:1] * ya_ref[...] + p_ref[:, 1:2] * yb_ref[...]
    _finish(x1_ref[...], g, mod_ref, g_ref, b_ref, x2_ref)


def _combine(ya, yb, p, x1, mod, ln_g, ln_b):
    return pl.pallas_call(
        _combine_kernel,
        grid=(N_ROWS // TM,),
        in_specs=[_row_spec(TM), _row_spec(TM), _row_spec(TM, 128), _row_spec(TM), _mod_spec(TM),
                  _const_spec((1, D)), _const_spec((1, D))],
        out_specs=_row_spec(TM),
        out_shape=jax.ShapeDtypeStruct((N_ROWS, D), F32),
        compiler_params=_cparams(("parallel",)),
        name="combine",
    )(ya, yb, p, x1, mod, ln_g.reshape(1, D), ln_b.reshape(1, D))


def _grid_pos_embed():
    rows = DEC_SEQ // GRID_W
    r, col = jnp.meshgrid(jnp.arange(rows, dtype=F32), jnp.arange(GRID_W, dtype=F32), indexing="ij")
    quarter = D // 4
    freq = 1.0 / (10000.0 ** (jnp.arange(quarter, dtype=F32) / quarter))

    def emb(p):
        ang = p.reshape(-1)[:, None] * freq[None, :]
        return jnp.concatenate([jnp.sin(ang), jnp.cos(ang)], axis=-1)

    return jnp.concatenate([emb(r), emb(col)], axis=-1)


def kernel(x_prompt, x_sample, c, state_ssm_re, state_ssm_im, c_ctx, w_ada, b_ada, ln_g, ln_b, s5_a_re, s5_a_im, s5_log_step, s5_b_re, s5_b_im, s5_c_re, s5_c_im, s5_d, s5_w_glu, gm_w_in, gm_b_in, gm_ln_g, gm_ln_b, gm_w_s, gm_b_s, gm_w_out, ffn_w_gate, ffn_w_up, ffn_w_down, moe_w_router, moe_b_router, moe_w_gate, moe_w_up, moe_w_down):
    cond8 = jnp.concatenate([c_ctx[None, :], c, jnp.zeros((N_COND - 1 - DEC_BATCH, D), F32)], axis=0)
    mods = _ada(cond8, w_ada, b_ada).reshape(DEPTH, N_COND, 6, D)

    x = _prep(x_prompt, x_sample, _grid_pos_embed())
    h = None

    fin_re, fin_im = [], []
    for i in range(DEPTH):
        j = i // 2
        mod = mods[i]
        nmod = mods[min(i + 1, DEPTH - 1)]
        if i % 2 == 0:
            params = _s5_params(s5_a_re[j], s5_a_im[j], s5_log_step[j], s5_b_re[j], s5_b_im[j], s5_c_re[j], s5_c_im[j])
            ysum, fin = _s5(x, mod, params, _s5_state_in(state_ssm_re[:, j], state_ssm_im[:, j]))
            fr, fi = _s5_state_out(fin)
            fin_re.append(fr)
            fin_im.append(fi)
            x1, h2 = _glu(ysum, x, mod, s5_d[j], s5_w_glu[j].astype(BF16), ln_g[i, 0], ln_b[i, 0])
            x, h = _ffn(h2, x1, mod, nmod, ffn_w_gate[j].astype(BF16), ffn_w_up[j].astype(BF16),
                        ffn_w_down[j].astype(BF16), ln_g[i, 1], ln_b[i, 1])
        else:
            b_s = jnp.broadcast_to(gm_b_s[j][:, :, None], (GM_HEADS, GM_CHUNK, GM_CHUNK)).astype(F32)
            w_r = jnp.pad(moe_w_router[j].astype(F32), ((0, 0), (0, 128 - N_EXPERTS)))
            w_r_hi = w_r.astype(BF16)
            w_r = jnp.stack([w_r_hi, (w_r - w_r_hi.astype(F32)).astype(BF16)], axis=0)
            b_r = jnp.pad(moe_b_router[j].astype(F32), (0, 128 - N_EXPERTS)).reshape(1, 128)
            x1, h2, idx, p = _gmlp(h, x, mod, gm_w_in[j].astype(BF16), gm_b_in[j], gm_ln_g[j], gm_ln_b[j],
                                   gm_w_s[j].astype(BF16), b_s, gm_w_out[j].astype(BF16), ln_g[i, 0], ln_b[i, 0], w_r, b_r)
            src, pos, units = _route(idx[:, :TOP_K])
            ys = _moe(j, units, src, h2, moe_w_gate, moe_w_up, moe_w_down)
            x = _combine(_take_rows(ys, pos[:, 0]), _take_rows(ys, pos[:, 1]), p, x1, mod, ln_g[i, 1], ln_b[i, 1])

    y_prompt = x[:N_P].reshape(BATCH, SEQ, D)
    y_sample = x[N_P:].reshape(DEC_BATCH, DEC_SEQ, D)
    new_re = jnp.stack(fin_re, axis=1).astype(x_prompt.dtype)
    new_im = jnp.stack(fin_im, axis=1).astype(x_prompt.dtype)
    return (y_prompt, y_sample, new_re, new_im)
```

```python
import jax
import jax.numpy as jnp
from jax import lax
from jax.experimental import pallas as pl
from jax.experimental.pallas import tpu as pltpu

F32 = jnp.float32
BF16 = jnp.bfloat16

D = 1024
BATCH, SEQ = 32, 256
DEC_BATCH, DEC_SEQ = 4, 1024
DEPTH = 4
GRID_W = 64
N_P = BATCH * SEQ
N_S = DEC_BATCH * DEC_SEQ
N_ROWS = N_P + N_S
N_COND = 8

S5_CG, S5_P, S5_G = 16, 64, 64
S5_T = 16
S5_Q = S5_G // 2
P_CHUNKS = SEQ // S5_T
S_CHUNKS = DEC_SEQ // S5_T
S5_ROWS_P = P_CHUNKS * BATCH
S5_ROWS_S = S_CHUNKS * DEC_BATCH
S5_ROWS = S5_ROWS_P + S5_ROWS_S

GM_CHUNK, GM_HEADS = 128, 8
FFN_DIM = 2816
N_EXPERTS, TOP_K, EXPERT_DIM = 8, 2, 3584
ALPHA = (2.0 * DEPTH) ** 0.25
LN_EPS = 1e-5

TM = 512
MOE_TM = 512
MOE_TF = 512
MOE_TILES = (N_ROWS * TOP_K) // MOE_TM + N_EXPERTS
MOE_ROWS = MOE_TILES * MOE_TM
MOE_UNITS = N_EXPERTS * (EXPERT_DIM // MOE_TF) + MOE_TILES
MOE_STREAM, MOE_CACHED, MOE_ZERO, MOE_IDLE = 1, 2, 0, 3
MOE_DRAIN = -2
MOE_AHEAD = 2
MOE_SLOTS = MOE_AHEAD + 1
VMEM_LIMIT = 56 * 1024 * 1024


def _cparams(sem):
    return pltpu.CompilerParams(dimension_semantics=sem, vmem_limit_bytes=VMEM_LIMIT)


def _cond_of_tile(t, tm):
    row0 = t * tm
    return jnp.where(row0 < N_P, 0, 1 + (row0 - N_P) // DEC_SEQ)


def _mod_spec(tm):
    return pl.BlockSpec((None, 6, D), lambda t: (_cond_of_tile(t, tm), 0, 0))


def _row_spec(tm, width=D):
    return pl.BlockSpec((tm, width), lambda t: (t, 0))


def _const_spec(shape):
    return pl.BlockSpec(shape, lambda t: (0,) * len(shape))


def _layer_norm(r, g, b):
    mu = jnp.mean(r, axis=-1, keepdims=True)
    c = r - mu
    var = jnp.mean(c * c, axis=-1, keepdims=True)
    return c * lax.rsqrt(var + LN_EPS) * g + b


def _modulate(x, mod_ref, shift_row, scale_row):
    return x * (1.0 + mod_ref[scale_row:scale_row + 1, :]) + mod_ref[shift_row:shift_row + 1, :]


def _ada_kernel(c_ref, w_ref, b_ref, o_ref):
    c = c_ref[...]
    o_ref[...] = jnp.dot(jax.nn.silu(c), w_ref[...], preferred_element_type=F32) + b_ref[...]


def _ada(cond8, w_ada, b_ada):
    tn = 1536
    return pl.pallas_call(
        _ada_kernel,
        grid=(DEPTH, 6 * D // tn),
        in_specs=[pl.BlockSpec((N_COND, D), lambda i, n: (0, 0)),
                  pl.BlockSpec((None, D, tn), lambda i, n: (i, 0, n)),
                  pl.BlockSpec((None, 1, tn), lambda i, n: (i, 0, n))],
        out_specs=pl.BlockSpec((None, N_COND, tn), lambda i, n: (i, 0, n)),
        out_shape=jax.ShapeDtypeStruct((DEPTH, N_COND, 6 * D), F32),
        compiler_params=_cparams(("arbitrary", "arbitrary")),
        name="ada",
    )(cond8, w_ada, b_ada.reshape(DEPTH, 1, 6 * D))


def _prep_kernel(xp_ref, xs_ref, pos_ref, xo_ref):
    is_latent = pl.program_id(0) * TM >= N_P
    xo_ref[...] = jnp.where(is_latent, xs_ref[...] + pos_ref[...], xp_ref[...])


def _prep(x_prompt, x_sample, pos):
    p_tiles = N_P // TM

    def latent_tile(t):
        return jnp.maximum(t - p_tiles, 0)

    return pl.pallas_call(
        _prep_kernel,
        grid=(N_ROWS // TM,),
        in_specs=[pl.BlockSpec((TM, D), lambda t: (jnp.minimum(t, p_tiles - 1), 0)),
                  pl.BlockSpec((TM, D), lambda t: (latent_tile(t), 0)),
                  pl.BlockSpec((TM, D), lambda t: (latent_tile(t) % (DEC_SEQ // TM), 0))],
        out_specs=_row_spec(TM),
        out_shape=jax.ShapeDtypeStruct((N_ROWS, D), F32),
        compiler_params=_cparams(("parallel",)),
        name="prep",
    )(x_prompt.reshape(N_P, D), x_sample.reshape(N_S, D), pos)


S5_BLK = 4
S5_W = 2 * S5_T * S5_CG
S5_GR = 2 * S5_CG
_NT = (((1,), (1,)), ((), ()))


def _granule_transpose4(a, lane):
    low = lane < 64
    a0 = jnp.where(low, a[0], pltpu.roll(a[2], 64, 1))
    a2 = jnp.where(low, pltpu.roll(a[0], 64, 1), a[2])
    a1 = jnp.where(low, a[1], pltpu.roll(a[3], 64, 1))
    a3 = jnp.where(low, pltpu.roll(a[1], 64, 1), a[3])
    even = (lane // S5_GR) % 2 == 0
    return [jnp.where(even, a0, pltpu.roll(a1, 32, 1)), jnp.where(even, pltpu.roll(a0, 96, 1), a1),
            jnp.where(even, a2, pltpu.roll(a3, 32, 1)), jnp.where(even, pltpu.roll(a2, 96, 1), a3)]


def _s5_chunk_matrices(p4, bb_ref, c_ref, pw_ref, min_scr, mi_scr, p_scr):
    half = 2 * S5_P
    gran = lax.broadcasted_iota(jnp.int32, (S5_W, 128), 1) // S5_GR
    rep = (lax.broadcasted_iota(jnp.int32, (S5_GR, S5_W), 1) % S5_GR
           == lax.broadcasted_iota(jnp.int32, (S5_GR, S5_W), 0)).astype(BF16)
    mout = []
    for k in range(2):
        bb_re, bb_im = bb_ref[p4, k, 0], bb_ref[p4, k, 1]
        c_re, c_im = c_ref[p4, k, 0], c_ref[p4, k, 1]
        out_blocks = []
        for r in range(S5_T):
            t_in = S5_T - 1 - r if k == 0 else r
            t_out = r + 1 if k == 0 else S5_T - r
            rows = slice(S5_GR * r, S5_GR * (r + 1))
            pr, pi = pw_ref[p4, k, 0, t_in:t_in + 1, :], pw_ref[p4, k, 1, t_in:t_in + 1, :]
            min_scr[k, rows, 0:half] = bb_re * pr - bb_im * pi
            min_scr[k, rows, half:2 * half] = bb_re * pi + bb_im * pr
            pr, pi = pw_ref[p4, k, 0, t_out:t_out + 1, :], pw_ref[p4, k, 1, t_out:t_out + 1, :]
            out_blocks.append(jnp.concatenate([c_re * pr - c_im * pi, -(c_re * pi + c_im * pr)], axis=-1))
        mout.append(jnp.concatenate(out_blocks, axis=0).astype(BF16))
        c2 = jnp.concatenate([c_re, -c_im], axis=-1)
        c2_hi = c2.astype(BF16)
        c2_lo = (c2 - c2_hi.astype(F32)).astype(BF16)
        m = min_scr[k]
        m_hi = m.astype(BF16)
        m_lo = (m - m_hi.astype(F32)).astype(BF16)
        z = (lax.dot_general(m_hi, c2_hi, _NT, preferred_element_type=F32)
             + lax.dot_general(m_lo, c2_hi, _NT, preferred_element_type=F32)
             + lax.dot_general(m_hi, c2_lo, _NT, preferred_element_type=F32))
        zrep = jnp.dot(z.astype(BF16), rep, preferred_element_type=F32)
        zeros = jnp.zeros((S5_W, S5_W), F32)
        p_scr[0:S5_W, :] = zrep if k == 0 else zeros
        p_scr[S5_W:2 * S5_W, :] = zeros if k == 0 else zrep
        for j in range(S5_W // 128):
            cols = slice(128 * j, 128 * (j + 1))
            acc = None
            for s4 in range(4):
                s = 4 * j + s4
                start = S5_GR * (S5_T - 1 - s) if k == 0 else S5_W - S5_GR * s
                win = p_scr[start:start + S5_W, cols]
                acc = win if acc is None else jnp.where(gran == s4, win, acc)
            mi_scr[k, :, cols] = acc.astype(BF16)
    return mout


def _s5_scan(k, s_scrs, h_scrs, row0, nb, nchunks, hre, him, ar, ai):
    for n in (range(nchunks - 1, -1, -1) if k == 1 else range(nchunks)):
        rows = pl.ds(row0 + n, nb, stride=nchunks)
        h_scrs[0][rows, :] = hre
        h_scrs[1][rows, :] = him
        sre = s_scrs[0][rows, :]
        sim = s_scrs[1][rows, :]
        hre, him = ar * hre - ai * him + sre, ar * him + ai * hre + sim
    return hre, him


def _s5_kernel(x_ref, mod_ref, bb_ref, c_ref, pw_ref, h0_ref, y_ref, fin_ref,
               u_scr, yp_scr, sre_scr, sim_scr, hre_scr, him_scr, min_scr, mi_scr, p_scr):
    half = 2 * S5_P
    lane = lax.broadcasted_iota(jnp.int32, (S5_ROWS, 128), 1)
    scale = jnp.concatenate(
        [jnp.broadcast_to(1.0 + mod_ref[0, 1:2, :], (S5_ROWS_P, 128))]
        + [jnp.broadcast_to(1.0 + mod_ref[1 + b, 1:2, :], (S_CHUNKS, 128)) for b in range(DEC_BATCH)], axis=0)
    shift = jnp.concatenate(
        [jnp.broadcast_to(mod_ref[0, 0:1, :], (S5_ROWS_P, 128))]
        + [jnp.broadcast_to(mod_ref[1 + b, 0:1, :], (S_CHUNKS, 128)) for b in range(DEC_BATCH)], axis=0)
    for j in range(S5_T // 4):
        a = [x_ref[pl.ds(4 * j + s4, S5_ROWS, stride=S5_T), :] * scale + shift for s4 in range(4)]
        b = _granule_transpose4(a, lane)
        for p4 in range(S5_BLK):
            u_scr[p4, :, 128 * j:128 * (j + 1)] = b[p4].astype(BF16)

    for p4 in range(S5_BLK):
        mout = _s5_chunk_matrices(p4, bb_ref, c_ref, pw_ref, min_scr, mi_scr, p_scr)
        u = u_scr[p4]
        u0, u1 = u[:, 0:256], u[:, 256:512]
        y = None
        for k in range(2):
            s = jnp.dot(u, min_scr[k].astype(BF16), preferred_element_type=F32)
            sre_scr[...] = s[:, 0:half]
            sim_scr[...] = s[:, half:2 * half]
            s_scrs, h_scrs = (sre_scr, sim_scr), (hre_scr, him_scr)
            ar_p = jnp.broadcast_to(pw_ref[p4, k, 0, S5_T:S5_T + 1, :], (BATCH, half))
            ai_p = jnp.broadcast_to(pw_ref[p4, k, 1, S5_T:S5_T + 1, :], (BATCH, half))
            zero = jnp.zeros((BATCH, half), F32)
            fre, fim = _s5_scan(k, s_scrs, h_scrs, 0, BATCH, P_CHUNKS, zero, zero, ar_p, ai_p)
            fin_ref[p4, k, :, 0:half] = fre
            fin_ref[p4, k, :, half:2 * half] = fim
            _s5_scan(k, s_scrs, h_scrs, S5_ROWS_P, DEC_BATCH, S_CHUNKS,
                     h0_ref[p4, k, :, 0:half], h0_ref[p4, k, :, half:2 * half], ar_p[:DEC_BATCH], ai_p[:DEC_BATCH])
            hin = jnp.concatenate([hre_scr[...], him_scr[...]], axis=-1).astype(BF16)
            readout = lax.dot_general(hin, mout[k], _NT, preferred_element_type=F32)
            m00 = jnp.dot(u0, mi_scr[k, 0:256, 0:256], preferred_element_type=F32)
            m11 = jnp.dot(u1, mi_scr[k, 256:512, 256:512], preferred_element_type=F32)
            if k == 0:
                m11 = m11 + jnp.dot(u0, mi_scr[k, 0:256, 256:512], preferred_element_type=F32)
            else:
                m00 = m00 + jnp.dot(u1, mi_scr[k, 256:512, 0:256], preferred_element_type=F32)
            yk = jnp.concatenate([m00, m11], axis=-1) + readout
            y = yk if y is None else y + yk
        yp_scr[p4] = y

    for j in range(S5_T // 4):
        b = [yp_scr[p4, :, 128 * j:128 * (j + 1)] for p4 in range(S5_BLK)]
        a = _granule_transpose4(b, lane)
        for s4 in range(4):
            y_ref[pl.ds(4 * j + s4, S5_ROWS, stride=S5_T), :] = a[s4]


def _s5(x, mods, params, h0):
    bb, cc, pw = params
    nblk = S5_Q // S5_BLK
    return pl.pallas_call(
        _s5_kernel,
        grid=(nblk,),
        in_specs=[pl.BlockSpec((N_ROWS, 128), lambda q: (0, q)),
                  pl.BlockSpec((N_COND, 6, 128), lambda q: (0, 0, q)),
                  pl.BlockSpec((S5_BLK, 2, 2, S5_GR, 128), lambda q: (q, 0, 0, 0, 0)),
                  pl.BlockSpec((S5_BLK, 2, 2, S5_GR, 128), lambda q: (q, 0, 0, 0, 0)),
                  pl.BlockSpec((S5_BLK, 2, 2, S5_T + 1, 128), lambda q: (q, 0, 0, 0, 0)),
                  pl.BlockSpec((S5_BLK, 2, DEC_BATCH, 256), lambda q: (q, 0, 0, 0))],
        out_specs=[pl.BlockSpec((N_ROWS, 128), lambda q: (0, q)),
                   pl.BlockSpec((S5_BLK, 2, BATCH, 256), lambda q: (q, 0, 0, 0))],
        out_shape=[jax.ShapeDtypeStruct((N_ROWS, D), F32),
                   jax.ShapeDtypeStruct((S5_Q, 2, BATCH, 256), F32)],
        scratch_shapes=[pltpu.VMEM((S5_BLK, S5_ROWS, S5_W), BF16), pltpu.VMEM((S5_BLK, S5_ROWS, S5_W), F32),
                        pltpu.VMEM((S5_ROWS, 128), F32), pltpu.VMEM((S5_ROWS, 128), F32),
                        pltpu.VMEM((S5_ROWS, 128), F32), pltpu.VMEM((S5_ROWS, 128), F32),
                        pltpu.VMEM((2, S5_W, 256), F32),
                        pltpu.VMEM((2, S5_W, S5_W), BF16), pltpu.VMEM((2 * S5_W, S5_W), F32)],
        compiler_params=_cparams(("parallel",)),
        name="s5",
    )(x, mods, bb, cc, pw, h0)


def _s5_params(a_re, a_im, log_step, b_re, b_im, c_re, c_im):
    lam_re = jnp.minimum(a_re.astype(F32), -1e-4)
    lam_im = a_im.astype(F32)
    dt = jnp.exp(log_step.astype(F32))[..., None]
    j = jnp.arange(S5_T + 1, dtype=F32)[:, None, None, None]
    mag = jnp.exp(lam_re * dt * j)
    pw_re = mag * jnp.cos(lam_im * dt * j)
    pw_im = mag * jnp.sin(lam_im * dt * j)
    nr, ni = pw_re[1] - 1.0, pw_im[1]
    den = lam_re * lam_re + lam_im * lam_im
    q_re = (nr * lam_re + ni * lam_im) / den
    q_im = (ni * lam_re - nr * lam_im) / den
    b_re, b_im = b_re.astype(F32), b_im.astype(F32)
    bb_re = q_re[..., None] * b_re - q_im[..., None] * b_im
    bb_im = q_re[..., None] * b_im + q_im[..., None] * b_re
    eye = jnp.eye(2, dtype=F32)

    def pair_rows(m, spec):
        m = m.reshape(2, S5_Q, 2, *m.shape[2:])
        return jnp.einsum(spec, m, eye).reshape(S5_Q, 2, S5_GR, 2 * S5_P)

    bb = jnp.stack([pair_rows(bb_re, "kqgpc,gh->qkgchp"), pair_rows(bb_im, "kqgpc,gh->qkgchp")], axis=2)
    cc = jnp.stack([pair_rows(c_re.astype(F32), "kqgcp,gh->qkgchp"),
                    pair_rows(c_im.astype(F32), "kqgcp,gh->qkgchp")], axis=2)

    def pw_lay(p):
        return p.reshape(S5_T + 1, 2, S5_Q, 2 * S5_P).transpose(2, 1, 0, 3)
    pw = jnp.stack([pw_lay(pw_re), pw_lay(pw_im)], axis=2)
    return bb, cc, pw


def _s5_state_in(st_re, st_im):
    def lay(s):
        return s.astype(F32).reshape(DEC_BATCH, 2, S5_Q, 2 * S5_P).transpose(2, 1, 0, 3)
    return jnp.concatenate([lay(st_re), lay(st_im)], axis=-1)


def _s5_state_out(fin):
    def lay(s):
        return s.transpose(2, 1, 0, 3).reshape(BATCH, 2, S5_G, S5_P)
    return lay(fin[..., :2 * S5_P]), lay(fin[..., 2 * S5_P:])


def _glu_kernel(ys_ref, x_ref, mod_ref, d_ref, w_ref, g_ref, b_ref, x1_ref, h2_ref):
    x = x_ref[...]
    h = _modulate(x, mod_ref, 0, 1)
    y = jax.nn.gelu(ys_ref[...] + d_ref[...] * h).astype(BF16)
    z = jnp.dot(y, w_ref[...], preferred_element_type=F32)
    f = z[:, :D] * jax.nn.sigmoid(z[:, D:])
    x1 = _layer_norm(ALPHA * x + (1.0 + mod_ref[2:3, :]) * f, g_ref[...], b_ref[...])
    x1_ref[...] = x1
    h2_ref[...] = _modulate(x1, mod_ref, 3, 4).astype(BF16)


def _glu(ysum, x, mod, d, w_glu, ln_g, ln_b):
    return pl.pallas_call(
        _glu_kernel,
        grid=(N_ROWS // TM,),
        in_specs=[_row_spec(TM), _row_spec(TM), _mod_spec(TM), _const_spec((1, D)),
                  _const_spec((D, 2 * D)), _const_spec((1, D)), _const_spec((1, D))],
        out_specs=[_row_spec(TM), _row_spec(TM)],
        out_shape=[jax.ShapeDtypeStruct((N_ROWS, D), F32), jax.ShapeDtypeStruct((N_ROWS, D), BF16)],
        compiler_params=_cparams(("parallel",)),
        name="glu",
    )(ysum, x, mod, d.reshape(1, D), w_glu, ln_g.reshape(1, D), ln_b.reshape(1, D))


def _finish(x1, g, mod_ref, g_ref, b_ref, x2_ref):
    x2 = _layer_norm(ALPHA * x1 + (1.0 + mod_ref[5:6, :]) * g, g_ref[...], b_ref[...])
    x2_ref[...] = x2
    return x2


FFN_TF = FFN_DIM // 2


def _ffn_kernel(h_ref, x1_ref, mod_ref, nmod_ref, wg_ref, wu_ref, wd_ref, g_ref, b_ref, x2_ref, hn_ref):
    h = h_ref[...]
    acc = None
    for c in range(FFN_DIM // FFN_TF):
        sl = slice(c * FFN_TF, (c + 1) * FFN_TF)
        a = jnp.dot(h, wg_ref[:, sl], preferred_element_type=F32)
        b = jnp.dot(h, wu_ref[:, sl], preferred_element_type=F32)
        p = jnp.dot((jax.nn.silu(a) * b).astype(BF16), wd_ref[sl, :], preferred_element_type=F32)
        acc = p if acc is None else acc + p
    x2 = _finish(x1_ref[...], acc, mod_ref, g_ref, b_ref, x2_ref)
    hn_ref[...] = _modulate(x2, nmod_ref, 0, 1).astype(BF16)


def _ffn(h2, x1, mod, nmod, wg, wu, wd, ln_g, ln_b):
    single = pl.Buffered(1)
    return pl.pallas_call(
        _ffn_kernel,
        grid=(N_ROWS // TM,),
        in_specs=[_row_spec(TM), _row_spec(TM), _mod_spec(TM), _mod_spec(TM),
                  pl.BlockSpec((D, FFN_DIM), lambda t: (0, 0), pipeline_mode=single),
                  pl.BlockSpec((D, FFN_DIM), lambda t: (0, 0), pipeline_mode=single),
                  pl.BlockSpec((FFN_DIM, D), lambda t: (0, 0), pipeline_mode=single),
                  _const_spec((1, D)), _const_spec((1, D))],
        out_specs=[_row_spec(TM), _row_spec(TM)],
        out_shape=[jax.ShapeDtypeStruct((N_ROWS, D), F32), jax.ShapeDtypeStruct((N_ROWS, D), BF16)],
        compiler_params=_cparams(("parallel",)),
        name="ffn",
    )(h2, x1, mod, nmod, wg, wu, wd, ln_g.reshape(1, D), ln_b.reshape(1, D))


def _gmlp_kernel(h_ref, x_ref, mod_ref, win_ref, bin_ref, vg_ref, vb_ref, ws_ref, bs_ref, wout_ref,
                 g_ref, b_ref, wr_ref, br_ref, x1_ref, h2_ref, idx_ref, p_ref, s_scr):
    z = jax.nn.gelu(jnp.dot(h_ref[...], win_ref[...], preferred_element_type=F32) + bin_ref[...])
    u = z[:, :D]
    v = _layer_norm(z[:, D:], vg_ref[...], vb_ref[...]).astype(BF16)
    n_chunks = TM // GM_CHUNK
    for hd in range(GM_HEADS):
        cols = slice(hd * GM_CHUNK, (hd + 1) * GM_CHUNK)
        rhs = jnp.concatenate([v[n * GM_CHUNK:(n + 1) * GM_CHUNK, cols] for n in range(n_chunks)], axis=-1)
        s = jnp.dot(ws_ref[hd], rhs, preferred_element_type=F32)
        for n in range(n_chunks):
            s_scr[n * GM_CHUNK:(n + 1) * GM_CHUNK, cols] = s[:, n * GM_CHUNK:(n + 1) * GM_CHUNK] + bs_ref[hd]
    f = jnp.dot((u * s_scr[...]).astype(BF16), wout_ref[...], preferred_element_type=F32)
    x1 = _layer_norm(ALPHA * x_ref[...] + (1.0 + mod_ref[2:3, :]) * f, g_ref[...], b_ref[...])
    x1_ref[...] = x1
    h2 = _modulate(x1, mod_ref, 3, 4)
    h2_ref[...] = h2
    h_hi = h2.astype(BF16)
    h_lo = (h2 - h_hi.astype(F32)).astype(BF16)
    w_hi, w_lo = wr_ref[0], wr_ref[1]
    logits = (jnp.dot(h_hi, w_hi, preferred_element_type=F32) + jnp.dot(h_lo, w_hi, preferred_element_type=F32)
              + jnp.dot(h_hi, w_lo, preferred_element_type=F32)) + br_ref[...]
    lane = lax.broadcasted_iota(jnp.int32, logits.shape, 1)
    neg = jnp.float32(-jnp.inf)
    logits = jnp.where(lane < N_EXPERTS, logits, neg)
    v1 = jnp.max(logits, axis=-1, keepdims=True)
    i1 = jnp.min(jnp.where(logits == v1, lane, 128), axis=-1, keepdims=True)
    rest = jnp.where(lane == i1, neg, logits)
    v2 = jnp.max(rest, axis=-1, keepdims=True)
    i2 = jnp.min(jnp.where(rest == v2, lane, 128), axis=-1, keepdims=True)
    e2 = jnp.exp(v2 - v1)
    p1 = 1.0 / (1.0 + e2)
    p2 = e2 / (1.0 + e2)
    idx_ref[...] = jnp.where(lane == 0, i1, jnp.where(lane == 1, i2, 0))
    p_ref[...] = jnp.where(lane == 0, p1, jnp.where(lane == 1, p2, 0.0))


def _gmlp(h1, x, mod, w_in, b_in, vg, vb, w_s, b_s, w_out, ln_g, ln_b, w_r, b_r):
    return pl.pallas_call(
        _gmlp_kernel,
        grid=(N_ROWS // TM,),
        in_specs=[_row_spec(TM), _row_spec(TM), _mod_spec(TM),
                  _const_spec((D, 2 * D)), _const_spec((1, 2 * D)), _const_spec((1, D)), _const_spec((1, D)),
                  _const_spec((GM_HEADS, GM_CHUNK, GM_CHUNK)), _const_spec((GM_HEADS, GM_CHUNK, GM_CHUNK)),
                  _const_spec((D, D)), _const_spec((1, D)), _const_spec((1, D)),
                  _const_spec((2, D, 128)), _const_spec((1, 128))],
        out_specs=[_row_spec(TM), _row_spec(TM), _row_spec(TM, 128), _row_spec(TM, 128)],
        out_shape=[jax.ShapeDtypeStruct((N_ROWS, D), F32), jax.ShapeDtypeStruct((N_ROWS, D), F32),
                   jax.ShapeDtypeStruct((N_ROWS, 128), jnp.int32), jax.ShapeDtypeStruct((N_ROWS, 128), F32)],
        scratch_shapes=[pltpu.VMEM((TM, D), F32)],
        compiler_params=_cparams(("parallel",)),
        name="gmlp",
    )(h1, x, mod, w_in, b_in.reshape(1, 2 * D), vg.reshape(1, D), vb.reshape(1, D), w_s, b_s, w_out,
      ln_g.reshape(1, D), ln_b.reshape(1, D), w_r, b_r)


def _moe_kernel(kind_ref, chunk_ref, ux_ref, uo_ref, ue_ref, pref_ref, src_ref, h_hbm, wg_ref, wu_ref, wd_ref, y_ref,
                acc_ref, xg_ref, sem, xb_ref, act_ref, wgb, wub, wdb):
    u = pl.program_id(0)
    nf = EXPERT_DIM // MOE_TF
    kind = kind_ref[u]
    c = chunk_ref[u]
    tile = ux_ref[u]
    slot = tile % MOE_SLOTS
    ahead_slot = (tile + MOE_AHEAD) % MOE_SLOTS
    begins_tile = jnp.logical_or(kind == MOE_CACHED, jnp.logical_and(kind == MOE_STREAM, c == 0))

    def row_copy(t, i, dst_slot):
        row = src_ref[t * MOE_TM + i]
        return pltpu.make_async_copy(h_hbm.at[pl.ds(row, 1), :], xg_ref.at[dst_slot, pl.ds(i, 1), :], sem.at[dst_slot])

    @pl.when(u == 0)
    def _():
        for t in range(MOE_AHEAD):
            def start_row(i, carry, t=t):
                row_copy(t, i, t).start()
                return carry
            lax.fori_loop(0, MOE_TM, start_row, 0)

    def wait_rows(dst_slot):
        pltpu.make_async_copy(h_hbm.at[pl.ds(0, MOE_TM), :], xg_ref.at[dst_slot], sem.at[dst_slot]).wait()

    def start_tile_ahead():
        nxt = pref_ref[u]
        for i in range(MOE_TM):
            row_copy(nxt, i, ahead_slot).start()

    @pl.when(begins_tile)
    def _():
        wait_rows(slot)

    @pl.when(pref_ref[u] == MOE_DRAIN)
    def _():
        for k in range(1, MOE_AHEAD + 1):
            wait_rows((tile + k) % MOE_SLOTS)

    def swiglu_chunk(x, k):
        a = jnp.dot(x, wgb[k], preferred_element_type=F32)
        b = jnp.dot(x, wub[k], preferred_element_type=F32)
        return (jax.nn.silu(a) * b).astype(BF16)

    @pl.when(kind == MOE_STREAM)
    def _():
        rows_c = pl.ds(pl.multiple_of(c * MOE_TF, MOE_TF), MOE_TF)
        wgb[c] = wg_ref[...].astype(BF16)
        wub[c] = wu_ref[...].astype(BF16)
        wdb[rows_c, :] = wd_ref[...].astype(BF16)

        @pl.when(c == 0)
        def _():
            xb_ref[...] = xg_ref[slot].astype(BF16)
            start_tile_ahead()

        p = jnp.dot(swiglu_chunk(xb_ref[...], c), wdb[rows_c, :], preferred_element_type=F32)
        @pl.when(c == 0)
        def _():
            acc_ref[...] = p

        @pl.when(jnp.logical_and(c > 0, c < nf - 1))
        def _():
            acc_ref[...] += p

        @pl.when(c == nf - 1)
        def _():
            y_ref[...] = acc_ref[...] + p

    @pl.when(kind == MOE_CACHED)
    def _():
        x = xg_ref[slot].astype(BF16)
        start_tile_ahead()
        for k in range(nf):
            act_ref[:, k * MOE_TF:(k + 1) * MOE_TF] = swiglu_chunk(x, k)
        y_ref[...] = jnp.dot(act_ref[...], wdb[...], preferred_element_type=F32)

    @pl.when(kind == MOE_ZERO)
    def _():
        y_ref[...] = jnp.zeros_like(y_ref)


def _moe(layer, units, src, h, wg, wu, wd):
    nf = EXPERT_DIM // MOE_TF

    def w_spec(shape, index):
        return pl.BlockSpec(shape, lambda u, kind, chunk, ux, uo, ue, pref, src_: index(ue[u], chunk[u]))

    grid_spec = pltpu.PrefetchScalarGridSpec(
        num_scalar_prefetch=7,
        grid=(MOE_UNITS,),
        in_specs=[pl.BlockSpec(memory_space=pl.ANY),
                  w_spec((None, None, D, MOE_TF), lambda e, k: (layer, e, 0, k)),
                  w_spec((None, None, D, MOE_TF), lambda e, k: (layer, e, 0, k)),
                  w_spec((None, None, MOE_TF, D), lambda e, k: (layer, e, k, 0))],
        out_specs=pl.BlockSpec((MOE_TM, D), lambda u, kind, chunk, ux, uo, ue, pref, src_: (uo[u], 0)),
        scratch_shapes=[pltpu.VMEM((MOE_TM, D), F32), pltpu.VMEM((MOE_SLOTS, MOE_TM, D), F32),
                        pltpu.SemaphoreType.DMA((MOE_SLOTS,)), pltpu.VMEM((MOE_TM, D), BF16),
                        pltpu.VMEM((MOE_TM, EXPERT_DIM), BF16),
                        pltpu.VMEM((nf, D, MOE_TF), BF16), pltpu.VMEM((nf, D, MOE_TF), BF16),
                        pltpu.VMEM((EXPERT_DIM, D), BF16)],
    )
    return pl.pallas_call(
        _moe_kernel,
        grid_spec=grid_spec,
        out_shape=jax.ShapeDtypeStruct((MOE_ROWS, D), F32),
        compiler_params=_cparams(("arbitrary",)),
        name="moe",
    )(*units, src, h, wg, wu, wd)


def _take_rows(a, rows):
    return a.at[rows].get(mode="promise_in_bounds")


def _route(idx):
    e = idx.reshape(-1)
    onehot = (e[:, None] == jnp.arange(N_EXPERTS, dtype=jnp.int32)[None, :]).astype(jnp.int32)
    counts = jnp.sum(onehot, axis=0)
    rank = jnp.sum((jnp.cumsum(onehot, axis=0) - 1) * onehot, axis=1)
    padded = ((counts + MOE_TM - 1) // MOE_TM) * MOE_TM
    ends = jnp.cumsum(padded)
    starts = ends - padded
    pos = starts[e] + rank
    token = jnp.arange(N_ROWS * TOP_K, dtype=jnp.int32) // TOP_K
    src = jnp.zeros((MOE_ROWS,), jnp.int32).at[pos].set(token, unique_indices=True, mode="promise_in_bounds")
    n_used = (ends[-1] // MOE_TM).astype(jnp.int32)
    tile_start = jnp.minimum(jnp.arange(MOE_TILES, dtype=jnp.int32), n_used - 1) * MOE_TM
    tile_expert = jnp.sum((tile_start[:, None] >= ends[None, :]).astype(jnp.int32), axis=1)
    tile_expert = jnp.minimum(tile_expert, N_EXPERTS - 1).astype(jnp.int32)
    prev = jnp.concatenate([jnp.full((1,), -1, jnp.int32), tile_expert[:-1]])
    tile_first = tile_expert != prev
    nf = EXPERT_DIM // MOE_TF
    tiles = jnp.arange(MOE_TILES, dtype=jnp.int32)
    live = tiles < n_used
    cost = jnp.where(live, jnp.where(tile_first, nf, 1), 0).astype(jnp.int32)
    off = jnp.cumsum(cost) - cost
    n_units = jnp.sum(cost)
    u = jnp.arange(MOE_UNITS, dtype=jnp.int32)
    tile_u = jnp.sum(jnp.logical_and(off[None, :] <= u[:, None], live[None, :]).astype(jnp.int32), axis=1) - 1
    first_u = tile_first[tile_u]
    work = u < n_units
    spare = u - n_units
    unit_kind = jnp.where(work, jnp.where(first_u, MOE_STREAM, MOE_CACHED),
                          jnp.where(spare < MOE_TILES - n_used, MOE_ZERO, MOE_IDLE)).astype(jnp.int32)
    unit_chunk = jnp.where(jnp.logical_and(work, first_u), u - off[tile_u], nf - 1).astype(jnp.int32)
    unit_out = jnp.where(work, tile_u, jnp.minimum(n_used + spare, MOE_TILES - 1)).astype(jnp.int32)
    unit_x = tile_u.astype(jnp.int32)
    unit_expert = tile_expert[tile_u]
    unit_pref = jnp.where(work, jnp.minimum(tile_u + MOE_AHEAD, n_used - 1),
                          jnp.where(spare == 0, MOE_DRAIN, -1)).astype(jnp.int32)
    return src, pos.reshape(N_ROWS, TOP_K), (unit_kind, unit_chunk, unit_x, unit_out, unit_expert, unit_pref)


def _combine_kernel(ya_ref, yb_ref, p_ref, x1_ref, mod_ref, g_ref, b_ref, x2_ref):
    g = p_ref[:, 0:1] * ya_ref[...] + p_ref[:, 1:2] * yb_ref[...]
    _finish(x1_ref[...], g, mod_ref, g_ref, b_ref, x2_ref)


def _combine(ya, yb, p, x1, mod, ln_g, ln_b):
    return pl.pallas_call(
        _combine_kernel,
        grid=(N_ROWS // TM,),
        in_specs=[_row_spec(TM), _row_spec(TM), _row_spec(TM, 128), _row_spec(TM), _mod_spec(TM),
                  _const_spec((1, D)), _const_spec((1, D))],
        out_specs=_row_spec(TM),
        out_shape=jax.ShapeDtypeStruct((N_ROWS, D), F32),
        compiler_params=_cparams(("parallel",)),
        name="combine",
    )(ya, yb, p, x1, mod, ln_g.reshape(1, D), ln_b.reshape(1, D))


def _grid_pos_embed():
    rows = DEC_SEQ // GRID_W
    r, col = jnp.meshgrid(jnp.arange(rows, dtype=F32), jnp.arange(GRID_W, dtype=F32), indexing="ij")
    quarter = D // 4
    freq = 1.0 / (10000.0 ** (jnp.arange(quarter, dtype=F32) / quarter))

    def emb(p):
        ang = p.reshape(-1)[:, None] * freq[None, :]
        return jnp.concatenate([jnp.sin(ang), jnp.cos(ang)], axis=-1)

    return jnp.concatenate([emb(r), emb(col)], axis=-1)


def kernel(x_prompt, x_sample, c, state_ssm_re, state_ssm_im, c_ctx, w_ada, b_ada, ln_g, ln_b, s5_a_re, s5_a_im, s5_log_step, s5_b_re, s5_b_im, s5_c_re, s5_c_im, s5_d, s5_w_glu, gm_w_in, gm_b_in, gm_ln_g, gm_ln_b, gm_w_s, gm_b_s, gm_w_out, ffn_w_gate, ffn_w_up, ffn_w_down, moe_w_router, moe_b_router, moe_w_gate, moe_w_up, moe_w_down):
    cond8 = jnp.concatenate([c_ctx[None, :], c, jnp.zeros((N_COND - 1 - DEC_BATCH, D), F32)], axis=0)
    mods = _ada(cond8, w_ada, b_ada).reshape(DEPTH, N_COND, 6, D)

    x = _prep(x_prompt, x_sample, _grid_pos_embed())
    h = None

    fin_re, fin_im = [], []
    for i in range(DEPTH):
        j = i // 2
        mod = mods[i]
        nmod = mods[min(i + 1, DEPTH - 1)]
        if i % 2 == 0:
            params = _s5_params(s5_a_re[j], s5_a_im[j], s5_log_step[j], s5_b_re[j], s5_b_im[j], s5_c_re[j], s5_c_im[j])
            ysum, fin = _s5(x, mod, params, _s5_state_in(state_ssm_re[:, j], state_ssm_im[:, j]))
            fr, fi = _s5_state_out(fin)
            fin_re.append(fr)
            fin_im.append(fi)
            x1, h2 = _glu(ysum, x, mod, s5_d[j], s5_w_glu[j].astype(BF16), ln_g[i, 0], ln_b[i, 0])
            x, h = _ffn(h2, x1, mod, nmod, ffn_w_gate[j].astype(BF16), ffn_w_up[j].astype(BF16),
                        ffn_w_down[j].astype(BF16), ln_g[i, 1], ln_b[i, 1])
        else:
            b_s = jnp.broadcast_to(gm_b_s[j][:, :, None], (GM_HEADS, GM_CHUNK, GM_CHUNK)).astype(F32)
            w_r = jnp.pad(moe_w_router[j].astype(F32), ((0, 0), (0, 128 - N_EXPERTS)))
            w_r_hi = w_r.astype(BF16)
            w_r = jnp.stack([w_r_hi, (w_r - w_r_hi.astype(F32)).astype(BF16)], axis=0)
            b_r = jnp.pad(moe_b_router[j].astype(F32), (0, 128 - N_EXPERTS)).reshape(1, 128)
            x1, h2, idx, p = _gmlp(h, x, mod, gm_w_in[j].astype(BF16), gm_b_in[j], gm_ln_g[j], gm_ln_b[j],
                                   gm_w_s[j].astype(BF16), b_s, gm_w_out[j].astype(BF16), ln_g[i, 0], ln_b[i, 0], w_r, b_r)
            src, pos, units = _route(idx[:, :TOP_K])
            ys = _moe(j, units, src, h2, moe_w_gate, moe_w_up, moe_w_down)
            x = _combine(_take_rows(ys, pos[:, 0]), _take_rows(ys, pos[:, 1]), p, x1, mod, ln_g[i, 1], ln_b[i, 1])

    y_prompt = x[:N_P].reshape(BATCH, SEQ, D)
    y_sample = x[N_P:].reshape(DEC_BATCH, DEC_SEQ, D)
    new_re = jnp.stack(fin_re, axis=1).astype(x_prompt.dtype)
    new_im = jnp.stack(fin_im, axis=1).astype(x_prompt.dtype)
    return (y_prompt, y_sample, new_re, new_im)
```

```python
import jax
import jax.numpy as jnp
from jax import lax
from jax.experimental import pallas as pl
from jax.experimental.pallas import tpu as pltpu

F32 = jnp.float32
BF16 = jnp.bfloat16

D = 1024
BATCH, SEQ = 32, 256
DEC_BATCH, DEC_SEQ = 4, 1024
DEPTH = 4
GRID_W = 64
N_P = BATCH * SEQ
N_S = DEC_BATCH * DEC_SEQ
N_ROWS = N_P + N_S
N_COND = 8

S5_CG, S5_P, S5_G = 16, 64, 64
S5_T = 16
S5_Q = S5_G // 2
P_CHUNKS = SEQ // S5_T
S_CHUNKS = DEC_SEQ // S5_T
S5_ROWS_P = P_CHUNKS * BATCH
S5_ROWS_S = S_CHUNKS * DEC_BATCH
S5_ROWS = S5_ROWS_P + S5_ROWS_S

GM_CHUNK, GM_HEADS = 128, 8
FFN_DIM = 2816
N_EXPERTS, TOP_K, EXPERT_DIM = 8, 2, 3584
ALPHA = (2.0 * DEPTH) ** 0.25
LN_EPS = 1e-5

TM = 512
MOE_TM = 512
MOE_TF = 512
MOE_TILES = (N_ROWS * TOP_K) // MOE_TM + N_EXPERTS
MOE_ROWS = MOE_TILES * MOE_TM
MOE_UNITS = N_EXPERTS * (EXPERT_DIM // MOE_TF) + MOE_TILES
MOE_STREAM, MOE_CACHED, MOE_ZERO, MOE_IDLE = 1, 2, 0, 3
MOE_DRAIN = -2
MOE_AHEAD = 2
MOE_SLOTS = MOE_AHEAD + 1
VMEM_LIMIT = 56 * 1024 * 1024


def _cparams(sem):
    return pltpu.CompilerParams(dimension_semantics=sem, vmem_limit_bytes=VMEM_LIMIT)


def _cond_of_tile(t, tm):
    row0 = t * tm
    return jnp.where(row0 < N_P, 0, 1 + (row0 - N_P) // DEC_SEQ)


def _mod_spec(tm):
    return pl.BlockSpec((None, 6, D), lambda t: (_cond_of_tile(t, tm), 0, 0))


def _row_spec(tm, width=D):
    return pl.BlockSpec((tm, width), lambda t: (t, 0))


def _const_spec(shape):
    return pl.BlockSpec(shape, lambda t: (0,) * len(shape))


def _layer_spec(layer, shape, **kwargs):
    return pl.BlockSpec((None, *shape), lambda t: (layer,) + (0,) * len(shape), **kwargs)


def _layer_norm(r, g, b):
    mu = jnp.mean(r, axis=-1, keepdims=True)
    c = r - mu
    var = jnp.mean(c * c, axis=-1, keepdims=True)
    return c * lax.rsqrt(var + LN_EPS) * g + b


def _modulate(x, mod_ref, shift_row, scale_row):
    return x * (1.0 + mod_ref[scale_row:scale_row + 1, :]) + mod_ref[shift_row:shift_row + 1, :]


def _ada_kernel(c_ref, w_ref, b_ref, o_ref):
    c = c_ref[...]
    o_ref[...] = jnp.dot(jax.nn.silu(c), w_ref[...], preferred_element_type=F32) + b_ref[...]


def _ada(cond8, w_ada, b_ada):
    tn = 1536
    return pl.pallas_call(
        _ada_kernel,
        grid=(DEPTH, 6 * D // tn),
        in_specs=[pl.BlockSpec((N_COND, D), lambda i, n: (0, 0)),
                  pl.BlockSpec((None, D, tn), lambda i, n: (i, 0, n)),
                  pl.BlockSpec((None, 1, tn), lambda i, n: (i, 0, n))],
        out_specs=pl.BlockSpec((None, N_COND, tn), lambda i, n: (i, 0, n)),
        out_shape=jax.ShapeDtypeStruct((DEPTH, N_COND, 6 * D), F32),
        compiler_params=_cparams(("arbitrary", "arbitrary")),
        name="ada",
    )(cond8, w_ada, b_ada.reshape(DEPTH, 1, 6 * D))


def _prep_kernel(xp_ref, xs_ref, pos_ref, xo_ref):
    is_latent = pl.program_id(0) * TM >= N_P
    xo_ref[...] = jnp.where(is_latent, xs_ref[...] + pos_ref[...], xp_ref[...])


def _prep(x_prompt, x_sample, pos):
    p_tiles = N_P // TM

    def latent_tile(t):
        return jnp.maximum(t - p_tiles, 0)

    return pl.pallas_call(
        _prep_kernel,
        grid=(N_ROWS // TM,),
        in_specs=[pl.BlockSpec((TM, D), lambda t: (jnp.minimum(t, p_tiles - 1), 0)),
                  pl.BlockSpec((TM, D), lambda t: (latent_tile(t), 0)),
                  pl.BlockSpec((TM, D), lambda t: (latent_tile(t) % (DEC_SEQ // TM), 0))],
        out_specs=_row_spec(TM),
        out_shape=jax.ShapeDtypeStruct((N_ROWS, D), F32),
        compiler_params=_cparams(("parallel",)),
        name="prep",
    )(x_prompt.reshape(N_P, D), x_sample.reshape(N_S, D), pos)


S5_BLK = 4
S5_W = 2 * S5_T * S5_CG
S5_GR = 2 * S5_CG
_NT = (((1,), (1,)), ((), ()))


def _granule_transpose4(a, lane):
    low = lane < 64
    a0 = jnp.where(low, a[0], pltpu.roll(a[2], 64, 1))
    a2 = jnp.where(low, pltpu.roll(a[0], 64, 1), a[2])
    a1 = jnp.where(low, a[1], pltpu.roll(a[3], 64, 1))
    a3 = jnp.where(low, pltpu.roll(a[1], 64, 1), a[3])
    even = (lane // S5_GR) % 2 == 0
    return [jnp.where(even, a0, pltpu.roll(a1, 32, 1)), jnp.where(even, pltpu.roll(a0, 96, 1), a1),
            jnp.where(even, a2, pltpu.roll(a3, 32, 1)), jnp.where(even, pltpu.roll(a2, 96, 1), a3)]


def _s5_chunk_matrices(p4, bb_ref, c_ref, pw_ref, min_scr, mi_scr, p_scr):
    half = 2 * S5_P
    gran = lax.broadcasted_iota(jnp.int32, (S5_W, 128), 1) // S5_GR
    rep = (lax.broadcasted_iota(jnp.int32, (S5_GR, S5_W), 1) % S5_GR
           == lax.broadcasted_iota(jnp.int32, (S5_GR, S5_W), 0)).astype(BF16)
    mout = []
    for k in range(2):
        bb_re, bb_im = bb_ref[p4, k, 0], bb_ref[p4, k, 1]
        c_re, c_im = c_ref[p4, k, 0], c_ref[p4, k, 1]
        out_blocks = []
        for r in range(S5_T):
            t_in = S5_T - 1 - r if k == 0 else r
            t_out = r + 1 if k == 0 else S5_T - r
            rows = slice(S5_GR * r, S5_GR * (r + 1))
            pr, pi = pw_ref[p4, k, 0, t_in:t_in + 1, :], pw_ref[p4, k, 1, t_in:t_in + 1, :]
            min_scr[k, rows, 0:half] = bb_re * pr - bb_im * pi
            min_scr[k, rows, half:2 * half] = bb_re * pi + bb_im * pr
            pr, pi = pw_ref[p4, k, 0, t_out:t_out + 1, :], pw_ref[p4, k, 1, t_out:t_out + 1, :]
            out_blocks.append(jnp.concatenate([c_re * pr - c_im * pi, -(c_re * pi + c_im * pr)], axis=-1))
        mout.append(jnp.concatenate(out_blocks, axis=0).astype(BF16))
        c2 = jnp.concatenate([c_re, -c_im], axis=-1)
        c2_hi = c2.astype(BF16)
        c2_lo = (c2 - c2_hi.astype(F32)).astype(BF16)
        m = min_scr[k]
        m_hi = m.astype(BF16)
        m_lo = (m - m_hi.astype(F32)).astype(BF16)
        z = (lax.dot_general(m_hi, c2_hi, _NT, preferred_element_type=F32)
             + lax.dot_general(m_lo, c2_hi, _NT, preferred_element_type=F32)
             + lax.dot_general(m_hi, c2_lo, _NT, preferred_element_type=F32))
        zrep = jnp.dot(z.astype(BF16), rep, preferred_element_type=F32)
        zeros = jnp.zeros((S5_W, S5_W), F32)
        p_scr[0:S5_W, :] = zrep if k == 0 else zeros
        p_scr[S5_W:2 * S5_W, :] = zeros if k == 0 else zrep
        for j in range(S5_W // 128):
            cols = slice(128 * j, 128 * (j + 1))
            acc = None
            for s4 in range(4):
                s = 4 * j + s4
                start = S5_GR * (S5_T - 1 - s) if k == 0 else S5_W - S5_GR * s
                win = p_scr[start:start + S5_W, cols]
                acc = win if acc is None else jnp.where(gran == s4, win, acc)
            mi_scr[k, :, cols] = acc.astype(BF16)
    return mout


def _s5_scan(k, s_scrs, h_scrs, row0, nb, nchunks, hre, him, ar, ai):
    for n in (range(nchunks - 1, -1, -1) if k == 1 else range(nchunks)):
        rows = pl.ds(row0 + n, nb, stride=nchunks)
        h_scrs[0][rows, :] = hre
        h_scrs[1][rows, :] = him
        sre = s_scrs[0][rows, :]
        sim = s_scrs[1][rows, :]
        hre, him = ar * hre - ai * him + sre, ar * him + ai * hre + sim
    return hre, him


def _s5_kernel(x_ref, mod_ref, bb_ref, c_ref, pw_ref, h0_ref, y_ref, fin_ref,
               u_scr, yp_scr, sre_scr, sim_scr, hre_scr, him_scr, min_scr, mi_scr, p_scr):
    half = 2 * S5_P
    lane = lax.broadcasted_iota(jnp.int32, (S5_ROWS, 128), 1)
    scale = jnp.concatenate(
        [jnp.broadcast_to(1.0 + mod_ref[0, 1:2, :], (S5_ROWS_P, 128))]
        + [jnp.broadcast_to(1.0 + mod_ref[1 + b, 1:2, :], (S_CHUNKS, 128)) for b in range(DEC_BATCH)], axis=0)
    shift = jnp.concatenate(
        [jnp.broadcast_to(mod_ref[0, 0:1, :], (S5_ROWS_P, 128))]
        + [jnp.broadcast_to(mod_ref[1 + b, 0:1, :], (S_CHUNKS, 128)) for b in range(DEC_BATCH)], axis=0)
    for j in range(S5_T // 4):
        a = [x_ref[pl.ds(4 * j + s4, S5_ROWS, stride=S5_T), :] * scale + shift for s4 in range(4)]
        b = _granule_transpose4(a, lane)
        for p4 in range(S5_BLK):
            u_scr[p4, :, 128 * j:128 * (j + 1)] = b[p4].astype(BF16)

    for p4 in range(S5_BLK):
        mout = _s5_chunk_matrices(p4, bb_ref, c_ref, pw_ref, min_scr, mi_scr, p_scr)
        u = u_scr[p4]
        u0, u1 = u[:, 0:256], u[:, 256:512]
        y = None
        for k in range(2):
            s = jnp.dot(u, min_scr[k].astype(BF16), preferred_element_type=F32)
            sre_scr[...] = s[:, 0:half]
            sim_scr[...] = s[:, half:2 * half]
            s_scrs, h_scrs = (sre_scr, sim_scr), (hre_scr, him_scr)
            ar_p = jnp.broadcast_to(pw_ref[p4, k, 0, S5_T:S5_T + 1, :], (BATCH, half))
            ai_p = jnp.broadcast_to(pw_ref[p4, k, 1, S5_T:S5_T + 1, :], (BATCH, half))
            zero = jnp.zeros((BATCH, half), F32)
            fre, fim = _s5_scan(k, s_scrs, h_scrs, 0, BATCH, P_CHUNKS, zero, zero, ar_p, ai_p)
            fin_ref[p4, k, :, 0:half] = fre
            fin_ref[p4, k, :, half:2 * half] = fim
            _s5_scan(k, s_scrs, h_scrs, S5_ROWS_P, DEC_BATCH, S_CHUNKS,
                     h0_ref[p4, k, :, 0:half], h0_ref[p4, k, :, half:2 * half], ar_p[:DEC_BATCH], ai_p[:DEC_BATCH])
            hin = jnp.concatenate([hre_scr[...], him_scr[...]], axis=-1).astype(BF16)
            readout = lax.dot_general(hin, mout[k], _NT, preferred_element_type=F32)
            m00 = jnp.dot(u0, mi_scr[k, 0:256, 0:256], preferred_element_type=F32)
            m11 = jnp.dot(u1, mi_scr[k, 256:512, 256:512], preferred_element_type=F32)
            if k == 0:
                m11 = m11 + jnp.dot(u0, mi_scr[k, 0:256, 256:512], preferred_element_type=F32)
            else:
                m00 = m00 + jnp.dot(u1, mi_scr[k, 256:512, 0:256], preferred_element_type=F32)
            yk = jnp.concatenate([m00, m11], axis=-1) + readout
            y = yk if y is None else y + yk
        yp_scr[p4] = y

    for j in range(S5_T // 4):
        b = [yp_scr[p4, :, 128 * j:128 * (j + 1)] for p4 in range(S5_BLK)]
        a = _granule_transpose4(b, lane)
        for s4 in range(4):
            y_ref[pl.ds(4 * j + s4, S5_ROWS, stride=S5_T), :] = a[s4]


def _s5(x, mods, params, h0):
    bb, cc, pw = params
    nblk = S5_Q // S5_BLK
    return pl.pallas_call(
        _s5_kernel,
        grid=(nblk,),
        in_specs=[pl.BlockSpec((N_ROWS, 128), lambda q: (0, q)),
                  pl.BlockSpec((N_COND, 6, 128), lambda q: (0, 0, q)),
                  pl.BlockSpec((S5_BLK, 2, 2, S5_GR, 128), lambda q: (q, 0, 0, 0, 0)),
                  pl.BlockSpec((S5_BLK, 2, 2, S5_GR, 128), lambda q: (q, 0, 0, 0, 0)),
                  pl.BlockSpec((S5_BLK, 2, 2, S5_T + 1, 128), lambda q: (q, 0, 0, 0, 0)),
                  pl.BlockSpec((S5_BLK, 2, DEC_BATCH, 256), lambda q: (q, 0, 0, 0))],
        out_specs=[pl.BlockSpec((N_ROWS, 128), lambda q: (0, q)),
                   pl.BlockSpec((S5_BLK, 2, BATCH, 256), lambda q: (q, 0, 0, 0))],
        out_shape=[jax.ShapeDtypeStruct((N_ROWS, D), F32),
                   jax.ShapeDtypeStruct((S5_Q, 2, BATCH, 256), F32)],
        scratch_shapes=[pltpu.VMEM((S5_BLK, S5_ROWS, S5_W), BF16), pltpu.VMEM((S5_BLK, S5_ROWS, S5_W), F32),
                        pltpu.VMEM((S5_ROWS, 128), F32), pltpu.VMEM((S5_ROWS, 128), F32),
                        pltpu.VMEM((S5_ROWS, 128), F32), pltpu.VMEM((S5_ROWS, 128), F32),
                        pltpu.VMEM((2, S5_W, 256), F32),
                        pltpu.VMEM((2, S5_W, S5_W), BF16), pltpu.VMEM((2 * S5_W, S5_W), F32)],
        compiler_params=_cparams(("parallel",)),
        name="s5",
    )(x, mods, bb, cc, pw, h0)


def _s5_params(a_re, a_im, log_step, b_re, b_im, c_re, c_im):
    lam_re = jnp.minimum(a_re.astype(F32), -1e-4)
    lam_im = a_im.astype(F32)
    dt = jnp.exp(log_step.astype(F32))[..., None]
    j = jnp.arange(S5_T + 1, dtype=F32)[:, None, None, None]
    mag = jnp.exp(lam_re * dt * j)
    pw_re = mag * jnp.cos(lam_im * dt * j)
    pw_im = mag * jnp.sin(lam_im * dt * j)
    nr, ni = pw_re[1] - 1.0, pw_im[1]
    den = lam_re * lam_re + lam_im * lam_im
    q_re = (nr * lam_re + ni * lam_im) / den
    q_im = (ni * lam_re - nr * lam_im) / den
    b_re, b_im = b_re.astype(F32), b_im.astype(F32)
    bb_re = q_re[..., None] * b_re - q_im[..., None] * b_im
    bb_im = q_re[..., None] * b_im + q_im[..., None] * b_re
    eye = jnp.eye(2, dtype=F32)

    def pair_rows(m, spec):
        m = m.reshape(2, S5_Q, 2, *m.shape[2:])
        return jnp.einsum(spec, m, eye).reshape(S5_Q, 2, S5_GR, 2 * S5_P)

    bb = jnp.stack([pair_rows(bb_re, "kqgpc,gh->qkgchp"), pair_rows(bb_im, "kqgpc,gh->qkgchp")], axis=2)
    cc = jnp.stack([pair_rows(c_re.astype(F32), "kqgcp,gh->qkgchp"),
                    pair_rows(c_im.astype(F32), "kqgcp,gh->qkgchp")], axis=2)

    def pw_lay(p):
        return p.reshape(S5_T + 1, 2, S5_Q, 2 * S5_P).transpose(2, 1, 0, 3)
    pw = jnp.stack([pw_lay(pw_re), pw_lay(pw_im)], axis=2)
    return bb, cc, pw


def _s5_state_in(st_re, st_im):
    def lay(s):
        return s.astype(F32).reshape(DEC_BATCH, 2, S5_Q, 2 * S5_P).transpose(2, 1, 0, 3)
    return jnp.concatenate([lay(st_re), lay(st_im)], axis=-1)


def _s5_state_out(fin):
    def lay(s):
        return s.transpose(2, 1, 0, 3).reshape(BATCH, 2, S5_G, S5_P)
    return lay(fin[..., :2 * S5_P]), lay(fin[..., 2 * S5_P:])


def _glu_kernel(ys_ref, x_ref, mod_ref, d_ref, w_ref, g_ref, b_ref, x1_ref, h2_ref):
    x = x_ref[...]
    h = _modulate(x, mod_ref, 0, 1)
    y = jax.nn.gelu(ys_ref[...] + d_ref[...] * h).astype(BF16)
    z = jnp.dot(y, w_ref[...], preferred_element_type=F32)
    f = z[:, :D] * jax.nn.sigmoid(z[:, D:])
    x1 = _layer_norm(ALPHA * x + (1.0 + mod_ref[2:3, :]) * f, g_ref[...], b_ref[...])
    x1_ref[...] = x1
    h2_ref[...] = _modulate(x1, mod_ref, 3, 4).astype(BF16)


def _glu(layer, ysum, x, mod, d, w_glu, ln_g, ln_b):
    return pl.pallas_call(
        _glu_kernel,
        grid=(N_ROWS // TM,),
        in_specs=[_row_spec(TM), _row_spec(TM), _mod_spec(TM), _const_spec((1, D)),
                  _layer_spec(layer, (D, 2 * D)), _const_spec((1, D)), _const_spec((1, D))],
        out_specs=[_row_spec(TM), _row_spec(TM)],
        out_shape=[jax.ShapeDtypeStruct((N_ROWS, D), F32), jax.ShapeDtypeStruct((N_ROWS, D), BF16)],
        compiler_params=_cparams(("parallel",)),
        name="glu",
    )(ysum, x, mod, d.reshape(1, D), w_glu, ln_g.reshape(1, D), ln_b.reshape(1, D))


def _finish(x1, g, mod_ref, g_ref, b_ref, x2_ref):
    x2 = _layer_norm(ALPHA * x1 + (1.0 + mod_ref[5:6, :]) * g, g_ref[...], b_ref[...])
    x2_ref[...] = x2
    return x2


FFN_TF = FFN_DIM // 2


def _ffn_kernel(h_ref, x1_ref, mod_ref, nmod_ref, wg_ref, wu_ref, wd_ref, g_ref, b_ref, x2_ref, hn_ref):
    h = h_ref[...]
    acc = None
    for c in range(FFN_DIM // FFN_TF):
        sl = slice(c * FFN_TF, (c + 1) * FFN_TF)
        a = jnp.dot(h, wg_ref[:, sl], preferred_element_type=F32)
        b = jnp.dot(h, wu_ref[:, sl], preferred_element_type=F32)
        p = jnp.dot((jax.nn.silu(a) * b).astype(BF16), wd_ref[sl, :], preferred_element_type=F32)
        acc = p if acc is None else acc + p
    x2 = _finish(x1_ref[...], acc, mod_ref, g_ref, b_ref, x2_ref)
    hn_ref[...] = _modulate(x2, nmod_ref, 0, 1).astype(BF16)


def _ffn(layer, h2, x1, mod, nmod, wg, wu, wd, ln_g, ln_b):
    single = pl.Buffered(1)
    return pl.pallas_call(
        _ffn_kernel,
        grid=(N_ROWS // TM,),
        in_specs=[_row_spec(TM), _row_spec(TM), _mod_spec(TM), _mod_spec(TM),
                  _layer_spec(layer, (D, FFN_DIM), pipeline_mode=single),
                  _layer_spec(layer, (D, FFN_DIM), pipeline_mode=single),
                  _layer_spec(layer, (FFN_DIM, D), pipeline_mode=single),
                  _const_spec((1, D)), _const_spec((1, D))],
        out_specs=[_row_spec(TM), _row_spec(TM)],
        out_shape=[jax.ShapeDtypeStruct((N_ROWS, D), F32), jax.ShapeDtypeStruct((N_ROWS, D), BF16)],
        compiler_params=_cparams(("parallel",)),
        name="ffn",
    )(h2, x1, mod, nmod, wg, wu, wd, ln_g.reshape(1, D), ln_b.reshape(1, D))


def _gmlp_kernel(h_ref, x_ref, mod_ref, win_ref, bin_ref, vg_ref, vb_ref, ws_ref, bs_ref, wout_ref,
                 g_ref, b_ref, wr_ref, br_ref, x1_ref, h2_ref, idx_ref, p_ref, s_scr):
    z = jax.nn.gelu(jnp.dot(h_ref[...], win_ref[...], preferred_element_type=F32) + bin_ref[...])
    u = z[:, :D]
    v = _layer_norm(z[:, D:], vg_ref[...], vb_ref[...]).astype(BF16)
    n_chunks = TM // GM_CHUNK
    for hd in range(GM_HEADS):
        cols = slice(hd * GM_CHUNK, (hd + 1) * GM_CHUNK)
        rhs = jnp.concatenate([v[n * GM_CHUNK:(n + 1) * GM_CHUNK, cols] for n in range(n_chunks)], axis=-1)
        s = jnp.dot(ws_ref[hd], rhs, preferred_element_type=F32)
        for n in range(n_chunks):
            s_scr[n * GM_CHUNK:(n + 1) * GM_CHUNK, cols] = s[:, n * GM_CHUNK:(n + 1) * GM_CHUNK] + bs_ref[hd]
    f = jnp.dot((u * s_scr[...]).astype(BF16), wout_ref[...], preferred_element_type=F32)
    x1 = _layer_norm(ALPHA * x_ref[...] + (1.0 + mod_ref[2:3, :]) * f, g_ref[...], b_ref[...])
    x1_ref[...] = x1
    h2 = _modulate(x1, mod_ref, 3, 4)
    h2_ref[...] = h2
    h_hi = h2.astype(BF16)
    h_lo = (h2 - h_hi.astype(F32)).astype(BF16)
    w_hi, w_lo = wr_ref[0], wr_ref[1]
    logits = (jnp.dot(h_hi, w_hi, preferred_element_type=F32) + jnp.dot(h_lo, w_hi, preferred_element_type=F32)
              + jnp.dot(h_hi, w_lo, preferred_element_type=F32)) + br_ref[...]
    lane = lax.broadcasted_iota(jnp.int32, logits.shape, 1)
    neg = jnp.float32(-jnp.inf)
    logits = jnp.where(lane < N_EXPERTS, logits, neg)
    v1 = jnp.max(logits, axis=-1, keepdims=True)
    i1 = jnp.min(jnp.where(logits == v1, lane, 128), axis=-1, keepdims=True)
    rest = jnp.where(lane == i1, neg, logits)
    v2 = jnp.max(rest, axis=-1, keepdims=True)
    i2 = jnp.min(jnp.where(rest == v2, lane, 128), axis=-1, keepdims=True)
    e2 = jnp.exp(v2 - v1)
    p1 = 1.0 / (1.0 + e2)
    p2 = e2 / (1.0 + e2)
    idx_ref[...] = jnp.where(lane == 0, i1, jnp.where(lane == 1, i2, 0))
    p_ref[...] = jnp.where(lane == 0, p1, jnp.where(lane == 1, p2, 0.0))


def _gmlp(layer, h1, x, mod, w_in, b_in, vg, vb, w_s, b_s, w_out, ln_g, ln_b, w_r, b_r):
    return pl.pallas_call(
        _gmlp_kernel,
        grid=(N_ROWS // TM,),
        in_specs=[_row_spec(TM), _row_spec(TM), _mod_spec(TM),
                  _layer_spec(layer, (D, 2 * D)), _const_spec((1, 2 * D)), _const_spec((1, D)), _const_spec((1, D)),
                  _layer_spec(layer, (GM_HEADS, GM_CHUNK, GM_CHUNK)), _const_spec((GM_HEADS, GM_CHUNK, GM_CHUNK)),
                  _layer_spec(layer, (D, D)), _const_spec((1, D)), _const_spec((1, D)),
                  _const_spec((2, D, 128)), _const_spec((1, 128))],
        out_specs=[_row_spec(TM), _row_spec(TM), _row_spec(TM, 128), _row_spec(TM, 128)],
        out_shape=[jax.ShapeDtypeStruct((N_ROWS, D), F32), jax.ShapeDtypeStruct((N_ROWS, D), F32),
                   jax.ShapeDtypeStruct((N_ROWS, 128), jnp.int32), jax.ShapeDtypeStruct((N_ROWS, 128), F32)],
        scratch_shapes=[pltpu.VMEM((TM, D), F32)],
        compiler_params=_cparams(("parallel",)),
        name="gmlp",
    )(h1, x, mod, w_in, b_in.reshape(1, 2 * D), vg.reshape(1, D), vb.reshape(1, D), w_s, b_s, w_out,
      ln_g.reshape(1, D), ln_b.reshape(1, D), w_r, b_r)


def _moe_kernel(kind_ref, chunk_ref, ux_ref, uo_ref, ue_ref, pref_ref, src_ref, h_hbm, wg_ref, wu_ref, wd_ref, y_ref,
                acc_ref, xg_ref, sem, xb_ref, act_ref, wgb, wub, wdb):
    u = pl.program_id(0)
    nf = EXPERT_DIM // MOE_TF
    kind = kind_ref[u]
    c = chunk_ref[u]
    tile = ux_ref[u]
    slot = tile % MOE_SLOTS
    ahead_slot = (tile + MOE_AHEAD) % MOE_SLOTS
    begins_tile = jnp.logical_or(kind == MOE_CACHED, jnp.logical_and(kind == MOE_STREAM, c == 0))

    def row_copy(t, i, dst_slot):
        row = src_ref[t * MOE_TM + i]
        return pltpu.make_async_copy(h_hbm.at[pl.ds(row, 1), :], xg_ref.at[dst_slot, pl.ds(i, 1), :], sem.at[dst_slot])

    @pl.when(u == 0)
    def _():
        for t in range(MOE_AHEAD):
            def start_row(i, carry, t=t):
                row_copy(t, i, t).start()
                return carry
            lax.fori_loop(0, MOE_TM, start_row, 0)

    def wait_rows(dst_slot):
        pltpu.make_async_copy(h_hbm.at[pl.ds(0, MOE_TM), :], xg_ref.at[dst_slot], sem.at[dst_slot]).wait()

    def start_tile_ahead():
        nxt = pref_ref[u]
        for i in range(MOE_TM):
            row_copy(nxt, i, ahead_slot).start()

    @pl.when(begins_tile)
    def _():
        wait_rows(slot)

    @pl.when(pref_ref[u] == MOE_DRAIN)
    def _():
        for k in range(1, MOE_AHEAD + 1):
            wait_rows((tile + k) % MOE_SLOTS)

    def swiglu_chunk(x, k):
        a = jnp.dot(x, wgb[k], preferred_element_type=F32)
        b = jnp.dot(x, wub[k], preferred_element_type=F32)
        return (jax.nn.silu(a) * b).astype(BF16)

    @pl.when(kind == MOE_STREAM)
    def _():
        rows_c = pl.ds(pl.multiple_of(c * MOE_TF, MOE_TF), MOE_TF)
        wgb[c] = wg_ref[...].astype(BF16)
        wub[c] = wu_ref[...].astype(BF16)
        wdb[rows_c, :] = wd_ref[...].astype(BF16)

        @pl.when(c == 0)
        def _():
            xb_ref[...] = xg_ref[slot].astype(BF16)
            start_tile_ahead()

        p = jnp.dot(swiglu_chunk(xb_ref[...], c), wdb[rows_c, :], preferred_element_type=F32)
        @pl.when(c == 0)
        def _():
            acc_ref[...] = p

        @pl.when(jnp.logical_and(c > 0, c < nf - 1))
        def _():
            acc_ref[...] += p

        @pl.when(c == nf - 1)
        def _():
            y_ref[...] = acc_ref[...] + p

    @pl.when(kind == MOE_CACHED)
    def _():
        x = xg_ref[slot].astype(BF16)
        start_tile_ahead()
        for k in range(nf):
            act_ref[:, k * MOE_TF:(k + 1) * MOE_TF] = swiglu_chunk(x, k)
        y_ref[...] = jnp.dot(act_ref[...], wdb[...], preferred_element_type=F32)

    @pl.when(kind == MOE_ZERO)
    def _():
        y_ref[...] = jnp.zeros_like(y_ref)


def _moe(layer, units, src, h, wg, wu, wd):
    nf = EXPERT_DIM // MOE_TF

    def w_spec(shape, index):
        return pl.BlockSpec(shape, lambda u, kind, chunk, ux, uo, ue, pref, src_: index(ue[u], chunk[u]))

    grid_spec = pltpu.PrefetchScalarGridSpec(
        num_scalar_prefetch=7,
        grid=(MOE_UNITS,),
        in_specs=[pl.BlockSpec(memory_space=pl.ANY),
                  w_spec((None, None, D, MOE_TF), lambda e, k: (layer, e, 0, k)),
                  w_spec((None, None, D, MOE_TF), lambda e, k: (layer, e, 0, k)),
                  w_spec((None, None, MOE_TF, D), lambda e, k: (layer, e, k, 0))],
        out_specs=pl.BlockSpec((MOE_TM, D), lambda u, kind, chunk, ux, uo, ue, pref, src_: (uo[u], 0)),
        scratch_shapes=[pltpu.VMEM((MOE_TM, D), F32), pltpu.VMEM((MOE_SLOTS, MOE_TM, D), F32),
                        pltpu.SemaphoreType.DMA((MOE_SLOTS,)), pltpu.VMEM((MOE_TM, D), BF16),
                        pltpu.VMEM((MOE_TM, EXPERT_DIM), BF16),
                        pltpu.VMEM((nf, D, MOE_TF), BF16), pltpu.VMEM((nf, D, MOE_TF), BF16),
                        pltpu.VMEM((EXPERT_DIM, D), BF16)],
    )
    return pl.pallas_call(
        _moe_kernel,
        grid_spec=grid_spec,
        out_shape=jax.ShapeDtypeStruct((MOE_ROWS, D), F32),
        compiler_params=_cparams(("arbitrary",)),
        name="moe",
    )(*units, src, h, wg, wu, wd)


def _take_rows(a, rows):
    return a.at[rows].get(mode="promise_in_bounds")


def _route(idx):
    e = idx.reshape(-1)
    onehot = (e[:, None] == jnp.arange(N_EXPERTS, dtype=jnp.int32)[None, :]).astype(jnp.int32)
    counts = jnp.sum(onehot, axis=0)
    rank = jnp.sum((jnp.cumsum(onehot, axis=0) - 1) * onehot, axis=1)
    padded = ((counts + MOE_TM - 1) // MOE_TM) * MOE_TM
    ends = jnp.cumsum(padded)
    starts = ends - padded
    pos = starts[e] + rank
    token = jnp.arange(N_ROWS * TOP_K, dtype=jnp.int32) // TOP_K
    src = jnp.zeros((MOE_ROWS,), jnp.int32).at[pos].set(token, unique_indices=True, mode="promise_in_bounds")
    n_used = (ends[-1] // MOE_TM).astype(jnp.int32)
    tile_start = jnp.minimum(jnp.arange(MOE_TILES, dtype=jnp.int32), n_used - 1) * MOE_TM
    tile_expert = jnp.sum((tile_start[:, None] >= ends[None, :]).astype(jnp.int32), axis=1)
    tile_expert = jnp.minimum(tile_expert, N_EXPERTS - 1).astype(jnp.int32)
    prev = jnp.concatenate([jnp.full((1,), -1, jnp.int32), tile_expert[:-1]])
    tile_first = tile_expert != prev
    nf = EXPERT_DIM // MOE_TF
    tiles = jnp.arange(MOE_TILES, dtype=jnp.int32)
    live = tiles < n_used
    cost = jnp.where(live, jnp.where(tile_first, nf, 1), 0).astype(jnp.int32)
    off = jnp.cumsum(cost) - cost
    n_units = jnp.sum(cost)
    u = jnp.arange(MOE_UNITS, dtype=jnp.int32)
    tile_u = jnp.sum(jnp.logical_and(off[None, :] <= u[:, None], live[None, :]).astype(jnp.int32), axis=1) - 1
    first_u = tile_first[tile_u]
    work = u < n_units
    spare = u - n_units
    unit_kind = jnp.where(work, jnp.where(first_u, MOE_STREAM, MOE_CACHED),
                          jnp.where(spare < MOE_TILES - n_used, MOE_ZERO, MOE_IDLE)).astype(jnp.int32)
    unit_chunk = jnp.where(jnp.logical_and(work, first_u), u - off[tile_u], nf - 1).astype(jnp.int32)
    unit_out = jnp.where(work, tile_u, jnp.minimum(n_used + spare, MOE_TILES - 1)).astype(jnp.int32)
    unit_x = tile_u.astype(jnp.int32)
    unit_expert = tile_expert[tile_u]
    unit_pref = jnp.where(work, jnp.minimum(tile_u + MOE_AHEAD, n_used - 1),
                          jnp.where(spare == 0, MOE_DRAIN, -1)).astype(jnp.int32)
    return src, pos.reshape(N_ROWS, TOP_K), (unit_kind, unit_chunk, unit_x, unit_out, unit_expert, unit_pref)


def _combine_kernel(ya_ref, yb_ref, p_ref, x1_ref, mod_ref, g_ref, b_ref, *x2_refs):
    g = p_ref[:, 0:1] * ya_ref[...] + p_ref[:, 1:2] * yb_ref[...]
    x2 = _layer_norm(ALPHA * x1_ref[...] + (1.0 + mod_ref[5:6, :]) * g, g_ref[...], b_ref[...])
    if len(x2_refs) == 1:
        x2_refs[0][...] = x2
    else:
        is_latent = pl.program_id(0) * TM >= N_P

        @pl.when(jnp.logical_not(is_latent))
        def _():
            x2_refs[0][...] = x2

        @pl.when(is_latent)
        def _():
            x2_refs[1][...] = x2


def _combine(ya, yb, p, x1, mod, ln_g, ln_b, split=False):
    p_tiles = N_P // TM
    if split:
        out_specs = [pl.BlockSpec((TM, D), lambda t: (jnp.minimum(t, p_tiles - 1), 0)),
                     pl.BlockSpec((TM, D), lambda t: (jnp.maximum(t - p_tiles, 0), 0))]
        out_shape = [jax.ShapeDtypeStruct((N_P, D), F32), jax.ShapeDtypeStruct((N_S, D), F32)]
    else:
        out_specs, out_shape = _row_spec(TM), jax.ShapeDtypeStruct((N_ROWS, D), F32)
    return pl.pallas_call(
        _combine_kernel,
        grid=(N_ROWS // TM,),
        in_specs=[_row_spec(TM), _row_spec(TM), _row_spec(TM, 128), _row_spec(TM), _mod_spec(TM),
                  _const_spec((1, D)), _const_spec((1, D))],
        out_specs=out_specs,
        out_shape=out_shape,
        compiler_params=_cparams(("arbitrary",) if split else ("parallel",)),
        name="combine",
    )(ya, yb, p, x1, mod, ln_g.reshape(1, D), ln_b.reshape(1, D))


def _grid_pos_embed():
    rows = DEC_SEQ // GRID_W
    r, col = jnp.meshgrid(jnp.arange(rows, dtype=F32), jnp.arange(GRID_W, dtype=F32), indexing="ij")
    quarter = D // 4
    freq = 1.0 / (10000.0 ** (jnp.arange(quarter, dtype=F32) / quarter))

    def emb(p):
        ang = p.reshape(-1)[:, None] * freq[None, :]
        return jnp.concatenate([jnp.sin(ang), jnp.cos(ang)], axis=-1)

    return jnp.concatenate([emb(r), emb(col)], axis=-1)


def kernel(x_prompt, x_sample, c, state_ssm_re, state_ssm_im, c_ctx, w_ada, b_ada, ln_g, ln_b, s5_a_re, s5_a_im, s5_log_step, s5_b_re, s5_b_im, s5_c_re, s5_c_im, s5_d, s5_w_glu, gm_w_in, gm_b_in, gm_ln_g, gm_ln_b, gm_w_s, gm_b_s, gm_w_out, ffn_w_gate, ffn_w_up, ffn_w_down, moe_w_router, moe_b_router, moe_w_gate, moe_w_up, moe_w_down):
    cond8 = jnp.concatenate([c_ctx[None, :], c, jnp.zeros((N_COND - 1 - DEC_BATCH, D), F32)], axis=0)
    mods = _ada(cond8, w_ada, b_ada).reshape(DEPTH, N_COND, 6, D)

    x = _prep(x_prompt, x_sample, _grid_pos_embed())
    h = None
    w_glu, w_gate, w_up, w_down = (w.astype(BF16) for w in (s5_w_glu, ffn_w_gate, ffn_w_up, ffn_w_down))
    w_in, w_s, w_out = (w.astype(BF16) for w in (gm_w_in, gm_w_s, gm_w_out))

    fin_re, fin_im = [], []
    for i in range(DEPTH):
        j = i // 2
        mod = mods[i]
        nmod = mods[min(i + 1, DEPTH - 1)]
        if i % 2 == 0:
            params = _s5_params(s5_a_re[j], s5_a_im[j], s5_log_step[j], s5_b_re[j], s5_b_im[j], s5_c_re[j], s5_c_im[j])
            ysum, fin = _s5(x, mod, params, _s5_state_in(state_ssm_re[:, j], state_ssm_im[:, j]))
            fr, fi = _s5_state_out(fin)
            fin_re.append(fr)
            fin_im.append(fi)
            x1, h2 = _glu(j, ysum, x, mod, s5_d[j], w_glu, ln_g[i, 0], ln_b[i, 0])
            x, h = _ffn(j, h2, x1, mod, nmod, w_gate, w_up, w_down, ln_g[i, 1], ln_b[i, 1])
        else:
            b_s = jnp.broadcast_to(gm_b_s[j][:, :, None], (GM_HEADS, GM_CHUNK, GM_CHUNK)).astype(F32)
            w_r = jnp.pad(moe_w_router[j].astype(F32), ((0, 0), (0, 128 - N_EXPERTS)))
            w_r_hi = w_r.astype(BF16)
            w_r = jnp.stack([w_r_hi, (w_r - w_r_hi.astype(F32)).astype(BF16)], axis=0)
            b_r = jnp.pad(moe_b_router[j].astype(F32), (0, 128 - N_EXPERTS)).reshape(1, 128)
            x1, h2, idx, p = _gmlp(j, h, x, mod, w_in, gm_b_in[j], gm_ln_g[j], gm_ln_b[j],
                                   w_s, b_s, w_out, ln_g[i, 0], ln_b[i, 0], w_r, b_r)
            src, pos, units = _route(idx[:, :TOP_K])
            ys = _moe(j, units, src, h2, moe_w_gate, moe_w_up, moe_w_down)
            x = _combine(_take_rows(ys, pos[:, 0]), _take_rows(ys, pos[:, 1]), p, x1, mod, ln_g[i, 1], ln_b[i, 1],
                         split=(i == DEPTH - 1))

    y_prompt = x[0].reshape(BATCH, SEQ, D)
    y_sample = x[1].reshape(DEC_BATCH, DEC_SEQ, D)
    new_re = jnp.stack(fin_re, axis=1).astype(x_prompt.dtype)
    new_im = jnp.stack(fin_im, axis=1).astype(x_prompt.dtype)
    return (y_prompt, y_sample, new_re, new_im)
```

```python
import jax
import jax.numpy as jnp
from jax import lax
from jax.experimental import pallas as pl
from jax.experimental.pallas import tpu as pltpu

F32 = jnp.float32
BF16 = jnp.bfloat16

D = 1024
BATCH, SEQ = 32, 256
DEC_BATCH, DEC_SEQ = 4, 1024
DEPTH = 4
GRID_W = 64
N_P = BATCH * SEQ
N_S = DEC_BATCH * DEC_SEQ
N_ROWS = N_P + N_S
N_COND = 8

S5_CG, S5_P, S5_G = 16, 64, 64
S5_T = 16
S5_Q = S5_G // 2
P_CHUNKS = SEQ // S5_T
S_CHUNKS = DEC_SEQ // S5_T
S5_ROWS_P = P_CHUNKS * BATCH
S5_ROWS_S = S_CHUNKS * DEC_BATCH
S5_ROWS = S5_ROWS_P + S5_ROWS_S

GM_CHUNK, GM_HEADS = 128, 8
FFN_DIM = 2816
N_EXPERTS, TOP_K, EXPERT_DIM = 8, 2, 3584
ALPHA = (2.0 * DEPTH) ** 0.25
LN_EPS = 1e-5

TM = 512
MOE_TM = 512
MOE_TF = 512
MOE_TILES = (N_ROWS * TOP_K) // MOE_TM + N_EXPERTS
MOE_ROWS = MOE_TILES * MOE_TM
MOE_UNITS = N_EXPERTS * (EXPERT_DIM // MOE_TF) + MOE_TILES
MOE_STREAM, MOE_CACHED, MOE_ZERO, MOE_IDLE = 1, 2, 0, 3
MOE_DRAIN = -2
MOE_AHEAD = 2
MOE_SLOTS = MOE_AHEAD + 1
YT_REAL = N_ROWS * TOP_K
YT_ROWS = YT_REAL + 2 * MOE_TM
VMEM_LIMIT = 56 * 1024 * 1024


def _cparams(sem):
    return pltpu.CompilerParams(dimension_semantics=sem, vmem_limit_bytes=VMEM_LIMIT)


def _cond_of_tile(t, tm):
    row0 = t * tm
    return jnp.where(row0 < N_P, 0, 1 + (row0 - N_P) // DEC_SEQ)


def _mod_spec(tm):
    return pl.BlockSpec((None, 6, D), lambda t: (_cond_of_tile(t, tm), 0, 0))


def _row_spec(tm, width=D):
    return pl.BlockSpec((tm, width), lambda t: (t, 0))


def _const_spec(shape):
    return pl.BlockSpec(shape, lambda t: (0,) * len(shape))


def _layer_spec(layer, shape, **kwargs):
    return pl.BlockSpec((None, *shape), lambda t: (layer,) + (0,) * len(shape), **kwargs)


def _layer_norm(r, g, b):
    mu = jnp.mean(r, axis=-1, keepdims=True)
    c = r - mu
    var = jnp.mean(c * c, axis=-1, keepdims=True)
    return c * lax.rsqrt(var + LN_EPS) * g + b


def _modulate(x, mod_ref, shift_row, scale_row):
    return x * (1.0 + mod_ref[scale_row:scale_row + 1, :]) + mod_ref[shift_row:shift_row + 1, :]


def _ada_kernel(c_ref, w_ref, b_ref, o_ref):
    c = c_ref[...]
    o_ref[...] = jnp.dot(jax.nn.silu(c), w_ref[...], preferred_element_type=F32) + b_ref[...]


def _ada(cond8, w_ada, b_ada):
    tn = 1536
    return pl.pallas_call(
        _ada_kernel,
        grid=(DEPTH, 6 * D // tn),
        in_specs=[pl.BlockSpec((N_COND, D), lambda i, n: (0, 0)),
                  pl.BlockSpec((None, D, tn), lambda i, n: (i, 0, n)),
                  pl.BlockSpec((None, 1, tn), lambda i, n: (i, 0, n))],
        out_specs=pl.BlockSpec((None, N_COND, tn), lambda i, n: (i, 0, n)),
        out_shape=jax.ShapeDtypeStruct((DEPTH, N_COND, 6 * D), F32),
        compiler_params=_cparams(("arbitrary", "arbitrary")),
        name="ada",
    )(cond8, w_ada, b_ada.reshape(DEPTH, 1, 6 * D))


def _prep_kernel(xp_ref, xs_ref, pos_ref, xo_ref):
    is_latent = pl.program_id(0) * TM >= N_P
    xo_ref[...] = jnp.where(is_latent, xs_ref[...] + pos_ref[...], xp_ref[...])


def _prep(x_prompt, x_sample, pos):
    p_tiles = N_P // TM

    def latent_tile(t):
        return jnp.maximum(t - p_tiles, 0)

    return pl.pallas_call(
        _prep_kernel,
        grid=(N_ROWS // TM,),
        in_specs=[pl.BlockSpec((TM, D), lambda t: (jnp.minimum(t, p_tiles - 1), 0)),
                  pl.BlockSpec((TM, D), lambda t: (latent_tile(t), 0)),
                  pl.BlockSpec((TM, D), lambda t: (latent_tile(t) % (DEC_SEQ // TM), 0))],
        out_specs=_row_spec(TM),
        out_shape=jax.ShapeDtypeStruct((N_ROWS, D), F32),
        compiler_params=_cparams(("parallel",)),
        name="prep",
    )(x_prompt.reshape(N_P, D), x_sample.reshape(N_S, D), pos)


S5_BLK = 4
S5_W = 2 * S5_T * S5_CG
S5_GR = 2 * S5_CG
_NT = (((1,), (1,)), ((), ()))


def _granule_transpose4(a, lane):
    low = lane < 64
    a0 = jnp.where(low, a[0], pltpu.roll(a[2], 64, 1))
    a2 = jnp.where(low, pltpu.roll(a[0], 64, 1), a[2])
    a1 = jnp.where(low, a[1], pltpu.roll(a[3], 64, 1))
    a3 = jnp.where(low, pltpu.roll(a[1], 64, 1), a[3])
    even = (lane // S5_GR) % 2 == 0
    return [jnp.where(even, a0, pltpu.roll(a1, 32, 1)), jnp.where(even, pltpu.roll(a0, 96, 1), a1),
            jnp.where(even, a2, pltpu.roll(a3, 32, 1)), jnp.where(even, pltpu.roll(a2, 96, 1), a3)]


def _s5_chunk_matrices(p4, bb_ref, c_ref, pw_ref, min_scr, mi_scr, p_scr):
    half = 2 * S5_P
    gran = lax.broadcasted_iota(jnp.int32, (S5_W, 128), 1) // S5_GR
    rep = (lax.broadcasted_iota(jnp.int32, (S5_GR, S5_W), 1) % S5_GR
           == lax.broadcasted_iota(jnp.int32, (S5_GR, S5_W), 0)).astype(BF16)
    mout = []
    for k in range(2):
        bb_re, bb_im = bb_ref[p4, k, 0], bb_ref[p4, k, 1]
        c_re, c_im = c_ref[p4, k, 0], c_ref[p4, k, 1]
        out_blocks = []
        for r in range(S5_T):
            t_in = S5_T - 1 - r if k == 0 else r
            t_out = r + 1 if k == 0 else S5_T - r
            rows = slice(S5_GR * r, S5_GR * (r + 1))
            pr, pi = pw_ref[p4, k, 0, t_in:t_in + 1, :], pw_ref[p4, k, 1, t_in:t_in + 1, :]
            min_scr[k, rows, 0:half] = bb_re * pr - bb_im * pi
            min_scr[k, rows, half:2 * half] = bb_re * pi + bb_im * pr
            pr, pi = pw_ref[p4, k, 0, t_out:t_out + 1, :], pw_ref[p4, k, 1, t_out:t_out + 1, :]
            out_blocks.append(jnp.concatenate([c_re * pr - c_im * pi, -(c_re * pi + c_im * pr)], axis=-1))
        mout.append(jnp.concatenate(out_blocks, axis=0).astype(BF16))
        c2 = jnp.concatenate([c_re, -c_im], axis=-1)
        c2_hi = c2.astype(BF16)
        c2_lo = (c2 - c2_hi.astype(F32)).astype(BF16)
        m = min_scr[k]
        m_hi = m.astype(BF16)
        m_lo = (m - m_hi.astype(F32)).astype(BF16)
        z = (lax.dot_general(m_hi, c2_hi, _NT, preferred_element_type=F32)
             + lax.dot_general(m_lo, c2_hi, _NT, preferred_element_type=F32)
             + lax.dot_general(m_hi, c2_lo, _NT, preferred_element_type=F32))
        zrep = jnp.dot(z.astype(BF16), rep, preferred_element_type=F32)
        zeros = jnp.zeros((S5_W, S5_W), F32)
        p_scr[0:S5_W, :] = zrep if k == 0 else zeros
        p_scr[S5_W:2 * S5_W, :] = zeros if k == 0 else zrep
        for j in range(S5_W // 128):
            cols = slice(128 * j, 128 * (j + 1))
            acc = None
            for s4 in range(4):
                s = 4 * j + s4
                start = S5_GR * (S5_T - 1 - s) if k == 0 else S5_W - S5_GR * s
                win = p_scr[start:start + S5_W, cols]
                acc = win if acc is None else jnp.where(gran == s4, win, acc)
            mi_scr[k, :, cols] = acc.astype(BF16)
    return mout


def _s5_scan(k, s_scrs, h_scrs, row0, nb, nchunks, hre, him, ar, ai):
    for n in (range(nchunks - 1, -1, -1) if k == 1 else range(nchunks)):
        rows = pl.ds(row0 + n, nb, stride=nchunks)
        h_scrs[0][rows, :] = hre
        h_scrs[1][rows, :] = him
        sre = s_scrs[0][rows, :]
        sim = s_scrs[1][rows, :]
        hre, him = ar * hre - ai * him + sre, ar * him + ai * hre + sim
    return hre, him


def _s5_kernel(x_ref, mod_ref, bb_ref, c_ref, pw_ref, h0_ref, y_ref, fin_ref,
               u_scr, yp_scr, sre_scr, sim_scr, hre_scr, him_scr, min_scr, mi_scr, p_scr):
    half = 2 * S5_P
    lane = lax.broadcasted_iota(jnp.int32, (S5_ROWS, 128), 1)
    scale = jnp.concatenate(
        [jnp.broadcast_to(1.0 + mod_ref[0, 1:2, :], (S5_ROWS_P, 128))]
        + [jnp.broadcast_to(1.0 + mod_ref[1 + b, 1:2, :], (S_CHUNKS, 128)) for b in range(DEC_BATCH)], axis=0)
    shift = jnp.concatenate(
        [jnp.broadcast_to(mod_ref[0, 0:1, :], (S5_ROWS_P, 128))]
        + [jnp.broadcast_to(mod_ref[1 + b, 0:1, :], (S_CHUNKS, 128)) for b in range(DEC_BATCH)], axis=0)
    for j in range(S5_T // 4):
        a = [x_ref[pl.ds(4 * j + s4, S5_ROWS, stride=S5_T), :] * scale + shift for s4 in range(4)]
        b = _granule_transpose4(a, lane)
        for p4 in range(S5_BLK):
            u_scr[p4, :, 128 * j:128 * (j + 1)] = b[p4].astype(BF16)

    for p4 in range(S5_BLK):
        mout = _s5_chunk_matrices(p4, bb_ref, c_ref, pw_ref, min_scr, mi_scr, p_scr)
        u = u_scr[p4]
        u0, u1 = u[:, 0:256], u[:, 256:512]
        y = None
        for k in range(2):
            s = jnp.dot(u, min_scr[k].astype(BF16), preferred_element_type=F32)
            sre_scr[...] = s[:, 0:half]
            sim_scr[...] = s[:, half:2 * half]
            s_scrs, h_scrs = (sre_scr, sim_scr), (hre_scr, him_scr)
            ar_p = jnp.broadcast_to(pw_ref[p4, k, 0, S5_T:S5_T + 1, :], (BATCH, half))
            ai_p = jnp.broadcast_to(pw_ref[p4, k, 1, S5_T:S5_T + 1, :], (BATCH, half))
            zero = jnp.zeros((BATCH, half), F32)
            fre, fim = _s5_scan(k, s_scrs, h_scrs, 0, BATCH, P_CHUNKS, zero, zero, ar_p, ai_p)
            fin_ref[p4, k, :, 0:half] = fre
            fin_ref[p4, k, :, half:2 * half] = fim
            _s5_scan(k, s_scrs, h_scrs, S5_ROWS_P, DEC_BATCH, S_CHUNKS,
                     h0_ref[p4, k, :, 0:half], h0_ref[p4, k, :, half:2 * half], ar_p[:DEC_BATCH], ai_p[:DEC_BATCH])
            hin = jnp.concatenate([hre_scr[...], him_scr[...]], axis=-1).astype(BF16)
            readout = lax.dot_general(hin, mout[k], _NT, preferred_element_type=F32)
            m00 = jnp.dot(u0, mi_scr[k, 0:256, 0:256], preferred_element_type=F32)
            m11 = jnp.dot(u1, mi_scr[k, 256:512, 256:512], preferred_element_type=F32)
            if k == 0:
                m11 = m11 + jnp.dot(u0, mi_scr[k, 0:256, 256:512], preferred_element_type=F32)
            else:
                m00 = m00 + jnp.dot(u1, mi_scr[k, 256:512, 0:256], preferred_element_type=F32)
            yk = jnp.concatenate([m00, m11], axis=-1) + readout
            y = yk if y is None else y + yk
        yp_scr[p4] = y

    for j in range(S5_T // 4):
        b = [yp_scr[p4, :, 128 * j:128 * (j + 1)] for p4 in range(S5_BLK)]
        a = _granule_transpose4(b, lane)
        for s4 in range(4):
            y_ref[pl.ds(4 * j + s4, S5_ROWS, stride=S5_T), :] = a[s4]


def _s5(x, mods, params, h0):
    bb, cc, pw = params
    nblk = S5_Q // S5_BLK
    return pl.pallas_call(
        _s5_kernel,
        grid=(nblk,),
        in_specs=[pl.BlockSpec((N_ROWS, 128), lambda q: (0, q)),
                  pl.BlockSpec((N_COND, 6, 128), lambda q: (0, 0, q)),
                  pl.BlockSpec((S5_BLK, 2, 2, S5_GR, 128), lambda q: (q, 0, 0, 0, 0)),
                  pl.BlockSpec((S5_BLK, 2, 2, S5_GR, 128), lambda q: (q, 0, 0, 0, 0)),
                  pl.BlockSpec((S5_BLK, 2, 2, S5_T + 1, 128), lambda q: (q, 0, 0, 0, 0)),
                  pl.BlockSpec((S5_BLK, 2, DEC_BATCH, 256), lambda q: (q, 0, 0, 0))],
        out_specs=[pl.BlockSpec((N_ROWS, 128), lambda q: (0, q)),
                   pl.BlockSpec((S5_BLK, 2, BATCH, 256), lambda q: (q, 0, 0, 0))],
        out_shape=[jax.ShapeDtypeStruct((N_ROWS, D), F32),
                   jax.ShapeDtypeStruct((S5_Q, 2, BATCH, 256), F32)],
        scratch_shapes=[pltpu.VMEM((S5_BLK, S5_ROWS, S5_W), BF16), pltpu.VMEM((S5_BLK, S5_ROWS, S5_W), F32),
                        pltpu.VMEM((S5_ROWS, 128), F32), pltpu.VMEM((S5_ROWS, 128), F32),
                        pltpu.VMEM((S5_ROWS, 128), F32), pltpu.VMEM((S5_ROWS, 128), F32),
                        pltpu.VMEM((2, S5_W, 256), F32),
                        pltpu.VMEM((2, S5_W, S5_W), BF16), pltpu.VMEM((2 * S5_W, S5_W), F32)],
        compiler_params=_cparams(("parallel",)),
        name="s5",
    )(x, mods, bb, cc, pw, h0)


def _s5_params(a_re, a_im, log_step, b_re, b_im, c_re, c_im):
    lam_re = jnp.minimum(a_re.astype(F32), -1e-4)
    lam_im = a_im.astype(F32)
    dt = jnp.exp(log_step.astype(F32))[..., None]
    j = jnp.arange(S5_T + 1, dtype=F32)[:, None, None, None]
    mag = jnp.exp(lam_re * dt * j)
    pw_re = mag * jnp.cos(lam_im * dt * j)
    pw_im = mag * jnp.sin(lam_im * dt * j)
    nr, ni = pw_re[1] - 1.0, pw_im[1]
    den = lam_re * lam_re + lam_im * lam_im
    q_re = (nr * lam_re + ni * lam_im) / den
    q_im = (ni * lam_re - nr * lam_im) / den
    b_re, b_im = b_re.astype(F32), b_im.astype(F32)
    bb_re = q_re[..., None] * b_re - q_im[..., None] * b_im
    bb_im = q_re[..., None] * b_im + q_im[..., None] * b_re
    eye = jnp.eye(2, dtype=F32)

    def pair_rows(m, spec):
        m = m.reshape(2, S5_Q, 2, *m.shape[2:])
        return jnp.einsum(spec, m, eye).reshape(S5_Q, 2, S5_GR, 2 * S5_P)

    bb = jnp.stack([pair_rows(bb_re, "kqgpc,gh->qkgchp"), pair_rows(bb_im, "kqgpc,gh->qkgchp")], axis=2)
    cc = jnp.stack([pair_rows(c_re.astype(F32), "kqgcp,gh->qkgchp"),
                    pair_rows(c_im.astype(F32), "kqgcp,gh->qkgchp")], axis=2)

    def pw_lay(p):
        return p.reshape(S5_T + 1, 2, S5_Q, 2 * S5_P).transpose(2, 1, 0, 3)
    pw = jnp.stack([pw_lay(pw_re), pw_lay(pw_im)], axis=2)
    return bb, cc, pw


def _s5_state_in(st_re, st_im):
    def lay(s):
        return s.astype(F32).reshape(DEC_BATCH, 2, S5_Q, 2 * S5_P).transpose(2, 1, 0, 3)
    return jnp.concatenate([lay(st_re), lay(st_im)], axis=-1)


def _s5_state_out(fin):
    def lay(s):
        return s.transpose(2, 1, 0, 3).reshape(BATCH, 2, S5_G, S5_P)
    return lay(fin[..., :2 * S5_P]), lay(fin[..., 2 * S5_P:])


def _glu_kernel(ys_ref, x_ref, mod_ref, d_ref, w_ref, g_ref, b_ref, x1_ref, h2_ref):
    x = x_ref[...]
    h = _modulate(x, mod_ref, 0, 1)
    y = jax.nn.gelu(ys_ref[...] + d_ref[...] * h).astype(BF16)
    z = jnp.dot(y, w_ref[...], preferred_element_type=F32)
    f = z[:, :D] * jax.nn.sigmoid(z[:, D:])
    x1 = _layer_norm(ALPHA * x + (1.0 + mod_ref[2:3, :]) * f, g_ref[...], b_ref[...])
    x1_ref[...] = x1
    h2_ref[...] = _modulate(x1, mod_ref, 3, 4).astype(BF16)


def _glu(layer, ysum, x, mod, d, w_glu, ln_g, ln_b):
    return pl.pallas_call(
        _glu_kernel,
        grid=(N_ROWS // TM,),
        in_specs=[_row_spec(TM), _row_spec(TM), _mod_spec(TM), _const_spec((1, D)),
                  _layer_spec(layer, (D, 2 * D)), _const_spec((1, D)), _const_spec((1, D))],
        out_specs=[_row_spec(TM), _row_spec(TM)],
        out_shape=[jax.ShapeDtypeStruct((N_ROWS, D), F32), jax.ShapeDtypeStruct((N_ROWS, D), BF16)],
        compiler_params=_cparams(("parallel",)),
        name="glu",
    )(ysum, x, mod, d.reshape(1, D), w_glu, ln_g.reshape(1, D), ln_b.reshape(1, D))


def _finish(x1, g, mod_ref, g_ref, b_ref, x2_ref):
    x2 = _layer_norm(ALPHA * x1 + (1.0 + mod_ref[5:6, :]) * g, g_ref[...], b_ref[...])
    x2_ref[...] = x2
    return x2


FFN_TF = FFN_DIM // 2


def _ffn_kernel(h_ref, x1_ref, mod_ref, nmod_ref, wg_ref, wu_ref, wd_ref, g_ref, b_ref, x2_ref, hn_ref):
    h = h_ref[...]
    acc = None
    for c in range(FFN_DIM // FFN_TF):
        sl = slice(c * FFN_TF, (c + 1) * FFN_TF)
        a = jnp.dot(h, wg_ref[:, sl], preferred_element_type=F32)
        b = jnp.dot(h, wu_ref[:, sl], preferred_element_type=F32)
        p = jnp.dot((jax.nn.silu(a) * b).astype(BF16), wd_ref[sl, :], preferred_element_type=F32)
        acc = p if acc is None else acc + p
    x2 = _finish(x1_ref[...], acc, mod_ref, g_ref, b_ref, x2_ref)
    hn_ref[...] = _modulate(x2, nmod_ref, 0, 1).astype(BF16)


def _ffn(layer, h2, x1, mod, nmod, wg, wu, wd, ln_g, ln_b):
    single = pl.Buffered(1)
    return pl.pallas_call(
        _ffn_kernel,
        grid=(N_ROWS // TM,),
        in_specs=[_row_spec(TM), _row_spec(TM), _mod_spec(TM), _mod_spec(TM),
                  _layer_spec(layer, (D, FFN_DIM), pipeline_mode=single),
                  _layer_spec(layer, (D, FFN_DIM), pipeline_mode=single),
                  _layer_spec(layer, (FFN_DIM, D), pipeline_mode=single),
                  _const_spec((1, D)), _const_spec((1, D))],
        out_specs=[_row_spec(TM), _row_spec(TM)],
        out_shape=[jax.ShapeDtypeStruct((N_ROWS, D), F32), jax.ShapeDtypeStruct((N_ROWS, D), BF16)],
        compiler_params=_cparams(("parallel",)),
        name="ffn",
    )(h2, x1, mod, nmod, wg, wu, wd, ln_g.reshape(1, D), ln_b.reshape(1, D))


def _gmlp_kernel(h_ref, x_ref, mod_ref, win_ref, bin_ref, vg_ref, vb_ref, ws_ref, bs_ref, wout_ref,
                 g_ref, b_ref, wr_ref, br_ref, x1_ref, h2_ref, idx_ref, p_ref, s_scr):
    z = jax.nn.gelu(jnp.dot(h_ref[...], win_ref[...], preferred_element_type=F32) + bin_ref[...])
    u = z[:, :D]
    v = _layer_norm(z[:, D:], vg_ref[...], vb_ref[...]).astype(BF16)
    n_chunks = TM // GM_CHUNK
    for hd in range(GM_HEADS):
        cols = slice(hd * GM_CHUNK, (hd + 1) * GM_CHUNK)
        rhs = jnp.concatenate([v[n * GM_CHUNK:(n + 1) * GM_CHUNK, cols] for n in range(n_chunks)], axis=-1)
        s = jnp.dot(ws_ref[hd], rhs, preferred_element_type=F32)
        for n in range(n_chunks):
            s_scr[n * GM_CHUNK:(n + 1) * GM_CHUNK, cols] = s[:, n * GM_CHUNK:(n + 1) * GM_CHUNK] + bs_ref[hd]
    f = jnp.dot((u * s_scr[...]).astype(BF16), wout_ref[...], preferred_element_type=F32)
    x1 = _layer_norm(ALPHA * x_ref[...] + (1.0 + mod_ref[2:3, :]) * f, g_ref[...], b_ref[...])
    x1_ref[...] = x1
    h2 = _modulate(x1, mod_ref, 3, 4)
    h2_ref[...] = h2
    h_hi = h2.astype(BF16)
    h_lo = (h2 - h_hi.astype(F32)).astype(BF16)
    w_hi, w_lo = wr_ref[0], wr_ref[1]
    logits = (jnp.dot(h_hi, w_hi, preferred_element_type=F32) + jnp.dot(h_lo, w_hi, preferred_element_type=F32)
              + jnp.dot(h_hi, w_lo, preferred_element_type=F32)) + br_ref[...]
    lane = lax.broadcasted_iota(jnp.int32, logits.shape, 1)
    neg = jnp.float32(-jnp.inf)
    logits = jnp.where(lane < N_EXPERTS, logits, neg)
    v1 = jnp.max(logits, axis=-1, keepdims=True)
    i1 = jnp.min(jnp.where(logits == v1, lane, 128), axis=-1, keepdims=True)
    rest = jnp.where(lane == i1, neg, logits)
    v2 = jnp.max(rest, axis=-1, keepdims=True)
    i2 = jnp.min(jnp.where(rest == v2, lane, 128), axis=-1, keepdims=True)
    e2 = jnp.exp(v2 - v1)
    p1 = 1.0 / (1.0 + e2)
    p2 = e2 / (1.0 + e2)
    idx_ref[...] = jnp.where(lane == 0, i1, jnp.where(lane == 1, i2, 0))
    p_ref[...] = jnp.where(lane == 0, p1, jnp.where(lane == 1, p2, 0.0))


def _gmlp(layer, h1, x, mod, w_in, b_in, vg, vb, w_s, b_s, w_out, ln_g, ln_b, w_r, b_r):
    return pl.pallas_call(
        _gmlp_kernel,
        grid=(N_ROWS // TM,),
        in_specs=[_row_spec(TM), _row_spec(TM), _mod_spec(TM),
                  _layer_spec(layer, (D, 2 * D)), _const_spec((1, 2 * D)), _const_spec((1, D)), _const_spec((1, D)),
                  _layer_spec(layer, (GM_HEADS, GM_CHUNK, GM_CHUNK)), _const_spec((GM_HEADS, GM_CHUNK, GM_CHUNK)),
                  _layer_spec(layer, (D, D)), _const_spec((1, D)), _const_spec((1, D)),
                  _const_spec((2, D, 128)), _const_spec((1, 128))],
        out_specs=[_row_spec(TM), _row_spec(TM), _row_spec(TM, 128), _row_spec(TM, 128)],
        out_shape=[jax.ShapeDtypeStruct((N_ROWS, D), F32), jax.ShapeDtypeStruct((N_ROWS, D), F32),
                   jax.ShapeDtypeStruct((N_ROWS, 128), jnp.int32), jax.ShapeDtypeStruct((N_ROWS, 128), F32)],
        scratch_shapes=[pltpu.VMEM((TM, D), F32)],
        compiler_params=_cparams(("parallel",)),
        name="gmlp",
    )(h1, x, mod, w_in, b_in.reshape(1, 2 * D), vg.reshape(1, D), vb.reshape(1, D), w_s, b_s, w_out,
      ln_g.reshape(1, D), ln_b.reshape(1, D), w_r, b_r)


def _moe_kernel(kind_ref, chunk_ref, ux_ref, uo_ref, ue_ref, pref_ref, src_ref, h_hbm, wg_ref, wu_ref, wd_ref, yt_hbm,
                acc_ref, xg_ref, sem, ybuf, osem, xb_ref, act_ref, wgb, wub, wdb):
    u = pl.program_id(0)
    nf = EXPERT_DIM // MOE_TF
    kind = kind_ref[u]
    c = chunk_ref[u]
    tile = ux_ref[u]
    slot = tile % MOE_SLOTS
    ahead_slot = (tile + MOE_AHEAD) % MOE_SLOTS
    oslot = tile % 2
    begins_tile = jnp.logical_or(kind == MOE_CACHED, jnp.logical_and(kind == MOE_STREAM, c == 0))

    def row_copy(t, i, dst_slot):
        row = jnp.minimum(src_ref[(t + 1) * MOE_TM + i] // TOP_K, N_ROWS - 1)
        return pltpu.make_async_copy(h_hbm.at[pl.ds(row, 1), :], xg_ref.at[dst_slot, pl.ds(i, 1), :], sem.at[dst_slot])

    def out_copy(q, i, src_slot):
        return pltpu.make_async_copy(ybuf.at[src_slot, pl.ds(i, 1), :], yt_hbm.at[pl.ds(q, 1), :], osem.at[src_slot])

    def wait_out(src_slot):
        pltpu.make_async_copy(ybuf.at[src_slot], yt_hbm.at[pl.ds(0, MOE_TM), :], osem.at[src_slot]).wait()

    def start_prev_out():
        for i in range(MOE_TM):
            out_copy(src_ref[tile * MOE_TM + i], i, 1 - oslot).start()

    @pl.when(u == 0)
    def _():
        ybuf[...] = jnp.zeros_like(ybuf)
        for t in range(MOE_AHEAD):
            def start_row(i, carry, t=t):
                row_copy(t, i, t).start()
                return carry
            lax.fori_loop(0, MOE_TM, start_row, 0)

    def wait_rows(dst_slot):
        pltpu.make_async_copy(h_hbm.at[pl.ds(0, MOE_TM), :], xg_ref.at[dst_slot], sem.at[dst_slot]).wait()

    def start_tile_ahead():
        nxt = pref_ref[u]
        for i in range(MOE_TM):
            row_copy(nxt, i, ahead_slot).start()

    @pl.when(begins_tile)
    def _():
        wait_rows(slot)

        @pl.when(tile >= 1)
        def _():
            wait_out(oslot)

    @pl.when(pref_ref[u] == MOE_DRAIN)
    def _():
        for k in range(1, MOE_AHEAD + 1):
            wait_rows((tile + k) % MOE_SLOTS)

        def send_row(i, carry):
            out_copy(src_ref[(tile + 1) * MOE_TM + i], i, oslot).start()
            return carry
        lax.fori_loop(0, MOE_TM, send_row, 0)
        wait_out(1 - oslot)
        wait_out(oslot)

        def fill_row(i, carry):
            out_copy(YT_REAL + i, i, oslot).start()
            return carry
        lax.fori_loop(0, MOE_TM, fill_row, 0)
        wait_out(oslot)

    def swiglu_chunk(x, k):
        a = jnp.dot(x, wgb[k], preferred_element_type=F32)
        b = jnp.dot(x, wub[k], preferred_element_type=F32)
        return (jax.nn.silu(a) * b).astype(BF16)

    @pl.when(kind == MOE_STREAM)
    def _():
        rows_c = pl.ds(pl.multiple_of(c * MOE_TF, MOE_TF), MOE_TF)
        wgb[c] = wg_ref[...].astype(BF16)
        wub[c] = wu_ref[...].astype(BF16)
        wdb[rows_c, :] = wd_ref[...].astype(BF16)

        @pl.when(c == 0)
        def _():
            xb_ref[...] = xg_ref[slot].astype(BF16)
            start_tile_ahead()
            start_prev_out()

        p = jnp.dot(swiglu_chunk(xb_ref[...], c), wdb[rows_c, :], preferred_element_type=F32)
        @pl.when(c == 0)
        def _():
            acc_ref[...] = p

        @pl.when(jnp.logical_and(c > 0, c < nf - 1))
        def _():
            acc_ref[...] += p

        @pl.when(c == nf - 1)
        def _():
            ybuf[oslot] = acc_ref[...] + p

    @pl.when(kind == MOE_CACHED)
    def _():
        x = xg_ref[slot].astype(BF16)
        start_tile_ahead()
        start_prev_out()
        for k in range(nf):
            act_ref[:, k * MOE_TF:(k + 1) * MOE_TF] = swiglu_chunk(x, k)
        ybuf[oslot] = jnp.dot(act_ref[...], wdb[...], preferred_element_type=F32)


def _moe(layer, units, src, h, wg, wu, wd):
    nf = EXPERT_DIM // MOE_TF

    def w_spec(shape, index):
        return pl.BlockSpec(shape, lambda u, kind, chunk, ux, uo, ue, pref, src_: index(ue[u], chunk[u]))

    grid_spec = pltpu.PrefetchScalarGridSpec(
        num_scalar_prefetch=7,
        grid=(MOE_UNITS,),
        in_specs=[pl.BlockSpec(memory_space=pl.ANY),
                  w_spec((None, None, D, MOE_TF), lambda e, k: (layer, e, 0, k)),
                  w_spec((None, None, D, MOE_TF), lambda e, k: (layer, e, 0, k)),
                  w_spec((None, None, MOE_TF, D), lambda e, k: (layer, e, k, 0))],
        out_specs=pl.BlockSpec(memory_space=pl.ANY),
        scratch_shapes=[pltpu.VMEM((MOE_TM, D), F32), pltpu.VMEM((MOE_SLOTS, MOE_TM, D), F32),
                        pltpu.SemaphoreType.DMA((MOE_SLOTS,)),
                        pltpu.VMEM((2, MOE_TM, D), F32), pltpu.SemaphoreType.DMA((2,)),
                        pltpu.VMEM((MOE_TM, D), BF16),
                        pltpu.VMEM((MOE_TM, EXPERT_DIM), BF16),
                        pltpu.VMEM((nf, D, MOE_TF), BF16), pltpu.VMEM((nf, D, MOE_TF), BF16),
                        pltpu.VMEM((EXPERT_DIM, D), BF16)],
    )
    return pl.pallas_call(
        _moe_kernel,
        grid_spec=grid_spec,
        out_shape=jax.ShapeDtypeStruct((YT_ROWS, D), F32),
        compiler_params=_cparams(("arbitrary",)),
        name="moe",
    )(*units, src, h, wg, wu, wd)


def _route(idx):
    e = idx.reshape(-1)
    onehot = (e[:, None] == jnp.arange(N_EXPERTS, dtype=jnp.int32)[None, :]).astype(jnp.int32)
    counts = jnp.sum(onehot, axis=0)
    rank = jnp.sum((jnp.cumsum(onehot, axis=0) - 1) * onehot, axis=1)
    padded = ((counts + MOE_TM - 1) // MOE_TM) * MOE_TM
    ends = jnp.cumsum(padded)
    starts = ends - padded
    pos = starts[e] + rank
    flat = jnp.arange(N_ROWS * TOP_K, dtype=jnp.int32)
    rows = jnp.arange(MOE_ROWS, dtype=jnp.int32)
    spare_rows = YT_REAL + ((rows // MOE_TM) % 2) * MOE_TM + rows % MOE_TM
    src = spare_rows.at[pos].set(flat, unique_indices=True, mode="promise_in_bounds")
    src = jnp.concatenate([YT_REAL + MOE_TM + jnp.arange(MOE_TM, dtype=jnp.int32), src])
    n_used = (ends[-1] // MOE_TM).astype(jnp.int32)
    tile_start = jnp.minimum(jnp.arange(MOE_TILES, dtype=jnp.int32), n_used - 1) * MOE_TM
    tile_expert = jnp.sum((tile_start[:, None] >= ends[None, :]).astype(jnp.int32), axis=1)
    tile_expert = jnp.minimum(tile_expert, N_EXPERTS - 1).astype(jnp.int32)
    prev = jnp.concatenate([jnp.full((1,), -1, jnp.int32), tile_expert[:-1]])
    tile_first = tile_expert != prev
    nf = EXPERT_DIM // MOE_TF
    tiles = jnp.arange(MOE_TILES, dtype=jnp.int32)
    live = tiles < n_used
    cost = jnp.where(live, jnp.where(tile_first, nf, 1), 0).astype(jnp.int32)
    off = jnp.cumsum(cost) - cost
    n_units = jnp.sum(cost)
    u = jnp.arange(MOE_UNITS, dtype=jnp.int32)
    tile_u = jnp.sum(jnp.logical_and(off[None, :] <= u[:, None], live[None, :]).astype(jnp.int32), axis=1) - 1
    first_u = tile_first[tile_u]
    work = u < n_units
    spare = u - n_units
    unit_kind = jnp.where(work, jnp.where(first_u, MOE_STREAM, MOE_CACHED),
                          jnp.where(spare < MOE_TILES - n_used, MOE_ZERO, MOE_IDLE)).astype(jnp.int32)
    unit_chunk = jnp.where(jnp.logical_and(work, first_u), u - off[tile_u], nf - 1).astype(jnp.int32)
    unit_out = jnp.where(work, tile_u, jnp.minimum(n_used + spare, MOE_TILES - 1)).astype(jnp.int32)
    unit_x = tile_u.astype(jnp.int32)
    unit_expert = tile_expert[tile_u]
    unit_pref = jnp.where(work, jnp.minimum(tile_u + MOE_AHEAD, n_used - 1),
                          jnp.where(spare == 0, MOE_DRAIN, -1)).astype(jnp.int32)
    return src, pos.reshape(N_ROWS, TOP_K), (unit_kind, unit_chunk, unit_x, unit_out, unit_expert, unit_pref)


def _combine_kernel(y_ref, p_ref, x1_ref, mod_ref, g_ref, b_ref, *x2_refs):
    g = p_ref[:, 0:1] * y_ref[:, 0:D] + p_ref[:, 1:2] * y_ref[:, D:2 * D]
    x2 = _layer_norm(ALPHA * x1_ref[...] + (1.0 + mod_ref[5:6, :]) * g, g_ref[...], b_ref[...])
    if len(x2_refs) == 1:
        x2_refs[0][...] = x2
    else:
        is_latent = pl.program_id(0) * TM >= N_P

        @pl.when(jnp.logical_not(is_latent))
        def _():
            x2_refs[0][...] = x2

        @pl.when(is_latent)
        def _():
            x2_refs[1][...] = x2


def _combine(yt, p, x1, mod, ln_g, ln_b, split=False):
    y2 = yt.reshape(YT_ROWS // TOP_K, TOP_K * D)
    p_tiles = N_P // TM
    if split:
        out_specs = [pl.BlockSpec((TM, D), lambda t: (jnp.minimum(t, p_tiles - 1), 0)),
                     pl.BlockSpec((TM, D), lambda t: (jnp.maximum(t - p_tiles, 0), 0))]
        out_shape = [jax.ShapeDtypeStruct((N_P, D), F32), jax.ShapeDtypeStruct((N_S, D), F32)]
    else:
        out_specs, out_shape = _row_spec(TM), jax.ShapeDtypeStruct((N_ROWS, D), F32)
    return pl.pallas_call(
        _combine_kernel,
        grid=(N_ROWS // TM,),
        in_specs=[_row_spec(TM, TOP_K * D), _row_spec(TM, 128), _row_spec(TM), _mod_spec(TM),
                  _const_spec((1, D)), _const_spec((1, D))],
        out_specs=out_specs,
        out_shape=out_shape,
        compiler_params=_cparams(("arbitrary",) if split else ("parallel",)),
        name="combine",
    )(y2, p, x1, mod, ln_g.reshape(1, D), ln_b.reshape(1, D))


def _grid_pos_embed():
    rows = DEC_SEQ // GRID_W
    r, col = jnp.meshgrid(jnp.arange(rows, dtype=F32), jnp.arange(GRID_W, dtype=F32), indexing="ij")
    quarter = D // 4
    freq = 1.0 / (10000.0 ** (jnp.arange(quarter, dtype=F32) / quarter))

    def emb(p):
        ang = p.reshape(-1)[:, None] * freq[None, :]
        return jnp.concatenate([jnp.sin(ang), jnp.cos(ang)], axis=-1)

    return jnp.concatenate([emb(r), emb(col)], axis=-1)


def kernel(x_prompt, x_sample, c, state_ssm_re, state_ssm_im, c_ctx, w_ada, b_ada, ln_g, ln_b, s5_a_re, s5_a_im, s5_log_step, s5_b_re, s5_b_im, s5_c_re, s5_c_im, s5_d, s5_w_glu, gm_w_in, gm_b_in, gm_ln_g, gm_ln_b, gm_w_s, gm_b_s, gm_w_out, ffn_w_gate, ffn_w_up, ffn_w_down, moe_w_router, moe_b_router, moe_w_gate, moe_w_up, moe_w_down):
    cond8 = jnp.concatenate([c_ctx[None, :], c, jnp.zeros((N_COND - 1 - DEC_BATCH, D), F32)], axis=0)
    mods = _ada(cond8, w_ada, b_ada).reshape(DEPTH, N_COND, 6, D)

    x = _prep(x_prompt, x_sample, _grid_pos_embed())
    h = None
    w_glu, w_gate, w_up, w_down = (w.astype(BF16) for w in (s5_w_glu, ffn_w_gate, ffn_w_up, ffn_w_down))
    w_in, w_s, w_out = (w.astype(BF16) for w in (gm_w_in, gm_w_s, gm_w_out))

    fin_re, fin_im = [], []
    for i in range(DEPTH):
        j = i // 2
        mod = mods[i]
        nmod = mods[min(i + 1, DEPTH - 1)]
        if i % 2 == 0:
            params = _s5_params(s5_a_re[j], s5_a_im[j], s5_log_step[j], s5_b_re[j], s5_b_im[j], s5_c_re[j], s5_c_im[j])
            ysum, fin = _s5(x, mod, params, _s5_state_in(state_ssm_re[:, j], state_ssm_im[:, j]))
            fr, fi = _s5_state_out(fin)
            fin_re.append(fr)
            fin_im.append(fi)
            x1, h2 = _glu(j, ysum, x, mod, s5_d[j], w_glu, ln_g[i, 0], ln_b[i, 0])
            x, h = _ffn(j, h2, x1, mod, nmod, w_gate, w_up, w_down, ln_g[i, 1], ln_b[i, 1])
        else:
            b_s = jnp.broadcast_to(gm_b_s[j][:, :, None], (GM_HEADS, GM_CHUNK, GM_CHUNK)).astype(F32)
            w_r = jnp.pad(moe_w_router[j].astype(F32), ((0, 0), (0, 128 - N_EXPERTS)))
            w_r_hi = w_r.astype(BF16)
            w_r = jnp.stack([w_r_hi, (w_r - w_r_hi.astype(F32)).astype(BF16)], axis=0)
            b_r = jnp.pad(moe_b_router[j].astype(F32), (0, 128 - N_EXPERTS)).reshape(1, 128)
            x1, h2, idx, p = _gmlp(j, h, x, mod, w_in, gm_b_in[j], gm_ln_g[j], gm_ln_b[j],
                                   w_s, b_s, w_out, ln_g[i, 0], ln_b[i, 0], w_r, b_r)
            src, pos, units = _route(idx[:, :TOP_K])
            yt = _moe(j, units, src, h2, moe_w_gate, moe_w_up, moe_w_down)
            x = _combine(yt, p, x1, mod, ln_g[i, 1], ln_b[i, 1], split=(i == DEPTH - 1))

    y_prompt = x[0].reshape(BATCH, SEQ, D)
    y_sample = x[1].reshape(DEC_BATCH, DEC_SEQ, D)
    new_re = jnp.stack(fin_re, axis=1).astype(x_prompt.dtype)
    new_im = jnp.stack(fin_im, axis=1).astype(x_prompt.dtype)
    return (y_prompt, y_sample, new_re, new_im)
```

```python
import jax
import jax.numpy as jnp
from jax import lax
from jax.experimental import pallas as pl
from jax.experimental.pallas import tpu as pltpu

F32 = jnp.float32
BF16 = jnp.bfloat16

D = 1024
BATCH, SEQ = 32, 256
DEC_BATCH, DEC_SEQ = 4, 1024
DEPTH = 4
GRID_W = 64
N_P = BATCH * SEQ
N_S = DEC_BATCH * DEC_SEQ
N_ROWS = N_P + N_S
N_COND = 8

S5_CG, S5_P, S5_G = 16, 64, 64
S5_T = 16
S5_Q = S5_G // 2
P_CHUNKS = SEQ // S5_T
S_CHUNKS = DEC_SEQ // S5_T
S5_ROWS_P = P_CHUNKS * BATCH
S5_ROWS_S = S_CHUNKS * DEC_BATCH
S5_ROWS = S5_ROWS_P + S5_ROWS_S

GM_CHUNK, GM_HEADS = 128, 8
FFN_DIM = 2816
N_EXPERTS, TOP_K, EXPERT_DIM = 8, 2, 3584
ALPHA = (2.0 * DEPTH) ** 0.25
LN_EPS = 1e-5

TM = 512
MOE_TM = 512
MOE_TF = 512
MOE_TILES = (N_ROWS * TOP_K) // MOE_TM + N_EXPERTS
MOE_ROWS = MOE_TILES * MOE_TM
MOE_UNITS = N_EXPERTS * (EXPERT_DIM // MOE_TF) + MOE_TILES
MOE_STREAM, MOE_CACHED, MOE_ZERO, MOE_IDLE = 1, 2, 0, 3
MOE_DRAIN = -2
MOE_AHEAD = 2
MOE_SLOTS = MOE_AHEAD + 1
YT_REAL = N_ROWS * TOP_K
YT_ROWS = YT_REAL + 2 * MOE_TM
VMEM_LIMIT = 56 * 1024 * 1024


def _cparams(sem):
    return pltpu.CompilerParams(dimension_semantics=sem, vmem_limit_bytes=VMEM_LIMIT)


def _cond_of_tile(t, tm):
    row0 = t * tm
    return jnp.where(row0 < N_P, 0, 1 + (row0 - N_P) // DEC_SEQ)


def _mod_spec(tm):
    return pl.BlockSpec((None, 6, D), lambda t: (_cond_of_tile(t, tm), 0, 0))


def _row_spec(tm, width=D):
    return pl.BlockSpec((tm, width), lambda t: (t, 0))


def _const_spec(shape):
    return pl.BlockSpec(shape, lambda t: (0,) * len(shape))


def _layer_spec(layer, shape, **kwargs):
    return pl.BlockSpec((None, *shape), lambda t: (layer,) + (0,) * len(shape), **kwargs)


def _layer_norm(r, g, b):
    mu = jnp.mean(r, axis=-1, keepdims=True)
    c = r - mu
    var = jnp.mean(c * c, axis=-1, keepdims=True)
    return c * lax.rsqrt(var + LN_EPS) * g + b


def _modulate(x, mod_ref, shift_row, scale_row):
    return x * (1.0 + mod_ref[scale_row:scale_row + 1, :]) + mod_ref[shift_row:shift_row + 1, :]


def _ada_kernel(c_ref, w_ref, b_ref, o_ref):
    c = c_ref[...]
    o_ref[...] = jnp.dot(jax.nn.silu(c), w_ref[...], preferred_element_type=F32) + b_ref[...]


def _ada(cond8, w_ada, b_ada):
    tn = 1536
    return pl.pallas_call(
        _ada_kernel,
        grid=(DEPTH, 6 * D // tn),
        in_specs=[pl.BlockSpec((N_COND, D), lambda i, n: (0, 0)),
                  pl.BlockSpec((None, D, tn), lambda i, n: (i, 0, n)),
                  pl.BlockSpec((None, 1, tn), lambda i, n: (i, 0, n))],
        out_specs=pl.BlockSpec((None, N_COND, tn), lambda i, n: (i, 0, n)),
        out_shape=jax.ShapeDtypeStruct((DEPTH, N_COND, 6 * D), F32),
        compiler_params=_cparams(("arbitrary", "arbitrary")),
        name="ada",
    )(cond8, w_ada, b_ada.reshape(DEPTH, 1, 6 * D))


def _prep_kernel(xp_ref, xs_ref, pos_ref, xo_ref):
    is_latent = pl.program_id(0) * TM >= N_P
    xo_ref[...] = jnp.where(is_latent, xs_ref[...] + pos_ref[...], xp_ref[...])


def _prep(x_prompt, x_sample, pos):
    p_tiles = N_P // TM

    def latent_tile(t):
        return jnp.maximum(t - p_tiles, 0)

    return pl.pallas_call(
        _prep_kernel,
        grid=(N_ROWS // TM,),
        in_specs=[pl.BlockSpec((TM, D), lambda t: (jnp.minimum(t, p_tiles - 1), 0)),
                  pl.BlockSpec((TM, D), lambda t: (latent_tile(t), 0)),
                  pl.BlockSpec((TM, D), lambda t: (latent_tile(t) % (DEC_SEQ // TM), 0))],
        out_specs=_row_spec(TM),
        out_shape=jax.ShapeDtypeStruct((N_ROWS, D), F32),
        compiler_params=_cparams(("parallel",)),
        name="prep",
    )(x_prompt.reshape(N_P, D), x_sample.reshape(N_S, D), pos)


S5_BLK = 4
S5_W = 2 * S5_T * S5_CG
S5_GR = 2 * S5_CG
_NT = (((1,), (1,)), ((), ()))


def _granule_transpose4(a, lane):
    low = lane < 64
    a0 = jnp.where(low, a[0], pltpu.roll(a[2], 64, 1))
    a2 = jnp.where(low, pltpu.roll(a[0], 64, 1), a[2])
    a1 = jnp.where(low, a[1], pltpu.roll(a[3], 64, 1))
    a3 = jnp.where(low, pltpu.roll(a[1], 64, 1), a[3])
    even = (lane // S5_GR) % 2 == 0
    return [jnp.where(even, a0, pltpu.roll(a1, 32, 1)), jnp.where(even, pltpu.roll(a0, 96, 1), a1),
            jnp.where(even, a2, pltpu.roll(a3, 32, 1)), jnp.where(even, pltpu.roll(a2, 96, 1), a3)]


def _s5_chunk_matrices(p4, bb_ref, c_ref, pw_ref, min_scr, mi_scr, p_scr):
    half = 2 * S5_P
    gran = lax.broadcasted_iota(jnp.int32, (S5_W, 128), 1) // S5_GR
    rep = (lax.broadcasted_iota(jnp.int32, (S5_GR, S5_W), 1) % S5_GR
           == lax.broadcasted_iota(jnp.int32, (S5_GR, S5_W), 0)).astype(BF16)
    mout = []
    for k in range(2):
        bb_re, bb_im = bb_ref[p4, k, 0], bb_ref[p4, k, 1]
        c_re, c_im = c_ref[p4, k, 0], c_ref[p4, k, 1]
        out_blocks = []
        for r in range(S5_T):
            t_in = S5_T - 1 - r if k == 0 else r
            t_out = r + 1 if k == 0 else S5_T - r
            rows = slice(S5_GR * r, S5_GR * (r + 1))
            pr, pi = pw_ref[p4, k, 0, t_in:t_in + 1, :], pw_ref[p4, k, 1, t_in:t_in + 1, :]
            min_scr[k, rows, 0:half] = bb_re * pr - bb_im * pi
            min_scr[k, rows, half:2 * half] = bb_re * pi + bb_im * pr
            pr, pi = pw_ref[p4, k, 0, t_out:t_out + 1, :], pw_ref[p4, k, 1, t_out:t_out + 1, :]
            out_blocks.append(jnp.concatenate([c_re * pr - c_im * pi, -(c_re * pi + c_im * pr)], axis=-1))
        mout.append(jnp.concatenate(out_blocks, axis=0).astype(BF16))
        c2 = jnp.concatenate([c_re, -c_im], axis=-1)
        c2_hi = c2.astype(BF16)
        c2_lo = (c2 - c2_hi.astype(F32)).astype(BF16)
        m = min_scr[k]
        m_hi = m.astype(BF16)
        m_lo = (m - m_hi.astype(F32)).astype(BF16)
        z = (lax.dot_general(m_hi, c2_hi, _NT, preferred_element_type=F32)
             + lax.dot_general(m_lo, c2_hi, _NT, preferred_element_type=F32)
             + lax.dot_general(m_hi, c2_lo, _NT, preferred_element_type=F32))
        zrep = jnp.dot(z.astype(BF16), rep, preferred_element_type=F32)
        zeros = jnp.zeros((S5_W, S5_W), F32)
        p_scr[0:S5_W, :] = zrep if k == 0 else zeros
        p_scr[S5_W:2 * S5_W, :] = zeros if k == 0 else zrep
        for j in range(S5_W // 128):
            cols = slice(128 * j, 128 * (j + 1))
            acc = None
            for s4 in range(4):
                s = 4 * j + s4
                start = S5_GR * (S5_T - 1 - s) if k == 0 else S5_W - S5_GR * s
                win = p_scr[start:start + S5_W, cols]
                acc = win if acc is None else jnp.where(gran == s4, win, acc)
            mi_scr[k, :, cols] = acc.astype(BF16)
    return mout


def _s5_scan(k, s_scrs, h_scrs, row0, nb, nchunks, hre, him, ar, ai):
    for n in (range(nchunks - 1, -1, -1) if k == 1 else range(nchunks)):
        rows = pl.ds(row0 + n, nb, stride=nchunks)
        h_scrs[0][rows, :] = hre
        h_scrs[1][rows, :] = him
        sre = s_scrs[0][rows, :]
        sim = s_scrs[1][rows, :]
        hre, him = ar * hre - ai * him + sre, ar * him + ai * hre + sim
    return hre, him


def _s5_kernel(x_ref, mod_ref, bb_ref, c_ref, pw_ref, h0_ref, y_ref, fin_ref,
               u_scr, yp_scr, sre_scr, sim_scr, hre_scr, him_scr, min_scr, mi_scr, p_scr):
    half = 2 * S5_P
    lane = lax.broadcasted_iota(jnp.int32, (S5_ROWS, 128), 1)
    scale = jnp.concatenate(
        [jnp.broadcast_to(1.0 + mod_ref[0, 1:2, :], (S5_ROWS_P, 128))]
        + [jnp.broadcast_to(1.0 + mod_ref[1 + b, 1:2, :], (S_CHUNKS, 128)) for b in range(DEC_BATCH)], axis=0)
    shift = jnp.concatenate(
        [jnp.broadcast_to(mod_ref[0, 0:1, :], (S5_ROWS_P, 128))]
        + [jnp.broadcast_to(mod_ref[1 + b, 0:1, :], (S_CHUNKS, 128)) for b in range(DEC_BATCH)], axis=0)
    for j in range(S5_T // 4):
        a = [x_ref[pl.ds(4 * j + s4, S5_ROWS, stride=S5_T), :] * scale + shift for s4 in range(4)]
        b = _granule_transpose4(a, lane)
        for p4 in range(S5_BLK):
            u_scr[p4, :, 128 * j:128 * (j + 1)] = b[p4].astype(BF16)

    for p4 in range(S5_BLK):
        mout = _s5_chunk_matrices(p4, bb_ref, c_ref, pw_ref, min_scr, mi_scr, p_scr)
        u = u_scr[p4]
        u0, u1 = u[:, 0:256], u[:, 256:512]
        y = None
        for k in range(2):
            s = jnp.dot(u, min_scr[k].astype(BF16), preferred_element_type=F32)
            sre_scr[...] = s[:, 0:half]
            sim_scr[...] = s[:, half:2 * half]
            s_scrs, h_scrs = (sre_scr, sim_scr), (hre_scr, him_scr)
            ar_p = jnp.broadcast_to(pw_ref[p4, k, 0, S5_T:S5_T + 1, :], (BATCH, half))
            ai_p = jnp.broadcast_to(pw_ref[p4, k, 1, S5_T:S5_T + 1, :], (BATCH, half))
            zero = jnp.zeros((BATCH, half), F32)
            fre, fim = _s5_scan(k, s_scrs, h_scrs, 0, BATCH, P_CHUNKS, zero, zero, ar_p, ai_p)
            fin_ref[p4, k, :, 0:half] = fre
            fin_ref[p4, k, :, half:2 * half] = fim
            _s5_scan(k, s_scrs, h_scrs, S5_ROWS_P, DEC_BATCH, S_CHUNKS,
                     h0_ref[p4, k, :, 0:half], h0_ref[p4, k, :, half:2 * half], ar_p[:DEC_BATCH], ai_p[:DEC_BATCH])
            hin = jnp.concatenate([hre_scr[...], him_scr[...]], axis=-1).astype(BF16)
            readout = lax.dot_general(hin, mout[k], _NT, preferred_element_type=F32)
            m00 = jnp.dot(u0, mi_scr[k, 0:256, 0:256], preferred_element_type=F32)
            m11 = jnp.dot(u1, mi_scr[k, 256:512, 256:512], preferred_element_type=F32)
            if k == 0:
                m11 = m11 + jnp.dot(u0, mi_scr[k, 0:256, 256:512], preferred_element_type=F32)
            else:
                m00 = m00 + jnp.dot(u1, mi_scr[k, 256:512, 0:256], preferred_element_type=F32)
            yk = jnp.concatenate([m00, m11], axis=-1) + readout
            y = yk if y is None else y + yk
        yp_scr[p4] = y

    for j in range(S5_T // 4):
        b = [yp_scr[p4, :, 128 * j:128 * (j + 1)] for p4 in range(S5_BLK)]
        a = _granule_transpose4(b, lane)
        for s4 in range(4):
            y_ref[pl.ds(4 * j + s4, S5_ROWS, stride=S5_T), :] = a[s4]


def _s5(x, mods, params, h0):
    bb, cc, pw = params
    nblk = S5_Q // S5_BLK
    return pl.pallas_call(
        _s5_kernel,
        grid=(nblk,),
        in_specs=[pl.BlockSpec((N_ROWS, 128), lambda q: (0, q)),
                  pl.BlockSpec((N_COND, 6, 128), lambda q: (0, 0, q)),
                  pl.BlockSpec((S5_BLK, 2, 2, S5_GR, 128), lambda q: (q, 0, 0, 0, 0)),
                  pl.BlockSpec((S5_BLK, 2, 2, S5_GR, 128), lambda q: (q, 0, 0, 0, 0)),
                  pl.BlockSpec((S5_BLK, 2, 2, S5_T + 1, 128), lambda q: (q, 0, 0, 0, 0)),
                  pl.BlockSpec((S5_BLK, 2, DEC_BATCH, 256), lambda q: (q, 0, 0, 0))],
        out_specs=[pl.BlockSpec((N_ROWS, 128), lambda q: (0, q)),
                   pl.BlockSpec((S5_BLK, 2, BATCH, 256), lambda q: (q, 0, 0, 0))],
        out_shape=[jax.ShapeDtypeStruct((N_ROWS, D), F32),
                   jax.ShapeDtypeStruct((S5_Q, 2, BATCH, 256), F32)],
        scratch_shapes=[pltpu.VMEM((S5_BLK, S5_ROWS, S5_W), BF16), pltpu.VMEM((S5_BLK, S5_ROWS, S5_W), F32),
                        pltpu.VMEM((S5_ROWS, 128), F32), pltpu.VMEM((S5_ROWS, 128), F32),
                        pltpu.VMEM((S5_ROWS, 128), F32), pltpu.VMEM((S5_ROWS, 128), F32),
                        pltpu.VMEM((2, S5_W, 256), F32),
                        pltpu.VMEM((2, S5_W, S5_W), BF16), pltpu.VMEM((2 * S5_W, S5_W), F32)],
        compiler_params=_cparams(("parallel",)),
        name="s5",
    )(x, mods, bb, cc, pw, h0)


def _s5_params(a_re, a_im, log_step, b_re, b_im, c_re, c_im):
    lam_re = jnp.minimum(a_re.astype(F32), -1e-4)
    lam_im = a_im.astype(F32)
    dt = jnp.exp(log_step.astype(F32))[..., None]
    j = jnp.arange(S5_T + 1, dtype=F32)[:, None, None, None]
    mag = jnp.exp(lam_re * dt * j)
    pw_re = mag * jnp.cos(lam_im * dt * j)
    pw_im = mag * jnp.sin(lam_im * dt * j)
    nr, ni = pw_re[1] - 1.0, pw_im[1]
    den = lam_re * lam_re + lam_im * lam_im
    q_re = (nr * lam_re + ni * lam_im) / den
    q_im = (ni * lam_re - nr * lam_im) / den
    b_re, b_im = b_re.astype(F32), b_im.astype(F32)
    bb_re = q_re[..., None] * b_re - q_im[..., None] * b_im
    bb_im = q_re[..., None] * b_im + q_im[..., None] * b_re
    eye = jnp.eye(2, dtype=F32)

    def pair_rows(m, spec):
        m = m.reshape(2, S5_Q, 2, *m.shape[2:])
        return jnp.einsum(spec, m, eye).reshape(S5_Q, 2, S5_GR, 2 * S5_P)

    bb = jnp.stack([pair_rows(bb_re, "kqgpc,gh->qkgchp"), pair_rows(bb_im, "kqgpc,gh->qkgchp")], axis=2)
    cc = jnp.stack([pair_rows(c_re.astype(F32), "kqgcp,gh->qkgchp"),
                    pair_rows(c_im.astype(F32), "kqgcp,gh->qkgchp")], axis=2)

    def pw_lay(p):
        return p.reshape(S5_T + 1, 2, S5_Q, 2 * S5_P).transpose(2, 1, 0, 3)
    pw = jnp.stack([pw_lay(pw_re), pw_lay(pw_im)], axis=2)
    return bb, cc, pw


def _s5_state_in(st_re, st_im):
    def lay(s):
        return s.astype(F32).reshape(DEC_BATCH, 2, S5_Q, 2 * S5_P).transpose(2, 1, 0, 3)
    return jnp.concatenate([lay(st_re), lay(st_im)], axis=-1)


def _s5_state_out(fin):
    def lay(s):
        return s.transpose(2, 1, 0, 3).reshape(BATCH, 2, S5_G, S5_P)
    return lay(fin[..., :2 * S5_P]), lay(fin[..., 2 * S5_P:])


def _glu_kernel(ys_ref, x_ref, mod_ref, d_ref, w_ref, g_ref, b_ref, x1_ref, h2_ref):
    x = x_ref[...]
    h = _modulate(x, mod_ref, 0, 1)
    y = jax.nn.gelu(ys_ref[...] + d_ref[...] * h).astype(BF16)
    z = jnp.dot(y, w_ref[...], preferred_element_type=F32)
    f = z[:, :D] * jax.nn.sigmoid(z[:, D:])
    x1 = _layer_norm(ALPHA * x + (1.0 + mod_ref[2:3, :]) * f, g_ref[...], b_ref[...])
    x1_ref[...] = x1
    h2_ref[...] = _modulate(x1, mod_ref, 3, 4).astype(BF16)


def _glu(layer, ysum, x, mod, d, w_glu, ln_g, ln_b):
    return pl.pallas_call(
        _glu_kernel,
        grid=(N_ROWS // TM,),
        in_specs=[_row_spec(TM), _row_spec(TM), _mod_spec(TM), _const_spec((1, D)),
                  _layer_spec(layer, (D, 2 * D)), _const_spec((1, D)), _const_spec((1, D))],
        out_specs=[_row_spec(TM), _row_spec(TM)],
        out_shape=[jax.ShapeDtypeStruct((N_ROWS, D), F32), jax.ShapeDtypeStruct((N_ROWS, D), BF16)],
        compiler_params=_cparams(("parallel",)),
        name="glu",
    )(ysum, x, mod, d.reshape(1, D), w_glu, ln_g.reshape(1, D), ln_b.reshape(1, D))


def _finish(x1, g, mod_ref, g_ref, b_ref, x2_ref):
    x2 = _layer_norm(ALPHA * x1 + (1.0 + mod_ref[5:6, :]) * g, g_ref[...], b_ref[...])
    x2_ref[...] = x2
    return x2


FFN_TF = FFN_DIM // 2


def _ffn_kernel(h_ref, x1_ref, mod_ref, nmod_ref, wg_ref, wu_ref, wd_ref, g_ref, b_ref, x2_ref, hn_ref):
    h = h_ref[...]
    acc = None
    for c in range(FFN_DIM // FFN_TF):
        sl = slice(c * FFN_TF, (c + 1) * FFN_TF)
        a = jnp.dot(h, wg_ref[:, sl], preferred_element_type=F32)
        b = jnp.dot(h, wu_ref[:, sl], preferred_element_type=F32)
        p = jnp.dot((jax.nn.silu(a) * b).astype(BF16), wd_ref[sl, :], preferred_element_type=F32)
        acc = p if acc is None else acc + p
    x2 = _finish(x1_ref[...], acc, mod_ref, g_ref, b_ref, x2_ref)
    hn_ref[...] = _modulate(x2, nmod_ref, 0, 1).astype(BF16)


def _ffn(layer, h2, x1, mod, nmod, wg, wu, wd, ln_g, ln_b):
    single = pl.Buffered(1)
    return pl.pallas_call(
        _ffn_kernel,
        grid=(N_ROWS // TM,),
        in_specs=[_row_spec(TM), _row_spec(TM), _mod_spec(TM), _mod_spec(TM),
                  _layer_spec(layer, (D, FFN_DIM), pipeline_mode=single),
                  _layer_spec(layer, (D, FFN_DIM), pipeline_mode=single),
                  _layer_spec(layer, (FFN_DIM, D), pipeline_mode=single),
                  _const_spec((1, D)), _const_spec((1, D))],
        out_specs=[_row_spec(TM), _row_spec(TM)],
        out_shape=[jax.ShapeDtypeStruct((N_ROWS, D), F32), jax.ShapeDtypeStruct((N_ROWS, D), BF16)],
        compiler_params=_cparams(("parallel",)),
        name="ffn",
    )(h2, x1, mod, nmod, wg, wu, wd, ln_g.reshape(1, D), ln_b.reshape(1, D))


def _gmlp_kernel(h_ref, x_ref, mod_ref, win_ref, bin_ref, vg_ref, vb_ref, ws_ref, bs_ref, wout_ref,
                 g_ref, b_ref, wr_ref, br_ref, x1_ref, h2_ref, idx_ref, p_ref, s_scr):
    z = jax.nn.gelu(jnp.dot(h_ref[...], win_ref[...], preferred_element_type=F32) + bin_ref[...])
    u = z[:, :D]
    v = _layer_norm(z[:, D:], vg_ref[...], vb_ref[...]).astype(BF16)
    n_chunks = TM // GM_CHUNK
    for hd in range(GM_HEADS):
        cols = slice(hd * GM_CHUNK, (hd + 1) * GM_CHUNK)
        rhs = jnp.concatenate([v[n * GM_CHUNK:(n + 1) * GM_CHUNK, cols] for n in range(n_chunks)], axis=-1)
        s = jnp.dot(ws_ref[hd], rhs, preferred_element_type=F32)
        for n in range(n_chunks):
            s_scr[n * GM_CHUNK:(n + 1) * GM_CHUNK, cols] = s[:, n * GM_CHUNK:(n + 1) * GM_CHUNK] + bs_ref[hd]
    f = jnp.dot((u * s_scr[...]).astype(BF16), wout_ref[...], preferred_element_type=F32)
    x1 = _layer_norm(ALPHA * x_ref[...] + (1.0 + mod_ref[2:3, :]) * f, g_ref[...], b_ref[...])
    x1_ref[...] = x1
    h2 = _modulate(x1, mod_ref, 3, 4)
    h2_ref[...] = h2
    h_hi = h2.astype(BF16)
    h_lo = (h2 - h_hi.astype(F32)).astype(BF16)
    w_hi, w_lo = wr_ref[0], wr_ref[1]
    logits = (jnp.dot(h_hi, w_hi, preferred_element_type=F32) + jnp.dot(h_lo, w_hi, preferred_element_type=F32)
              + jnp.dot(h_hi, w_lo, preferred_element_type=F32)) + br_ref[...]
    lane = lax.broadcasted_iota(jnp.int32, logits.shape, 1)
    neg = jnp.float32(-jnp.inf)
    logits = jnp.where(lane < N_EXPERTS, logits, neg)
    v1 = jnp.max(logits, axis=-1, keepdims=True)
    i1 = jnp.min(jnp.where(logits == v1, lane, 128), axis=-1, keepdims=True)
    rest = jnp.where(lane == i1, neg, logits)
    v2 = jnp.max(rest, axis=-1, keepdims=True)
    i2 = jnp.min(jnp.where(rest == v2, lane, 128), axis=-1, keepdims=True)
    e2 = jnp.exp(v2 - v1)
    p1 = 1.0 / (1.0 + e2)
    p2 = e2 / (1.0 + e2)
    idx_ref[...] = jnp.where(lane == 0, i1, jnp.where(lane == 1, i2, 0))
    p_ref[...] = jnp.where(lane == 0, p1, jnp.where(lane == 1, p2, 0.0))


def _gmlp(layer, h1, x, mod, w_in, b_in, vg, vb, w_s, b_s, w_out, ln_g, ln_b, w_r, b_r):
    return pl.pallas_call(
        _gmlp_kernel,
        grid=(N_ROWS // TM,),
        in_specs=[_row_spec(TM), _row_spec(TM), _mod_spec(TM),
                  _layer_spec(layer, (D, 2 * D)), _const_spec((1, 2 * D)), _const_spec((1, D)), _const_spec((1, D)),
                  _layer_spec(layer, (GM_HEADS, GM_CHUNK, GM_CHUNK)), _const_spec((GM_HEADS, GM_CHUNK, GM_CHUNK)),
                  _layer_spec(layer, (D, D)), _const_spec((1, D)), _const_spec((1, D)),
                  _const_spec((2, D, 128)), _const_spec((1, 128))],
        out_specs=[_row_spec(TM), _row_spec(TM), _row_spec(TM, 128), _row_spec(TM, 128)],
        out_shape=[jax.ShapeDtypeStruct((N_ROWS, D), F32), jax.ShapeDtypeStruct((N_ROWS, D), F32),
                   jax.ShapeDtypeStruct((N_ROWS, 128), jnp.int32), jax.ShapeDtypeStruct((N_ROWS, 128), F32)],
        scratch_shapes=[pltpu.VMEM((TM, D), F32)],
        compiler_params=_cparams(("parallel",)),
        name="gmlp",
    )(h1, x, mod, w_in, b_in.reshape(1, 2 * D), vg.reshape(1, D), vb.reshape(1, D), w_s, b_s, w_out,
      ln_g.reshape(1, D), ln_b.reshape(1, D), w_r, b_r)


def _moe_kernel(kind_ref, chunk_ref, ux_ref, uo_ref, ue_ref, pref_ref, src_ref, h_hbm, wg_ref, wu_ref, wd_ref, yt_hbm,
                acc_ref, xg_ref, sem, ybuf, osem, xb_ref, act_ref, wgb, wub, wdb):
    u = pl.program_id(0)
    nf = EXPERT_DIM // MOE_TF
    kind = kind_ref[u]
    c = chunk_ref[u]
    tile = ux_ref[u]
    slot = tile % MOE_SLOTS
    ahead_slot = (tile + MOE_AHEAD) % MOE_SLOTS
    oslot = tile % 2
    begins_tile = jnp.logical_or(kind == MOE_CACHED, jnp.logical_and(kind == MOE_STREAM, c == 0))

    def row_copy(t, i, dst_slot):
        q = src_ref[(t + 1) * MOE_TM + i]
        row = jnp.where(q >= 2 * N_ROWS, q - 2 * N_ROWS, jnp.where(q >= N_ROWS, q - N_ROWS, q))
        return pltpu.make_async_copy(h_hbm.at[pl.ds(row, 1), :], xg_ref.at[dst_slot, pl.ds(i, 1), :], sem.at[dst_slot])

    def out_copy(q, i, src_slot):
        return pltpu.make_async_copy(ybuf.at[src_slot, pl.ds(i, 1), :], yt_hbm.at[pl.ds(q, 1), :], osem.at[src_slot])

    def wait_out(src_slot):
        pltpu.make_async_copy(ybuf.at[src_slot], yt_hbm.at[pl.ds(0, MOE_TM), :], osem.at[src_slot]).wait()

    def start_prev_out():
        for i in range(MOE_TM):
            out_copy(src_ref[tile * MOE_TM + i], i, 1 - oslot).start()

    @pl.when(u == 0)
    def _():
        ybuf[...] = jnp.zeros_like(ybuf)
        for t in range(MOE_AHEAD):
            def start_row(i, carry, t=t):
                row_copy(t, i, t).start()
                return carry
            lax.fori_loop(0, MOE_TM, start_row, 0)

    def wait_rows(dst_slot):
        pltpu.make_async_copy(h_hbm.at[pl.ds(0, MOE_TM), :], xg_ref.at[dst_slot], sem.at[dst_slot]).wait()

    def start_tile_ahead():
        nxt = pref_ref[u]
        for i in range(MOE_TM):
            row_copy(nxt, i, ahead_slot).start()

    @pl.when(begins_tile)
    def _():
        wait_rows(slot)

        @pl.when(tile >= 1)
        def _():
            wait_out(oslot)

    @pl.when(pref_ref[u] == MOE_DRAIN)
    def _():
        for k in range(1, MOE_AHEAD + 1):
            wait_rows((tile + k) % MOE_SLOTS)

        def send_row(i, carry):
            out_copy(src_ref[(tile + 1) * MOE_TM + i], i, oslot).start()
            return carry
        lax.fori_loop(0, MOE_TM, send_row, 0)
        wait_out(1 - oslot)
        wait_out(oslot)

        def fill_row(i, carry):
            out_copy(YT_REAL + i, i, oslot).start()
            return carry
        lax.fori_loop(0, MOE_TM, fill_row, 0)
        wait_out(oslot)

    def swiglu_chunk(x, k):
        a = jnp.dot(x, wgb[k], preferred_element_type=F32)
        b = jnp.dot(x, wub[k], preferred_element_type=F32)
        return (jax.nn.silu(a) * b).astype(BF16)

    @pl.when(kind == MOE_STREAM)
    def _():
        rows_c = pl.ds(pl.multiple_of(c * MOE_TF, MOE_TF), MOE_TF)
        wgb[c] = wg_ref[...].astype(BF16)
        wub[c] = wu_ref[...].astype(BF16)
        wdb[rows_c, :] = wd_ref[...].astype(BF16)

        @pl.when(c == 0)
        def _():
            xb_ref[...] = xg_ref[slot].astype(BF16)
            start_tile_ahead()
            start_prev_out()

        p = jnp.dot(swiglu_chunk(xb_ref[...], c), wdb[rows_c, :], preferred_element_type=F32)
        @pl.when(c == 0)
        def _():
            acc_ref[...] = p

        @pl.when(jnp.logical_and(c > 0, c < nf - 1))
        def _():
            acc_ref[...] += p

        @pl.when(c == nf - 1)
        def _():
            ybuf[oslot] = acc_ref[...] + p

    @pl.when(kind == MOE_CACHED)
    def _():
        x = xg_ref[slot].astype(BF16)
        start_tile_ahead()
        start_prev_out()
        for k in range(nf):
            act_ref[:, k * MOE_TF:(k + 1) * MOE_TF] = swiglu_chunk(x, k)
        ybuf[oslot] = jnp.dot(act_ref[...], wdb[...], preferred_element_type=F32)


def _moe(layer, units, src, h, wg, wu, wd):
    nf = EXPERT_DIM // MOE_TF

    def w_spec(shape, index):
        return pl.BlockSpec(shape, lambda u, kind, chunk, ux, uo, ue, pref, src_: index(ue[u], chunk[u]))

    grid_spec = pltpu.PrefetchScalarGridSpec(
        num_scalar_prefetch=7,
        grid=(MOE_UNITS,),
        in_specs=[pl.BlockSpec(memory_space=pl.ANY),
                  w_spec((None, None, D, MOE_TF), lambda e, k: (layer, e, 0, k)),
                  w_spec((None, None, D, MOE_TF), lambda e, k: (layer, e, 0, k)),
                  w_spec((None, None, MOE_TF, D), lambda e, k: (layer, e, k, 0))],
        out_specs=pl.BlockSpec(memory_space=pl.ANY),
        scratch_shapes=[pltpu.VMEM((MOE_TM, D), F32), pltpu.VMEM((MOE_SLOTS, MOE_TM, D), F32),
                        pltpu.SemaphoreType.DMA((MOE_SLOTS,)),
                        pltpu.VMEM((2, MOE_TM, D), F32), pltpu.SemaphoreType.DMA((2,)),
                        pltpu.VMEM((MOE_TM, D), BF16),
                        pltpu.VMEM((MOE_TM, EXPERT_DIM), BF16),
                        pltpu.VMEM((nf, D, MOE_TF), BF16), pltpu.VMEM((nf, D, MOE_TF), BF16),
                        pltpu.VMEM((EXPERT_DIM, D), BF16)],
    )
    return pl.pallas_call(
        _moe_kernel,
        grid_spec=grid_spec,
        out_shape=jax.ShapeDtypeStruct((YT_ROWS, D), F32),
        compiler_params=_cparams(("arbitrary",)),
        name="moe",
    )(*units, src, h, wg, wu, wd)


def _route(idx):
    e = idx.reshape(-1)
    onehot = (e[:, None] == jnp.arange(N_EXPERTS, dtype=jnp.int32)[None, :]).astype(jnp.int32)
    counts = jnp.sum(onehot, axis=0)
    rank = jnp.sum((jnp.cumsum(onehot, axis=0) - 1) * onehot, axis=1)
    padded = ((counts + MOE_TM - 1) // MOE_TM) * MOE_TM
    ends = jnp.cumsum(padded)
    starts = ends - padded
    pos = starts[e] + rank
    flat = jnp.arange(N_ROWS * TOP_K, dtype=jnp.int32)
    flat = (flat % TOP_K) * N_ROWS + flat // TOP_K
    rows = jnp.arange(MOE_ROWS, dtype=jnp.int32)
    spare_rows = YT_REAL + ((rows // MOE_TM) % 2) * MOE_TM + rows % MOE_TM
    src = spare_rows.at[pos].set(flat, unique_indices=True, mode="promise_in_bounds")
    src = jnp.concatenate([YT_REAL + MOE_TM + jnp.arange(MOE_TM, dtype=jnp.int32), src])
    n_used = (ends[-1] // MOE_TM).astype(jnp.int32)
    tile_start = jnp.minimum(jnp.arange(MOE_TILES, dtype=jnp.int32), n_used - 1) * MOE_TM
    tile_expert = jnp.sum((tile_start[:, None] >= ends[None, :]).astype(jnp.int32), axis=1)
    tile_expert = jnp.minimum(tile_expert, N_EXPERTS - 1).astype(jnp.int32)
    prev = jnp.concatenate([jnp.full((1,), -1, jnp.int32), tile_expert[:-1]])
    tile_first = tile_expert != prev
    nf = EXPERT_DIM // MOE_TF
    tiles = jnp.arange(MOE_TILES, dtype=jnp.int32)
    live = tiles < n_used
    cost = jnp.where(live, jnp.where(tile_first, nf, 1), 0).astype(jnp.int32)
    off = jnp.cumsum(cost) - cost
    n_units = jnp.sum(cost)
    u = jnp.arange(MOE_UNITS, dtype=jnp.int32)
    tile_u = jnp.sum(jnp.logical_and(off[None, :] <= u[:, None], live[None, :]).astype(jnp.int32), axis=1) - 1
    first_u = tile_first[tile_u]
    work = u < n_units
    spare = u - n_units
    unit_kind = jnp.where(work, jnp.where(first_u, MOE_STREAM, MOE_CACHED),
                          jnp.where(spare < MOE_TILES - n_used, MOE_ZERO, MOE_IDLE)).astype(jnp.int32)
    unit_chunk = jnp.where(jnp.logical_and(work, first_u), u - off[tile_u], nf - 1).astype(jnp.int32)
    unit_out = jnp.where(work, tile_u, jnp.minimum(n_used + spare, MOE_TILES - 1)).astype(jnp.int32)
    unit_x = tile_u.astype(jnp.int32)
    unit_expert = tile_expert[tile_u]
    unit_pref = jnp.where(work, jnp.minimum(tile_u + MOE_AHEAD, n_used - 1),
                          jnp.where(spare == 0, MOE_DRAIN, -1)).astype(jnp.int32)
    return src, pos.reshape(N_ROWS, TOP_K), (unit_kind, unit_chunk, unit_x, unit_out, unit_expert, unit_pref)


def _combine_kernel(ya_ref, yb_ref, p_ref, x1_ref, mod_ref, g_ref, b_ref, *x2_refs):
    g = p_ref[:, 0:1] * ya_ref[...] + p_ref[:, 1:2] * yb_ref[...]
    x2 = _layer_norm(ALPHA * x1_ref[...] + (1.0 + mod_ref[5:6, :]) * g, g_ref[...], b_ref[...])
    if len(x2_refs) == 1:
        x2_refs[0][...] = x2
    else:
        is_latent = pl.program_id(0) * TM >= N_P

        @pl.when(jnp.logical_not(is_latent))
        def _():
            x2_refs[0][...] = x2

        @pl.when(is_latent)
        def _():
            x2_refs[1][...] = x2


def _combine(yt, p, x1, mod, ln_g, ln_b, split=False):
    p_tiles = N_P // TM
    if split:
        out_specs = [pl.BlockSpec((TM, D), lambda t: (jnp.minimum(t, p_tiles - 1), 0)),
                     pl.BlockSpec((TM, D), lambda t: (jnp.maximum(t - p_tiles, 0), 0))]
        out_shape = [jax.ShapeDtypeStruct((N_P, D), F32), jax.ShapeDtypeStruct((N_S, D), F32)]
    else:
        out_specs, out_shape = _row_spec(TM), jax.ShapeDtypeStruct((N_ROWS, D), F32)
    return pl.pallas_call(
        _combine_kernel,
        grid=(N_ROWS // TM,),
        in_specs=[_row_spec(TM), pl.BlockSpec((TM, D), lambda t: (t + N_ROWS // TM, 0)),
                  _row_spec(TM, 128), _row_spec(TM), _mod_spec(TM), _const_spec((1, D)), _const_spec((1, D))],
        out_specs=out_specs,
        out_shape=out_shape,
        compiler_params=_cparams(("arbitrary",) if split else ("parallel",)),
        name="combine",
    )(yt, yt, p, x1, mod, ln_g.reshape(1, D), ln_b.reshape(1, D))


def _grid_pos_embed():
    rows = DEC_SEQ // GRID_W
    r, col = jnp.meshgrid(jnp.arange(rows, dtype=F32), jnp.arange(GRID_W, dtype=F32), indexing="ij")
    quarter = D // 4
    freq = 1.0 / (10000.0 ** (jnp.arange(quarter, dtype=F32) / quarter))

    def emb(p):
        ang = p.reshape(-1)[:, None] * freq[None, :]
        return jnp.concatenate([jnp.sin(ang), jnp.cos(ang)], axis=-1)

    return jnp.concatenate([emb(r), emb(col)], axis=-1)


def kernel(x_prompt, x_sample, c, state_ssm_re, state_ssm_im, c_ctx, w_ada, b_ada, ln_g, ln_b, s5_a_re, s5_a_im, s5_log_step, s5_b_re, s5_b_im, s5_c_re, s5_c_im, s5_d, s5_w_glu, gm_w_in, gm_b_in, gm_ln_g, gm_ln_b, gm_w_s, gm_b_s, gm_w_out, ffn_w_gate, ffn_w_up, ffn_w_down, moe_w_router, moe_b_router, moe_w_gate, moe_w_up, moe_w_down):
    cond8 = jnp.concatenate([c_ctx[None, :], c, jnp.zeros((N_COND - 1 - DEC_BATCH, D), F32)], axis=0)
    mods = _ada(cond8, w_ada, b_ada).reshape(DEPTH, N_COND, 6, D)

    x = _prep(x_prompt, x_sample, _grid_pos_embed())
    h = None
    w_glu, w_gate, w_up, w_down = (w.astype(BF16) for w in (s5_w_glu, ffn_w_gate, ffn_w_up, ffn_w_down))
    w_in, w_s, w_out = (w.astype(BF16) for w in (gm_w_in, gm_w_s, gm_w_out))

    fin_re, fin_im = [], []
    for i in range(DEPTH):
        j = i // 2
        mod = mods[i]
        nmod = mods[min(i + 1, DEPTH - 1)]
        if i % 2 == 0:
            params = _s5_params(s5_a_re[j], s5_a_im[j], s5_log_step[j], s5_b_re[j], s5_b_im[j], s5_c_re[j], s5_c_im[j])
            ysum, fin = _s5(x, mod, params, _s5_state_in(state_ssm_re[:, j], state_ssm_im[:, j]))
            fr, fi = _s5_state_out(fin)
            fin_re.append(fr)
            fin_im.append(fi)
            x1, h2 = _glu(j, ysum, x, mod, s5_d[j], w_glu, ln_g[i, 0], ln_b[i, 0])
            x, h = _ffn(j, h2, x1, mod, nmod, w_gate, w_up, w_down, ln_g[i, 1], ln_b[i, 1])
        else:
            b_s = jnp.broadcast_to(gm_b_s[j][:, :, None], (GM_HEADS, GM_CHUNK, GM_CHUNK)).astype(F32)
            w_r = jnp.pad(moe_w_router[j].astype(F32), ((0, 0), (0, 128 - N_EXPERTS)))
            w_r_hi = w_r.astype(BF16)
            w_r = jnp.stack([w_r_hi, (w_r - w_r_hi.astype(F32)).astype(BF16)], axis=0)
            b_r = jnp.pad(moe_b_router[j].astype(F32), (0, 128 - N_EXPERTS)).reshape(1, 128)
            x1, h2, idx, p = _gmlp(j, h, x, mod, w_in, gm_b_in[j], gm_ln_g[j], gm_ln_b[j],
                                   w_s, b_s, w_out, ln_g[i, 0], ln_b[i, 0], w_r, b_r)
            src, pos, units = _route(idx[:, :TOP_K])
            yt = _moe(j, units, src, h2, moe_w_gate, moe_w_up, moe_w_down)
            x = _combine(yt, p, x1, mod, ln_g[i, 1], ln_b[i, 1], split=(i == DEPTH - 1))

    y_prompt = x[0].reshape(BATCH, SEQ, D)
    y_sample = x[1].reshape(DEC_BATCH, DEC_SEQ, D)
    new_re = jnp.stack(fin_re, axis=1).astype(x_prompt.dtype)
    new_im = jnp.stack(fin_im, axis=1).astype(x_prompt.dtype)
    return (y_prompt, y_sample, new_re, new_im)
```
